```python
import math
import jax, jax.numpy as jnp
from jax import lax
import numpy as np

D_MODEL = 1024
BATCH = 4
SEQ = 4096
DEPTH = 4
DEC_BATCH = 32
DEC_SEQ = 8
PAST_LEN = 8192
PAGE_SIZE = 128

N_SUB = DEPTH // 2
MIX_HALF = D_MODEL // 2
S5_GROUP_CH = 16
S5_GROUPS = MIX_HALF // S5_GROUP_CH
S5_STATE = 64
S5_DT_MIN = 0.001
S5_DT_MAX = 0.1
GM_CHUNK = 128
GM_GROUPS = 4
GM_CH = MIX_HALF // GM_GROUPS
POOL_WINDOWS = (2, 4, 8, 16)
POOL_CH = MIX_HALF // len(POOL_WINDOWS)
POOL_BUF = max(POOL_WINDOWS) - 1
ATT_HEADS = 8
HEAD_DIM = MIX_HALF // ATT_HEADS
ROT_DIM = HEAD_DIM // 4
ROPE_THETA = 500000.0
MOBA_BLOCK = 256
MOBA_TOPK = 3
MOBA_QBLOCK = 64
N_EXPERTS = 32
TOPK = 4
D_FF = D_MODEL
SWIGLU_LIMIT = 7.0
SWIGLU_ALPHA = 1.702
MOE_ROWS = 64
DN_ALPHA = (2 * DEPTH) ** 0.25
DN_BETA = (8 * DEPTH) ** -0.25
LN_EPS = 1e-5

kernel_name = "hybrid_s5_gmlp_pool_moba_moe_decode_step"

F32 = jnp.float32


def _layer_norm(x, g, b=None):
    xf = x.astype(F32)
    mu = xf.mean(-1, keepdims=True)
    var = jnp.square(xf - mu).mean(-1, keepdims=True)
    y = (xf - mu) * lax.rsqrt(var + LN_EPS) * g.astype(F32)
    if b is not None:
        y = y + b.astype(F32)
    return y.astype(x.dtype)


def _ssm_combine(left, right):
    a_l, b_l = left
    a_r, b_r = right
    return a_r * a_l, a_r * b_l + b_r


def _s5(u, x0, lam_re, lam_im, b_re, b_im, c_re, c_im, d_skip, log_dt, w_glu, b_glu):
    n_b, n_s, n_c = u.shape
    uf = u.astype(F32).reshape(n_b, n_s, S5_GROUPS, S5_GROUP_CH)
    lam = lax.complex(lam_re.astype(F32), lam_im.astype(F32))
    dt = jnp.exp(log_dt.astype(F32))[:, None]
    lam_bar = jnp.exp(lam * dt)
    b_bar = ((lam_bar - 1.0) / lam)[..., None] * lax.complex(b_re.astype(F32), b_im.astype(F32))
    bu = lax.complex(jnp.einsum('gph,bsgh->bsgp', b_bar.real, uf),
                     jnp.einsum('gph,bsgh->bsgp', b_bar.imag, uf))
    x_init = lax.complex(x0[..., 0].astype(F32), x0[..., 1].astype(F32))
    bu = bu.at[:, 0].add(lam_bar * x_init)
    a = jnp.broadcast_to(lam_bar, bu.shape)
    _, states = lax.associative_scan(_ssm_combine, (a, bu), axis=1)
    c = lax.complex(c_re.astype(F32), c_im.astype(F32))
    y = jnp.einsum('ghp,bsgp->bsgh', c, states).real.reshape(n_b, n_s, n_c)
    y = y + d_skip.astype(F32) * u.astype(F32)
    g = jax.nn.gelu(y)
    out = g * jax.nn.sigmoid(g @ w_glu.astype(F32) + b_glu.astype(F32))
    last = states[:, -1]
    new_state = jnp.stack([last.real, last.imag], axis=-1)
    return out.astype(u.dtype), new_state.astype(x0.dtype)


def _gmlp(u, v, norm_g, w_s, b_s, chunk):
    n_b, n_s, n_c = v.shape
    vn = _layer_norm(v.reshape(n_b, n_s, GM_GROUPS, GM_CH), norm_g.reshape(GM_GROUPS, GM_CH))
    vc = vn.reshape(n_b, n_s // chunk, chunk, GM_GROUPS, GM_CH)
    w = jnp.tril(w_s[:, :chunk, :chunk])
    s = jnp.einsum('gij,bcjgd->bcigd', w, vc) + b_s[:, :chunk].T[:, :, None]
    out = u * s.reshape(n_b, n_s, n_c).astype(u.dtype)
    return out, vn.reshape(n_b, n_s, n_c)


def _pool_mix(ext, w, scale):
    n_b, n_l, n_c = ext.shape
    xf = ext.astype(F32)
    cs = jnp.cumsum(xf, axis=1)
    cs0 = jnp.pad(cs, ((0, 0), (1, 0), (0, 0)))
    t = jnp.arange(n_l)
    means = []
    for g, win in enumerate(POOL_WINDOWS):
        lo = jnp.maximum(t + 1 - win, 0)
        cnt = (t + 1 - lo).astype(F32)[None, :, None]
        seg = slice(g * POOL_CH, (g + 1) * POOL_CH)
        means.append((cs[:, :, seg] - cs0[:, lo, seg]) / cnt)
    pooled = jnp.concatenate(means, axis=-1) - xf
    y = jnp.einsum('bsgc,gcd->bsgd', pooled.reshape(n_b, n_l, len(POOL_WINDOWS), POOL_CH), w.astype(F32))
    return (y.reshape(n_b, n_l, n_c) * scale.astype(F32)).astype(ext.dtype)


def _rotary(x, pos):
    half = ROT_DIM // 2
    inv = ROPE_THETA ** (-jnp.arange(half, dtype=F32) * 2.0 / ROT_DIM)
    ang = pos.astype(F32)[:, None] * inv[None, :]
    cos = jnp.cos(ang)[None, :, None, :]
    sin = jnp.sin(ang)[None, :, None, :]
    xr = x[..., :ROT_DIM].astype(F32)
    x1, x2 = xr[..., :half], xr[..., half:]
    rot = jnp.concatenate([x1 * cos - x2 * sin, x2 * cos + x1 * sin], axis=-1).astype(x.dtype)
    return jnp.concatenate([rot, x[..., ROT_DIM:]], axis=-1)


def _moba(q, k_all, v_all, q_pos0):
    n_b, n_q, n_h, dh = q.shape
    n_l = k_all.shape[1]
    n_blk = -(-n_l // MOBA_BLOCK)
    pad = n_blk * MOBA_BLOCK - n_l

    def to_blocks(t):
        t = jnp.pad(t, ((0, 0), (0, pad), (0, 0), (0, 0)))
        return t.reshape(n_b, n_blk, MOBA_BLOCK, n_h, dh).transpose(0, 3, 1, 2, 4)

    k_bh = to_blocks(k_all)
    v_bh = to_blocks(v_all)
    k_mean = k_bh.astype(F32).mean(axis=3)
    k_top = min(MOBA_TOPK, n_blk)
    qb = min(MOBA_QBLOCK, n_q)
    n_qb = n_q // qb
    q_blocks = q.reshape(n_b, n_qb, qb, n_h, dh).transpose(1, 0, 3, 2, 4)
    starts = q_pos0 + jnp.arange(n_qb) * qb
    b_idx = jnp.arange(n_b)[:, None, None, None]
    h_idx = jnp.arange(n_h)[None, :, None, None]
    blk_pos = jnp.arange(MOBA_BLOCK)
    scale = HEAD_DIM ** -0.5

    def one_block(args):
        qblk, p0 = args
        qf = qblk.astype(F32) * scale
        own = p0 // MOBA_BLOCK
        q_pos = p0 + jnp.arange(qb)
        gate = jnp.einsum('bhqd,bhnd->bhqn', qf, k_mean)
        gate = jnp.where(jnp.arange(n_blk) < own, gate, -jnp.inf)
        _, sel = lax.top_k(gate, k_top)
        sel_valid = jnp.arange(k_top) < own
        k_sel = k_bh[b_idx, h_idx, sel].astype(F32)
        v_sel = v_bh[b_idx, h_idx, sel].astype(F32)
        s_sel = jnp.einsum('bhqd,bhqnjd->bhqnj', qf, k_sel)
        s_sel = jnp.where(sel_valid[:, None], s_sel, -jnp.inf)
        k_own = lax.dynamic_index_in_dim(k_bh, own, axis=2, keepdims=False).astype(F32)
        v_own = lax.dynamic_index_in_dim(v_bh, own, axis=2, keepdims=False).astype(F32)
        s_own = jnp.einsum('bhqd,bhjd->bhqj', qf, k_own)
        own_pos = own * MOBA_BLOCK + blk_pos
        s_own = jnp.where(own_pos[None, :] <= q_pos[:, None], s_own, -jnp.inf)
        s_all = jnp.concatenate([s_sel.reshape(n_b, n_h, qb, k_top * MOBA_BLOCK), s_own], axis=-1)
        prob = jax.nn.softmax(s_all, axis=-1)
        p_sel = prob[..., :k_top * MOBA_BLOCK].reshape(n_b, n_h, qb, k_top, MOBA_BLOCK)
        p_own = prob[..., k_top * MOBA_BLOCK:]
        o = (jnp.einsum('bhqnj,bhqnjd->bhqd', p_sel, v_sel)
             + jnp.einsum('bhqj,bhjd->bhqd', p_own, v_own))
        return o.astype(q.dtype)

    out = lax.map(one_block, (q_blocks, starts))
    return out.transpose(1, 0, 3, 2, 4).reshape(n_b, n_q, n_h * dh)


def _moe(x, w_r, b_r, w_gate, b_gate, w_up, b_up, w_down, b_down):
    n_b, n_s, d = x.shape
    xt = x.reshape(-1, d)
    n_tok = xt.shape[0]
    logits = (xt @ w_r).astype(F32) + b_r.astype(F32)
    top_val, top_idx = lax.top_k(logits, TOPK)
    gates = jax.nn.softmax(top_val, axis=-1)
    n_asg = n_tok * TOPK
    e_flat = top_idx.reshape(-1)
    order = jnp.argsort(e_flat)
    e_sorted = e_flat[order]
    tok_sorted = (order // TOPK).astype(jnp.int32)
    g_sorted = gates.reshape(-1)[order]
    counts = jnp.bincount(e_flat, length=N_EXPERTS)
    padded = (counts + MOE_ROWS - 1) // MOE_ROWS * MOE_ROWS
    start = jnp.cumsum(counts) - counts
    pend = jnp.cumsum(padded)
    pstart = pend - padded
    dest = pstart[e_sorted] + jnp.arange(n_asg) - start[e_sorted]
    n_blocks = -(-n_asg // MOE_ROWS) + N_EXPERTS
    n_rows = n_blocks * MOE_ROWS
    row_tok = jnp.zeros((n_rows,), jnp.int32).at[dest].set(tok_sorted)
    row_gate = jnp.zeros((n_rows,), F32).at[dest].set(g_sorted)
    blk_exp = jnp.minimum(jnp.searchsorted(pend, jnp.arange(n_blocks) * MOE_ROWS, side='right'),
                          N_EXPERTS - 1)

    def expert_block(args):
        e, toks = args
        xb = xt[toks]
        gt = jnp.minimum(xb @ w_gate[e] + b_gate[e], SWIGLU_LIMIT)
        up = jnp.clip(xb @ w_up[e] + b_up[e], -SWIGLU_LIMIT, SWIGLU_LIMIT)
        act = gt * jax.nn.sigmoid(SWIGLU_ALPHA * gt) * (up + 1.0)
        return act @ w_down[e] + b_down[e]

    rows = lax.map(expert_block, (blk_exp, row_tok.reshape(n_blocks, MOE_ROWS)))
    rows = rows.reshape(n_rows, d) * row_gate[:, None].astype(x.dtype)
    out = jax.ops.segment_sum(rows, row_tok, num_segments=n_tok)
    return out.reshape(n_b, n_s, d).astype(x.dtype)


def _trunk(x, pos0, s5_init, pool_buf, cache_k, cache_v, page_table, gm_chunk, p):
    n_b, n_s, _ = x.shape
    sample = cache_k is not None
    pos = pos0 + jnp.arange(n_s)
    s5_new, gmv_new, pool_new, k_new, v_new = [], [], [], [], []

    def heads(t):
        return t.reshape(n_b, n_s, ATT_HEADS, HEAD_DIM)

    for layer in range(DEPTH):
        i = layer // 2
        if layer % 2 == 0:
            a_in, g_u, g_v = jnp.split(x @ p['w_in_ab'][i], 3, axis=-1)
            a_out, s5_last = _s5(a_in, s5_init[i], p['s5_lambda_re'][i], p['s5_lambda_im'][i],
                                 p['s5_b_re'][i], p['s5_b_im'][i], p['s5_c_re'][i], p['s5_c_im'][i],
                                 p['s5_d'][i], p['s5_log_dt'][i], p['s5_w_glu'][i], p['s5_b_glu'][i])
            b_out, v_rows = _gmlp(jax.nn.gelu(g_u), jax.nn.gelu(g_v), p['gm_norm_g'][i],
                                  p['gm_w_s'][i], p['gm_b_s'][i], gm_chunk)
            mix = jnp.concatenate([a_out, b_out], axis=-1)
            s5_new.append(s5_last)
            if sample:
                gmv_new.append(v_rows)
        else:
            c_in, q, k, v = jnp.split(x @ p['w_in_cd'][i], 4, axis=-1)
            if pool_buf is None:
                ext = c_in
            else:
                ext = jnp.concatenate([pool_buf[i].astype(c_in.dtype), c_in], axis=1)
            c_out = _pool_mix(ext, p['pool_w'][i], p['pool_scale'][i])[:, -n_s:]
            pool_new.append(ext[:, -POOL_BUF:])
            q = _rotary(heads(q), pos)
            k = _rotary(heads(k), pos)
            v = heads(v)
            if sample:
                past_k = cache_k[i][page_table].reshape(n_b, -1, ATT_HEADS, HEAD_DIM)
                past_v = cache_v[i][page_table].reshape(n_b, -1, ATT_HEADS, HEAD_DIM)
                k_all = jnp.concatenate([past_k.astype(k.dtype), k], axis=1)
                v_all = jnp.concatenate([past_v.astype(v.dtype), v], axis=1)
            else:
                k_all, v_all = k, v
            d_out = _moba(q, k_all, v_all, pos0)
            mix = jnp.concatenate([c_out, d_out], axis=-1)
            k_new.append(k)
            v_new.append(v)
        x = _layer_norm(DN_ALPHA * x + mix @ p['w_out'][layer], p['ln_g'][layer, 0], p['ln_b'][layer, 0])
        ffn = _moe(x, p['router_w'][layer], p['router_b'][layer], p['moe_w_gate'][layer],
                   p['moe_b_gate'][layer], p['moe_w_up'][layer], p['moe_b_up'][layer],
                   p['moe_w_down'][layer], p['moe_b_down'][layer])
        x = _layer_norm(DN_ALPHA * x + ffn, p['ln_g'][layer, 1], p['ln_b'][layer, 1])
    gmv = jnp.stack(gmv_new) if sample else None
    return x, jnp.stack(s5_new), gmv, jnp.stack(pool_new), jnp.stack(k_new), jnp.stack(v_new)


def setup_inputs(seed: int = 0) -> dict:
    key = jax.random.key(seed)
    ks = jax.random.split(key, 40)

    def nrm(k, shape, scale):
        return jax.random.normal(k, shape, F32) * scale

    n_pages = PAST_LEN // PAGE_SIZE
    n_used = DEC_BATCH * n_pages
    n_pool = n_used + n_used // 4
    page_table = jax.random.permutation(ks[0], n_pool)[:n_used].reshape(DEC_BATCH, n_pages).astype(jnp.int32)
    lam_im = jnp.broadcast_to(jnp.pi * jnp.arange(S5_STATE, dtype=F32), (N_SUB, S5_GROUPS, S5_STATE))
    return {
        'x_prompt': nrm(ks[1], (BATCH, SEQ, D_MODEL), 1.0),
        'x_sample': nrm(ks[2], (DEC_BATCH, DEC_SEQ, D_MODEL), 1.0),
        'state_s5': nrm(ks[3], (N_SUB, DEC_BATCH, S5_GROUPS, S5_STATE, 2), 0.1),
        'state_pool': nrm(ks[4], (N_SUB, DEC_BATCH, POOL_BUF, MIX_HALF), 1.0),
        'cache_k': nrm(ks[5], (N_SUB, n_pool, PAGE_SIZE, ATT_HEADS, HEAD_DIM), 1.0),
        'cache_v': nrm(ks[6], (N_SUB, n_pool, PAGE_SIZE, ATT_HEADS, HEAD_DIM), 1.0),
        'page_table': page_table,
        'w_in_ab': nrm(ks[7], (N_SUB, D_MODEL, 3 * MIX_HALF), D_MODEL ** -0.5),
        's5_lambda_re': -0.5 + nrm(ks[8], (N_SUB, S5_GROUPS, S5_STATE), 0.01),
        's5_lambda_im': lam_im + nrm(ks[9], (N_SUB, S5_GROUPS, S5_STATE), 0.01),
        's5_b_re': nrm(ks[10], (N_SUB, S5_GROUPS, S5_STATE, S5_GROUP_CH), (2 * S5_GROUP_CH) ** -0.5),
        's5_b_im': nrm(ks[11], (N_SUB, S5_GROUPS, S5_STATE, S5_GROUP_CH), (2 * S5_GROUP_CH) ** -0.5),
        's5_c_re': nrm(ks[12], (N_SUB, S5_GROUPS, S5_GROUP_CH, S5_STATE), S5_STATE ** -0.5),
        's5_c_im': nrm(ks[13], (N_SUB, S5_GROUPS, S5_GROUP_CH, S5_STATE), S5_STATE ** -0.5),
        's5_d': nrm(ks[14], (N_SUB, MIX_HALF), 1.0),
        's5_log_dt': jax.random.uniform(ks[15], (N_SUB, S5_GROUPS), F32,
                                        math.log(S5_DT_MIN), math.log(S5_DT_MAX)),
        's5_w_glu': nrm(ks[16], (N_SUB, MIX_HALF, MIX_HALF), MIX_HALF ** -0.5),
        's5_b_glu': nrm(ks[17], (N_SUB, MIX_HALF), 0.02),
        'gm_norm_g': 1.0 + nrm(ks[18], (N_SUB, MIX_HALF), 0.02),
        'gm_w_s': nrm(ks[19], (N_SUB, GM_GROUPS, GM_CHUNK, GM_CHUNK), 0.5 * GM_CHUNK ** -0.5),
        'gm_b_s': 1.0 + nrm(ks[20], (N_SUB, GM_GROUPS, GM_CHUNK), 0.1),
        'w_in_cd': nrm(ks[21], (N_SUB, D_MODEL, 4 * MIX_HALF), D_MODEL ** -0.5),
        'pool_w': nrm(ks[22], (N_SUB, len(POOL_WINDOWS), POOL_CH, POOL_CH), POOL_CH ** -0.5),
        'pool_scale': 1.0 + nrm(ks[23], (N_SUB, MIX_HALF), 0.02),
        'w_out': nrm(ks[24], (DEPTH, 2 * MIX_HALF, D_MODEL), (2 * MIX_HALF) ** -0.5 * DN_BETA),
        'ln_g': 1.0 + nrm(ks[25], (DEPTH, 2, D_MODEL), 0.02),
        'ln_b': nrm(ks[26], (DEPTH, 2, D_MODEL), 0.02),
        'router_w': nrm(ks[27], (DEPTH, D_MODEL, N_EXPERTS), D_MODEL ** -0.5),
        'router_b': nrm(ks[28], (DEPTH, N_EXPERTS), 0.01),
        'moe_w_gate': nrm(ks[29], (DEPTH, N_EXPERTS, D_MODEL, D_FF), D_MODEL ** -0.5),
        'moe_b_gate': nrm(ks[30], (DEPTH, N_EXPERTS, D_FF), 0.02),
        'moe_w_up': nrm(ks[31], (DEPTH, N_EXPERTS, D_MODEL, D_FF), D_MODEL ** -0.5),
        'moe_b_up': nrm(ks[32], (DEPTH, N_EXPERTS, D_FF), 0.02),
        'moe_w_down': nrm(ks[33], (DEPTH, N_EXPERTS, D_FF, D_MODEL), D_FF ** -0.5 * DN_BETA),
        'moe_b_down': nrm(ks[34], (DEPTH, N_EXPERTS, D_MODEL), 0.02),
    }


def reference(x_prompt, x_sample, state_s5, state_pool, cache_k, cache_v, page_table,
              w_in_ab, s5_lambda_re, s5_lambda_im, s5_b_re, s5_b_im, s5_c_re, s5_c_im, s5_d,
              s5_log_dt, s5_w_glu, s5_b_glu, gm_norm_g, gm_w_s, gm_b_s, w_in_cd, pool_w,
              pool_scale, w_out, ln_g, ln_b, router_w, router_b, moe_w_gate, moe_b_gate,
              moe_w_up, moe_b_up, moe_w_down, moe_b_down):
    p = dict(w_in_ab=w_in_ab, s5_lambda_re=s5_lambda_re, s5_lambda_im=s5_lambda_im,
             s5_b_re=s5_b_re, s5_b_im=s5_b_im, s5_c_re=s5_c_re, s5_c_im=s5_c_im, s5_d=s5_d,
             s5_log_dt=s5_log_dt, s5_w_glu=s5_w_glu, s5_b_glu=s5_b_glu, gm_norm_g=gm_norm_g,
             gm_w_s=gm_w_s, gm_b_s=gm_b_s, w_in_cd=w_in_cd, pool_w=pool_w, pool_scale=pool_scale,
             w_out=w_out, ln_g=ln_g, ln_b=ln_b, router_w=router_w, router_b=router_b,
             moe_w_gate=moe_w_gate, moe_b_gate=moe_b_gate, moe_w_up=moe_w_up, moe_b_up=moe_b_up,
             moe_w_down=moe_w_down, moe_b_down=moe_b_down)
    zero_s5 = jnp.zeros((N_SUB, x_prompt.shape[0], S5_GROUPS, S5_STATE, 2), x_prompt.dtype)
    y_prompt, s5_p, _, pool_p, k_p, v_p = _trunk(
        x_prompt, 0, zero_s5, None, None, None, None, GM_CHUNK, p)
    past_len = page_table.shape[1] * PAGE_SIZE
    y_sample, s5_s, gmv_s, pool_s, k_s, v_s = _trunk(
        x_sample, past_len, state_s5, state_pool, cache_k, cache_v, page_table, x_sample.shape[1], p)
    return (y_prompt, y_sample, s5_p, s5_s, gmv_s, pool_p, pool_s, k_p, v_p, k_s, v_s)
```

```python
import functools
import math

import jax
import jax.numpy as jnp
from jax import lax
from jax.experimental import pallas as pl
from jax.experimental.pallas import tpu as pltpu

F32 = jnp.float32
BF16 = jnp.bfloat16

D_MODEL = 1024
MIX_HALF = D_MODEL // 2
S5_GROUP_CH = 16
S5_GROUPS = MIX_HALF // S5_GROUP_CH
S5_STATE = 64
GM_CHUNK = 128
GM_GROUPS = 4
GM_CH = MIX_HALF // GM_GROUPS
POOL_WINDOWS = (2, 4, 8, 16)
POOL_CH = MIX_HALF // len(POOL_WINDOWS)
POOL_BUF = max(POOL_WINDOWS) - 1
ATT_HEADS = 8
HEAD_DIM = MIX_HALF // ATT_HEADS
ROT_DIM = HEAD_DIM // 4
ROPE_THETA = 500000.0
MOBA_BLOCK = 256
MOBA_TOPK = 3
N_EXPERTS = 32
TOPK = 4
SWIGLU_LIMIT = 7.0
SWIGLU_ALPHA = 1.702
LN_EPS = 1e-5
PAGE_SIZE = 128

LANES = 128
SUBLANES = 8
VMEM_LIMIT = 56 * 1024 * 1024

S5_OCT = 4
S5_PAIRS = S5_GROUPS * S5_STATE // LANES
NEG_INF = float("-inf")


def _cparams(sem):
    return pltpu.CompilerParams(dimension_semantics=sem, vmem_limit_bytes=VMEM_LIMIT)


def _gelu(x):
    return 0.5 * x * (1.0 + jnp.tanh(math.sqrt(2.0 / math.pi) * (x + 0.044715 * (x * x * x))))


def _sigmoid(x):
    return 1.0 / (1.0 + jnp.exp(-x))


PROJ_TILE = 640


def _proj_kernel(x_ref, w_ref, o_ref):
    o_ref[...] = jnp.dot(x_ref[...].astype(BF16), w_ref[...], preferred_element_type=F32)


def _proj(x, w_bf16, tm):
    t, k = x.shape
    n = w_bf16.shape[1]
    return pl.pallas_call(
        _proj_kernel,
        grid=(t // tm,),
        in_specs=[pl.BlockSpec((tm, k), lambda i: (i, 0)),
                  pl.BlockSpec((k, n), lambda i: (0, 0))],
        out_specs=pl.BlockSpec((tm, n), lambda i: (i, 0)),
        out_shape=jax.ShapeDtypeStruct((t, n), F32),
        compiler_params=_cparams(("parallel",)),
    )(x, w_bf16)


def _s5_params(lam_re, lam_im, b_re, b_im, c_re, c_im, log_dt):
    dt = jnp.exp(log_dt.astype(F32))[:, None]
    lam = lax.complex(lam_re.astype(F32), lam_im.astype(F32))
    lam_bar = jnp.exp(lam * dt)
    b_bar = ((lam_bar - 1.0) / lam)[..., None] * lax.complex(b_re.astype(F32), b_im.astype(F32))
    eye = jnp.eye(SUBLANES, dtype=F32)
    bb = b_bar.reshape(S5_OCT, 8, S5_STATE, S5_GROUP_CH)

    def bdiag_b(t):
        return jnp.einsum('qgph,gk->qghkp', t, eye).reshape(S5_OCT, 128, 512)

    bw = jnp.concatenate([bdiag_b(bb.real), bdiag_b(bb.imag)], axis=-1).astype(BF16)
    cc_re = c_re.astype(F32).reshape(S5_OCT, 8, S5_GROUP_CH, S5_STATE)
    cc_im = c_im.astype(F32).reshape(S5_OCT, 8, S5_GROUP_CH, S5_STATE)

    def bdiag_c(t):
        return jnp.einsum('qghp,gk->qgpkh', t, eye).reshape(S5_OCT, 512, 128)

    cw = jnp.concatenate([bdiag_c(cc_re), -bdiag_c(cc_im)], axis=1).astype(BF16)
    rows = jnp.arange(SUBLANES)
    planes = []
    for d in (1, 2, 4):
        pw = jnp.exp(lam * dt * float(d)).reshape(S5_PAIRS, 1, LANES)
        m = (rows >= d).astype(F32)[None, :, None]
        planes += [pw.real * m, pw.imag * m]
    pw = jnp.exp((lam * dt).reshape(S5_PAIRS, 1, LANES) * (rows + 1).astype(F32)[None, :, None])
    planes += [pw.real, pw.imag]
    coef = jnp.stack(planes, axis=1).astype(F32)
    return bw, cw, coef


def _even_kernel(u_ref, gu_ref, gv_ref, x0_ref, bw_ref, coef_ref, cw_ref, d_ref, wglu_ref, bglu_ref,
                 ng_ref, m_ref, bias_ref, *rest, ts, chunk, per_block_init, with_vn, n_prev):
    rest = rest[n_prev:]
    if with_vn:
        a_ref, b_ref, vn_ref, st_out_ref, st_ref, carry_ref = rest
    else:
        a_ref, b_ref, st_out_ref, st_ref, carry_ref = rest
        vn_ref = None
    n_rb = ts // SUBLANES

    if not per_block_init:
        @pl.when(pl.program_id(1) == 0)
        def _():
            carry_ref[...] = x0_ref[0]

    u = u_ref[...]
    ub = u.astype(BF16)
    for q in range(S5_OCT):
        bu = jnp.dot(ub[:, q * 128:(q + 1) * 128], bw_ref[q], preferred_element_type=F32)
        for c in range(4):
            st_ref[q * 4 + c] = bu[:, c * 128:(c + 1) * 128]
            st_ref[S5_PAIRS + q * 4 + c] = bu[:, 512 + c * 128:512 + (c + 1) * 128]

    def pair_body(j, _):
        cf = coef_ref[j]
        a1r, a1i, a2r, a2i, a4r, a4i, pr, pi = [cf[k] for k in range(8)]

        def rb_body(r, carry):
            cr, ci = carry
            row = pl.multiple_of(r * SUBLANES, SUBLANES)
            xr = st_ref[j, pl.ds(row, SUBLANES), :]
            xi = st_ref[S5_PAIRS + j, pl.ds(row, SUBLANES), :]
            for d, ar, ai in ((1, a1r, a1i), (2, a2r, a2i), (4, a4r, a4i)):
                sr = pltpu.roll(xr, d, 0)
                si = pltpu.roll(xi, d, 0)
                xr, xi = xr + ar * sr - ai * si, xi + ar * si + ai * sr
            if per_block_init:
                cr = x0_ref[j, r]
                ci = x0_ref[S5_PAIRS + j, r]
            xr, xi = xr + pr * cr - pi * ci, xi + pr * ci + pi * cr
            st_ref[j, pl.ds(row, SUBLANES), :] = xr
            st_ref[S5_PAIRS + j, pl.ds(row, SUBLANES), :] = xi
            ncr = jnp.broadcast_to(xr[SUBLANES - 1:SUBLANES, :], (SUBLANES, LANES))
            nci = jnp.broadcast_to(xi[SUBLANES - 1:SUBLANES, :], (SUBLANES, LANES))
            if per_block_init:
                st_out_ref[j, r] = ncr
                st_out_ref[S5_PAIRS + j, r] = nci
            return ncr, nci

        cr, ci = lax.fori_loop(0, n_rb, rb_body, (carry_ref[j], carry_ref[S5_PAIRS + j]),
                               unroll=min(4, n_rb))
        carry_ref[j] = cr
        carry_ref[S5_PAIRS + j] = ci
        return 0

    lax.fori_loop(0, S5_PAIRS, pair_body, 0)

    if not per_block_init:
        @pl.when(pl.program_id(1) == pl.num_programs(1) - 1)
        def _():
            st_out_ref[0] = carry_ref[...]

    ys = []
    for q in range(S5_OCT):
        xq = jnp.concatenate([st_ref[q * 4 + c] for c in range(4)]
                             + [st_ref[S5_PAIRS + q * 4 + c] for c in range(4)], axis=-1)
        ys.append(jnp.dot(xq.astype(BF16), cw_ref[q], preferred_element_type=F32))
    y = jnp.concatenate(ys, axis=-1) + d_ref[...] * u
    g = _gelu(y)
    z = jnp.dot(g.astype(BF16), wglu_ref[...], preferred_element_type=F32) + bglu_ref[...]
    a_ref[...] = g * _sigmoid(z)

    gu = _gelu(gu_ref[...])
    gv = _gelu(gv_ref[...])
    for gi in range(GM_GROUPS):
        sl = slice(gi * GM_CH, (gi + 1) * GM_CH)
        v = gv[:, sl]
        mu = jnp.mean(v, axis=-1, keepdims=True)
        vc = v - mu
        var = jnp.mean(vc * vc, axis=-1, keepdims=True)
        vn = vc * lax.rsqrt(var + LN_EPS) * ng_ref[:, sl]
        if with_vn:
            vn_ref[:, sl] = vn
        vnb = vn.astype(BF16)
        for c in range(ts // chunk):
            rs = slice(c * chunk, (c + 1) * chunk)
            s = jnp.dot(m_ref[gi], vnb[rs], preferred_element_type=F32) + bias_ref[:, sl]
            b_ref[rs, sl] = gu[rs, sl] * s


def _even_mixer(proj, x0, s5p, d_skip, w_glu, b_glu, norm_g, m_mix, bias, *, n_b, n_s, per_block_init,
                row0=0, prev=()):
    bw, cw, coef = s5p
    if per_block_init:
        ts, grid, chunk = n_b * n_s, (1, 1), n_b * n_s
        assert n_s == SUBLANES
        n_rb = ts // SUBLANES
        x0_spec = pl.BlockSpec((2 * S5_PAIRS, n_rb, SUBLANES, LANES), lambda b, t: (0, 0, 0, 0))
        st_shape = (2 * S5_PAIRS, n_rb, SUBLANES, LANES)
        st_spec = x0_spec
    else:
        ts = min(512, n_s)
        grid, chunk = (n_b, n_s // ts), GM_CHUNK
        x0_spec = pl.BlockSpec((1, 2 * S5_PAIRS, SUBLANES, LANES), lambda b, t: (b, 0, 0, 0))
        st_shape = (n_b, 2 * S5_PAIRS, SUBLANES, LANES)
        st_spec = x0_spec
    nt = grid[1]
    with_vn = per_block_init
    blk0 = row0 // ts
    assert row0 % ts == 0

    def rows(col):
        return pl.BlockSpec((ts, MIX_HALF), lambda b, t, col=col: (blk0 + b * nt + t, col))

    def full(a):
        return pl.BlockSpec(a.shape, lambda b, t, nd=a.ndim: (0,) * nd)

    row_out = pl.BlockSpec((ts, MIX_HALF), lambda b, t: (blk0 + b * nt + t, 0))
    t_all = proj.shape[0]
    out_shape = [jax.ShapeDtypeStruct((t_all, MIX_HALF), F32), jax.ShapeDtypeStruct((t_all, MIX_HALF), F32)]
    out_specs = [row_out, row_out]
    if with_vn:
        out_shape.append(jax.ShapeDtypeStruct((n_b * n_s, MIX_HALF), F32))
        out_specs.append(pl.BlockSpec((ts, MIX_HALF), lambda b, t: (b * nt + t, 0)))
    out_shape.append(jax.ShapeDtypeStruct(st_shape, F32))
    out_specs.append(st_spec)
    weights = (bw, coef, cw, d_skip, w_glu, b_glu, norm_g, m_mix, bias)
    n_in = 4 + len(weights)
    outs = pl.pallas_call(
        functools.partial(_even_kernel, ts=ts, chunk=chunk, per_block_init=per_block_init, with_vn=with_vn,
                          n_prev=len(prev)),
        grid=grid,
        in_specs=[rows(0), rows(1), rows(2), x0_spec] + [full(w) for w in weights]
        + [pl.BlockSpec(memory_space=pl.ANY)] * len(prev),
        out_specs=out_specs,
        out_shape=out_shape,
        input_output_aliases={n_in + k: k for k in range(len(prev))},
        scratch_shapes=[pltpu.VMEM((2 * S5_PAIRS, ts, LANES), F32),
                        pltpu.VMEM((2 * S5_PAIRS, SUBLANES, LANES), F32)],
        compiler_params=_cparams(("arbitrary", "arbitrary")),
    )(proj, proj, proj, x0, *weights, *prev)
    if with_vn:
        return outs[0], outs[1], outs[2], outs[3]
    return outs[0], outs[1], None, outs[2]


def _state_to_lanes(x0, n_b):
    re = x0[..., 0].astype(F32).reshape(n_b, S5_PAIRS, LANES)
    im = x0[..., 1].astype(F32).reshape(n_b, S5_PAIRS, LANES)
    return jnp.concatenate([re, im], axis=1)


def _lanes_to_state(st, n_b):
    re = st[:, :S5_PAIRS].reshape(n_b, S5_GROUPS, S5_STATE)
    im = st[:, S5_PAIRS:].reshape(n_b, S5_GROUPS, S5_STATE)
    return jnp.stack([re, im], axis=-1)


def _even_layer_mix(proj, x0, p, i, *, n_b, n_s, sample, row0=0, prev=()):
    s5p = _s5_params(p['s5_lambda_re'][i], p['s5_lambda_im'][i], p['s5_b_re'][i], p['s5_b_im'][i],
                     p['s5_c_re'][i], p['s5_c_im'][i], p['s5_log_dt'][i])
    st0 = _state_to_lanes(x0, n_b)
    if sample:
        chunk = n_s
        x0k = jnp.broadcast_to(st0.transpose(1, 0, 2)[:, :, None, :], (2 * S5_PAIRS, n_b, SUBLANES, LANES))
        w = jnp.tril(p['gm_w_s'][i][:, :chunk, :chunk])
        m_mix = jnp.einsum('bc,gij->gbicj', jnp.eye(n_b, dtype=F32), w).reshape(GM_GROUPS, n_b * chunk, n_b * chunk)
        bias_rows = jnp.tile(p['gm_b_s'][i][:, :chunk].T, (n_b, 1))
    else:
        x0k = jnp.broadcast_to(st0[:, :, None, :], (n_b, 2 * S5_PAIRS, SUBLANES, LANES))
        m_mix = jnp.tril(p['gm_w_s'][i][:, :GM_CHUNK, :GM_CHUNK])
        bias_rows = p['gm_b_s'][i][:, :GM_CHUNK].T
    bias = jnp.repeat(bias_rows.astype(F32), GM_CH, axis=1)
    a, b, vn, st = _even_mixer(
        proj, x0k, s5p, p['s5_d'][i].reshape(1, MIX_HALF).astype(F32), p['s5_w_glu'][i].astype(BF16),
        p['s5_b_glu'][i].reshape(1, MIX_HALF).astype(F32), p['gm_norm_g'][i].reshape(1, MIX_HALF).astype(F32),
        m_mix.astype(BF16), bias, n_b=n_b, n_s=n_s, per_block_init=sample, row0=row0, prev=prev)
    if sample:
        st = st[:, :, 0, :].transpose(1, 0, 2)
    else:
        st = st[:, :, 0, :]
    return a, b, vn, _lanes_to_state(st, n_b)


POOL_HALO = 16


def _pool_kernel(c_ref, halo_ref, w_ref, scale_ref, *rest, ts, base):
    o_ref, hist_ref = rest[-2:]
    t = pl.program_id(1)

    @pl.when(t == 0)
    def _():
        hist_ref[0:POOL_HALO, :] = halo_ref[0]

    @pl.when(t > 0)
    def _():
        hist_ref[0:POOL_HALO, :] = hist_ref[ts:ts + POOL_HALO, :]

    hist_ref[POOL_HALO:POOL_HALO + ts, :] = c_ref[...]
    pos = base + t * ts + lax.broadcasted_iota(jnp.int32, (ts, 1), 0)
    for g, win in enumerate(POOL_WINDOWS):
        sl = slice(g * POOL_CH, (g + 1) * POOL_CH)
        x = hist_ref[POOL_HALO:POOL_HALO + ts, sl]
        acc = x
        for d in range(1, win):
            acc = acc + hist_ref[POOL_HALO - d:POOL_HALO - d + ts, sl]
        cnt = jnp.minimum(pos + 1, win).astype(F32)
        pooled = acc / cnt - x
        y = jnp.dot(pooled.astype(BF16), w_ref[g], preferred_element_type=F32)
        o_ref[:, sl] = y * scale_ref[:, sl]


def _pool_mixer(proj, halo, w, scale, *, n_b, n_s, base, row0=0, prev=()):
    ts = min(512, n_s)
    nt = n_s // ts
    blk0 = row0 // ts
    assert row0 % ts == 0
    return pl.pallas_call(
        functools.partial(_pool_kernel, ts=ts, base=base),
        grid=(n_b, nt),
        in_specs=[pl.BlockSpec((ts, MIX_HALF), lambda b, t: (blk0 + b * nt + t, 0)),
                  pl.BlockSpec((1, POOL_HALO, MIX_HALF), lambda b, t: (b, 0, 0)),
                  pl.BlockSpec(w.shape, lambda b, t: (0, 0, 0)),
                  pl.BlockSpec((1, MIX_HALF), lambda b, t: (0, 0))]
        + [pl.BlockSpec(memory_space=pl.ANY)] * len(prev),
        out_specs=pl.BlockSpec((ts, MIX_HALF), lambda b, t: (blk0 + b * nt + t, 0)),
        out_shape=jax.ShapeDtypeStruct((proj.shape[0], MIX_HALF), F32),
        input_output_aliases={4 + k: k for k in range(len(prev))},
        scratch_shapes=[pltpu.VMEM((POOL_HALO + ts, MIX_HALF), F32)],
        compiler_params=_cparams(("arbitrary", "arbitrary")),
    )(proj, halo, w, scale, *prev)


def _rope_tables(pos):
    half = ROT_DIM // 2
    inv = ROPE_THETA ** (-jnp.arange(half, dtype=F32) * 2.0 / ROT_DIM)
    ang = pos.astype(F32)[:, None] * inv[None, :]
    cos, sin = jnp.cos(ang), jnp.sin(ang)
    n = pos.shape[0]
    one = jnp.ones((n, HEAD_DIM - ROT_DIM), F32)
    zero = jnp.zeros((n, HEAD_DIM - ROT_DIM), F32)
    z8 = jnp.zeros((n, half), F32)
    ca = jnp.concatenate([cos, cos, one], axis=1)
    sp = jnp.concatenate([z8, sin, zero], axis=1)
    sm = jnp.concatenate([-sin, z8, zero], axis=1)
    return tuple(jnp.tile(t, (1, LANES // HEAD_DIM)) for t in (ca, sp, sm))


def _rope_kernel(q_ref, k_ref, v_ref, ca_ref, sp_ref, sm_ref, qo_ref, ko_ref, qb_ref, kb_ref, vb_ref, km_ref):
    ca, sp, sm = ca_ref[...], sp_ref[...], sm_ref[...]
    half = ROT_DIM // 2
    for c in range(MIX_HALF // LANES):
        sl = slice(c * LANES, (c + 1) * LANES)
        for src, dst in ((q_ref, qo_ref), (k_ref, ko_ref)):
            x = src[:, sl]
            dst[:, sl] = x * ca + pltpu.roll(x, half, 1) * sp + pltpu.roll(x, LANES - half, 1) * sm
    q = qo_ref[...]
    k = ko_ref[...]
    qb_ref[...] = (q * (HEAD_DIM ** -0.5)).astype(BF16)
    kb_ref[...] = k.astype(BF16)
    vb_ref[...] = v_ref[...].astype(BF16)
    km_ref[0] = jnp.mean(k, axis=0, keepdims=True)


def _rope(proj, tables):
    t_rows = proj.shape[0]
    ts = MOBA_BLOCK

    def col(c):
        return pl.BlockSpec((ts, MIX_HALF), lambda i, c=c: (i, c))

    tab = pl.BlockSpec((ts, LANES), lambda i: (i, 0))
    row = pl.BlockSpec((ts, MIX_HALF), lambda i: (i, 0))
    f32o = jax.ShapeDtypeStruct((t_rows, MIX_HALF), F32)
    bfo = jax.ShapeDtypeStruct((t_rows, MIX_HALF), BF16)
    outs = pl.pallas_call(
        _rope_kernel,
        grid=(t_rows // ts,),
        in_specs=[col(1), col(2), col(3), tab, tab, tab],
        out_specs=[row, row, row, row, row, pl.BlockSpec((1, 1, MIX_HALF), lambda i: (i, 0, 0))],
        out_shape=[f32o, f32o, bfo, bfo, bfo, jax.ShapeDtypeStruct((t_rows // ts, 1, MIX_HALF), F32)],
        compiler_params=_cparams(("parallel",)),
    )(proj, proj, proj, *tables)
    return list(outs[:5]) + [outs[5].reshape(t_rows // ts, MIX_HALF)]


def _top_rows_mask(gate, n_valid_rows, k_top):
    n = gate.shape[0]
    row = lax.broadcasted_iota(jnp.int32, gate.shape, 0)
    live = row < n_valid_rows
    sel = jnp.zeros(gate.shape, jnp.bool_)
    for _ in range(k_top):
        g = jnp.where(live, gate, NEG_INF)
        mx = jnp.max(g, axis=0, keepdims=True)
        first = jnp.min(jnp.where(live & (g == mx), row, n), axis=0, keepdims=True)
        pick = row == first
        sel = sel | pick
        live = live & jnp.logical_not(pick)
    return sel


def _moba_prompt_kernel(q_ref, qb_ref, kb_ref, vb_ref, km_ref, _, o_ref, *, n_blk):
    qi = pl.program_id(1)
    tq = MOBA_BLOCK
    krow = lax.broadcasted_iota(jnp.int32, (tq, tq), 0)
    qcol = lax.broadcasted_iota(jnp.int32, (tq, tq), 1)
    causal = krow <= qcol
    scale = HEAD_DIM ** -0.5
    for h in range(ATT_HEADS):
        hs = slice(h * HEAD_DIM, (h + 1) * HEAD_DIM)
        qh = q_ref[:, hs] * scale
        qbh = qb_ref[:, hs]
        gate = lax.dot_general(km_ref[:, hs], qh, (((1,), (1,)), ((), ())),
                               precision=lax.Precision.HIGHEST, preferred_element_type=F32)
        sel = _top_rows_mask(gate, qi, MOBA_TOPK)
        self32 = sel.astype(F32)
        row0 = pl.multiple_of(qi * tq, tq)
        s = lax.dot_general(kb_ref[pl.ds(row0, tq), hs], qbh, (((1,), (1,)), ((), ())),
                            preferred_element_type=F32)
        s = jnp.where(causal, s, NEG_INF)
        m = jnp.max(s, axis=0, keepdims=True)
        p = jnp.exp(s - m)
        l = jnp.sum(p, axis=0, keepdims=True)
        acc = lax.dot_general(vb_ref[pl.ds(row0, tq), hs], p.astype(BF16), (((0,), (0,)), ((), ())),
                              preferred_element_type=F32)

        def blk_body(j, carry):
            m, l, acc = carry
            r0 = pl.multiple_of(j * tq, tq)
            s = lax.dot_general(kb_ref[pl.ds(r0, tq), hs], qbh, (((1,), (1,)), ((), ())),
                                preferred_element_type=F32)
            rowsel = jnp.sum(jnp.where(lax.broadcasted_iota(jnp.int32, (n_blk, tq), 0) == j, self32, 0.0),
                             axis=0, keepdims=True)
            s = jnp.where(rowsel > 0.5, s, NEG_INF)
            m_new = jnp.maximum(m, jnp.max(s, axis=0, keepdims=True))
            alpha = jnp.exp(m - m_new)
            p = jnp.exp(s - m_new)
            l = alpha * l + jnp.sum(p, axis=0, keepdims=True)
            acc = alpha * acc + lax.dot_general(vb_ref[pl.ds(r0, tq), hs], p.astype(BF16),
                                                (((0,), (0,)), ((), ())), preferred_element_type=F32)
            return m_new, l, acc

        m, l, acc = lax.fori_loop(0, qi, blk_body, (m, l, acc))
        o_ref[:, hs] = (acc / l).T


def _moba_prompt(q_rot, qb, kb, vb, kmean, d_prev, *, n_b, n_s):
    n_blk = n_s // MOBA_BLOCK
    tq = MOBA_BLOCK
    qspec = pl.BlockSpec((tq, MIX_HALF), lambda b, i: (b * n_blk + i, 0))
    kvspec = pl.BlockSpec((n_s, MIX_HALF), lambda b, i: (b, 0))
    return pl.pallas_call(
        functools.partial(_moba_prompt_kernel, n_blk=n_blk),
        grid=(n_b, n_blk),
        in_specs=[qspec, qspec, kvspec, kvspec, pl.BlockSpec((n_blk, MIX_HALF), lambda b, i: (b, 0)),
                  pl.BlockSpec(memory_space=pl.ANY)],
        out_specs=qspec,
        out_shape=jax.ShapeDtypeStruct(d_prev.shape, F32),
        input_output_aliases={5: 0},
        compiler_params=_cparams(("arbitrary", "arbitrary")),
    )(q_rot, qb, kb, vb, kmean, d_prev)


PAGES_PER_STEP = 8
BLOCK_PAGES = MOBA_BLOCK // PAGE_SIZE


def _moba_sample_kernel(pt_ref, *refs, n_blk, n_q):
    del pt_ref
    kp = refs[:PAGES_PER_STEP]
    vp = refs[PAGES_PER_STEP:2 * PAGES_PER_STEP]
    qbd_ref, qbdf_ref, kn_ref, vn_ref, _, o_ref, oacc_ref, m_ref, l_ref, km_ref = refs[2 * PAGES_PER_STEP:]
    c = pl.program_id(1)
    ncol = ATT_HEADS * n_q
    blocks_per_step = PAGES_PER_STEP // BLOCK_PAGES
    qbd = qbd_ref[0]
    for blk in range(blocks_per_step):
        n = c * blocks_per_step + blk
        kf = jnp.concatenate([kp[blk * BLOCK_PAGES + j][0] for j in range(BLOCK_PAGES)], axis=0)
        vf = jnp.concatenate([vp[blk * BLOCK_PAGES + j][0] for j in range(BLOCK_PAGES)], axis=0)
        km_ref[pl.ds(n, 1), :] = jnp.mean(kf, axis=0, keepdims=True)
        s = jnp.dot(kf.astype(BF16), qbd, preferred_element_type=F32)
        m = jnp.max(s, axis=0, keepdims=True)
        p = jnp.exp(s - m)
        m_ref[pl.ds(n, 1), :] = m
        l_ref[pl.ds(n, 1), :] = jnp.sum(p, axis=0, keepdims=True)
        oacc_ref[n] = lax.dot_general(p.astype(BF16), vf.astype(BF16), (((0,), (0,)), ((), ())),
                                      preferred_element_type=F32)

    @pl.when(c == pl.num_programs(1) - 1)
    def _():
        gate = jnp.dot(km_ref[...], qbdf_ref[0], precision=lax.Precision.HIGHEST,
                       preferred_element_type=F32)
        sel = _top_rows_mask(gate, n_blk, MOBA_TOPK)
        s_own = jnp.dot(kn_ref[...].astype(BF16), qbd, preferred_element_type=F32)
        krow = lax.broadcasted_iota(jnp.int32, (n_q, ncol), 0)
        qidx = lax.broadcasted_iota(jnp.int32, (n_q, ncol), 1) % n_q
        s_own = jnp.where(krow <= qidx, s_own, NEG_INF)
        m_all = m_ref[...]
        m_fin = jnp.maximum(jnp.max(jnp.where(sel, m_all, NEG_INF), axis=0, keepdims=True),
                            jnp.max(s_own, axis=0, keepdims=True))
        w = jnp.where(sel, jnp.exp(m_all - m_fin), 0.0)
        p_own = jnp.exp(s_own - m_fin)
        l_fin = jnp.sum(w * l_ref[...], axis=0, keepdims=True) + jnp.sum(p_own, axis=0, keepdims=True)
        w = w / l_fin
        p_own = p_own / l_fin
        wpad = jnp.concatenate([w, jnp.zeros((LANES - n_blk, ncol), F32)], axis=0)
        wpad = jnp.concatenate([wpad, jnp.zeros((LANES, LANES - ncol), F32)], axis=1)
        wt = wpad.T
        acc = lax.dot_general(p_own.astype(BF16), vn_ref[...].astype(BF16), (((0,), (0,)), ((), ())),
                              preferred_element_type=F32)
        for n in range(n_blk):
            acc = acc + wt[:ncol, n:n + 1] * oacc_ref[n]
        head = lax.broadcasted_iota(jnp.int32, (n_q, MIX_HALF), 1) // HEAD_DIM
        out = jnp.zeros((n_q, MIX_HALF), F32)
        for h in range(ATT_HEADS):
            out = out + jnp.where(head == h, acc[h * n_q:(h + 1) * n_q], 0.0)
        o_ref[...] = out


def _moba_sample(page_table, cache_k, cache_v, q_rot, k_rot, proj, d_prev, *, n_b, n_q, row0):
    n_pages = page_table.shape[1]
    n_blk = n_pages // BLOCK_PAGES
    ncol = ATT_HEADS * n_q
    assert n_blk + 1 <= LANES and ncol <= LANES and n_pages % PAGES_PER_STEP == 0 and row0 % n_q == 0
    blk0 = row0 // n_q
    q4 = (q_rot[row0:row0 + n_b * n_q] * (HEAD_DIM ** -0.5)).reshape(n_b, n_q, ATT_HEADS, HEAD_DIM)
    qbdf = jnp.einsum('bihd,hg->bhdgi', q4, jnp.eye(ATT_HEADS, dtype=F32)).reshape(n_b, MIX_HALF, ncol)

    def page_spec(j):
        return pl.BlockSpec((1, PAGE_SIZE, MIX_HALF), lambda b, c, pt, j=j: (pt[b, c * PAGES_PER_STEP + j], 0, 0))

    per_b3 = lambda shape: pl.BlockSpec(shape, lambda b, c, pt: (b, 0, 0))
    grid_spec = pltpu.PrefetchScalarGridSpec(
        num_scalar_prefetch=1,
        grid=(n_b, n_pages // PAGES_PER_STEP),
        in_specs=[page_spec(j) for j in range(PAGES_PER_STEP)] * 2
        + [per_b3((1, MIX_HALF, ncol)), per_b3((1, MIX_HALF, ncol)),
           pl.BlockSpec((n_q, MIX_HALF), lambda b, c, pt: (blk0 + b, 0)),
           pl.BlockSpec((n_q, MIX_HALF), lambda b, c, pt: (blk0 + b, 3)),
           pl.BlockSpec(memory_space=pl.ANY)],
        out_specs=pl.BlockSpec((n_q, MIX_HALF), lambda b, c, pt: (blk0 + b, 0)),
        scratch_shapes=[pltpu.VMEM((n_blk, ncol, MIX_HALF), F32), pltpu.VMEM((n_blk, ncol), F32),
                        pltpu.VMEM((n_blk, ncol), F32), pltpu.VMEM((n_blk, MIX_HALF), F32)])
    return pl.pallas_call(
        functools.partial(_moba_sample_kernel, n_blk=n_blk, n_q=n_q),
        grid_spec=grid_spec,
        out_shape=jax.ShapeDtypeStruct(d_prev.shape, F32),
        input_output_aliases={1 + 2 * PAGES_PER_STEP + 4: 0},
        compiler_params=_cparams(("arbitrary", "arbitrary")),
    )(page_table, *([cache_k] * PAGES_PER_STEP), *([cache_v] * PAGES_PER_STEP), qbdf.astype(BF16), qbdf,
      k_rot, proj, d_prev)


TOK_TILE = 256
SEG_ALIGN = SUBLANES
ASG_TILE = -(-(TOK_TILE * TOPK + N_EXPERTS * (SEG_ALIGN - 1)) // LANES) * LANES
EXP_TILE = 256


def _layer_norm_rows(h, g, b):
    mu = jnp.mean(h, axis=-1, keepdims=True)
    hc = h - mu
    var = jnp.mean(hc * hc, axis=-1, keepdims=True)
    return hc * lax.rsqrt(var + LN_EPS) * g + b


def _route_kernel(x_ref, a_ref, b_ref, wo_ref, g_ref, bt_ref, wr_ref, br_ref,
                  x1_ref, xs_ref, dg_ref, cnt_ref, *, alpha):
    h = (alpha * x_ref[...]
         + jnp.dot(a_ref[...].astype(BF16), wo_ref[0:MIX_HALF, :], preferred_element_type=F32)
         + jnp.dot(b_ref[...].astype(BF16), wo_ref[MIX_HALF:, :], preferred_element_type=F32))
    x1 = _layer_norm_rows(h, g_ref[...], bt_ref[...])
    x1_ref[...] = x1
    logits = lax.dot_general(wr_ref[...], x1, (((1,), (1,)), ((), ())), precision=lax.Precision.HIGHEST,
                             preferred_element_type=F32) + br_ref[...]
    row = lax.broadcasted_iota(jnp.int32, logits.shape, 0)
    g = logits
    picks, vals = [], []
    for _ in range(TOPK):
        mx = jnp.max(g, axis=0, keepdims=True)
        first = jnp.min(jnp.where(g == mx, row, N_EXPERTS), axis=0, keepdims=True)
        pick = row == first
        picks.append(pick)
        vals.append(mx)
        g = jnp.where(pick, NEG_INF, g)
    es = [jnp.exp(v - vals[0]) for v in vals]
    den = es[0] + es[1] + es[2] + es[3]
    onehot = [p.astype(F32) for p in picks]
    member = onehot[0] + onehot[1] + onehot[2] + onehot[3]
    t_r = lax.broadcasted_iota(jnp.int32, (TOK_TILE, TOK_TILE), 0)
    t_c = lax.broadcasted_iota(jnp.int32, (TOK_TILE, TOK_TILE), 1)
    before = (t_r < t_c).astype(BF16)
    rank = jnp.dot(member.astype(BF16), before, preferred_element_type=F32)
    cnt = jnp.sum(member, axis=1, keepdims=True)
    cnt = jnp.ceil(cnt * (1.0 / SEG_ALIGN)) * SEG_ALIGN
    e_r = lax.broadcasted_iota(jnp.int32, (N_EXPERTS, N_EXPERTS), 0)
    e_c = lax.broadcasted_iota(jnp.int32, (N_EXPERTS, N_EXPERTS), 1)
    lower = (e_c < e_r).astype(F32)
    off = jnp.dot(lower, jnp.broadcast_to(cnt, (N_EXPERTS, TOK_TILE)), precision=lax.Precision.HIGHEST,
                  preferred_element_type=F32)
    slot = off + rank
    dests = [jnp.sum(oh * slot, axis=0, keepdims=True) for oh in onehot]
    r_iota = lax.broadcasted_iota(jnp.int32, (ASG_TILE, TOK_TILE), 0)
    perm = jnp.zeros((ASG_TILE, TOK_TILE), F32)
    for d in dests:
        perm = perm + (r_iota == d.astype(jnp.int32)).astype(F32)
    xs_ref[...] = jnp.dot(perm.astype(BF16), x1.astype(BF16), preferred_element_type=F32)
    dg_ref[0] = jnp.concatenate(dests + [e / den for e in es], axis=0)
    cnt_ref[0] = jnp.broadcast_to(cnt, (N_EXPERTS, LANES))


def _route(x, mix_a, mix_b, w_out_bf, ln_g, ln_b, wr_t, br, *, alpha):
    t = x.shape[0]
    nt = t // TOK_TILE
    full2 = lambda a: pl.BlockSpec(a.shape, lambda i: (0, 0))
    return pl.pallas_call(
        functools.partial(_route_kernel, alpha=alpha),
        grid=(nt,),
        in_specs=[pl.BlockSpec((TOK_TILE, D_MODEL), lambda i: (i, 0)),
                  pl.BlockSpec((TOK_TILE, MIX_HALF), lambda i: (i, 0)),
                  pl.BlockSpec((TOK_TILE, MIX_HALF), lambda i: (i, 0)),
                  full2(w_out_bf), full2(ln_g), full2(ln_b), full2(wr_t), full2(br)],
        out_specs=[pl.BlockSpec((TOK_TILE, D_MODEL), lambda i: (i, 0)),
                   pl.BlockSpec((ASG_TILE, D_MODEL), lambda i: (i, 0)),
                   pl.BlockSpec((1, 2 * TOPK, TOK_TILE), lambda i: (i, 0, 0)),
                   pl.BlockSpec((1, N_EXPERTS, LANES), lambda i: (i, 0, 0))],
        out_shape=[jax.ShapeDtypeStruct((t, D_MODEL), F32),
                   jax.ShapeDtypeStruct((nt * ASG_TILE, D_MODEL), F32),
                   jax.ShapeDtypeStruct((nt, 2 * TOPK, TOK_TILE), F32),
                   jax.ShapeDtypeStruct((nt, N_EXPERTS, LANES), F32)],
        compiler_params=_cparams(("parallel",)),
    )(x, mix_a, mix_b, w_out_bf, ln_g, ln_b, wr_t, br)


SEG_PIECES = tuple(1 << b for b in range(TOK_TILE.bit_length() - 1, SEG_ALIGN.bit_length() - 2, -1))
SEG_LAG = 8


def _segcopy_kernel(src_ref, dst_ref, len_ref, x_ref, init_ref, o_ref, sem, *, n_seg):
    del init_ref

    def pieces(s, fn):
        n = len_ref[s]
        for size in SEG_PIECES:
            @pl.when((n & size) != 0)
            def _():
                done = n & ~(2 * size - 1)
                src = pl.multiple_of(src_ref[s] + done, SEG_ALIGN)
                dst = pl.multiple_of(dst_ref[s] + done, SEG_ALIGN)
                fn(pltpu.make_async_copy(x_ref.at[pl.ds(src, size)], o_ref.at[pl.ds(dst, size)], sem))

    def body(s, _):
        @pl.when(s < n_seg)
        def _():
            pieces(s, lambda cp: cp.start())

        @pl.when(s >= SEG_LAG)
        def _():
            pieces(s - SEG_LAG, lambda cp: cp.wait())
        return 0

    lax.fori_loop(0, n_seg + SEG_LAG, body, 0)


def _segcopy(src_rows, dst_rows, lens, x, init):
    n_seg = src_rows.shape[0]
    grid_spec = pltpu.PrefetchScalarGridSpec(
        num_scalar_prefetch=3, grid=(1,),
        in_specs=[pl.BlockSpec(memory_space=pl.ANY), pl.BlockSpec(memory_space=pl.ANY)],
        out_specs=pl.BlockSpec(memory_space=pl.ANY),
        scratch_shapes=[pltpu.SemaphoreType.DMA(())])
    return pl.pallas_call(
        functools.partial(_segcopy_kernel, n_seg=n_seg),
        grid_spec=grid_spec,
        out_shape=jax.ShapeDtypeStruct(init.shape, init.dtype),
        input_output_aliases={4: 0},
        compiler_params=_cparams(("arbitrary",)),
    )(src_rows, dst_rows, lens, x, init)


def _expert_kernel(te_ref, nu_ref, x_ref, wg_ref, bg_ref, wu_ref, bu_ref, wd_ref, bd_ref, o_ref,
                   wgb_ref, wub_ref, wdb_ref):
    m = pl.program_id(0)
    used = m < nu_ref[0]
    prev = te_ref[jnp.maximum(m - 1, 0)]

    @pl.when(used & ((m == 0) | (te_ref[m] != prev)))
    def _():
        wgb_ref[...] = wg_ref[0].astype(BF16)
        wub_ref[...] = wu_ref[0].astype(BF16)
        wdb_ref[...] = wd_ref[0].astype(BF16)

    @pl.when(used)
    def _():
        xb = x_ref[...].astype(BF16)
        gt = jnp.minimum(jnp.dot(xb, wgb_ref[...], preferred_element_type=F32) + bg_ref[0], SWIGLU_LIMIT)
        up = jnp.clip(jnp.dot(xb, wub_ref[...], preferred_element_type=F32) + bu_ref[0], -SWIGLU_LIMIT, SWIGLU_LIMIT)
        act = gt * _sigmoid(SWIGLU_ALPHA * gt) * (up + 1.0)
        o_ref[...] = jnp.dot(act.astype(BF16), wdb_ref[...], preferred_element_type=F32) + bd_ref[0]

    @pl.when(jnp.logical_not(used))
    def _():
        o_ref[...] = jnp.zeros(o_ref.shape, F32)


def _experts(tile_expert, n_used, xg, w_gate, b_gate, w_up, b_up, w_down, b_down):
    n_rows = xg.shape[0]
    n_tiles = n_rows // EXP_TILE

    def xmap(m, te, nu):
        return (jnp.minimum(m, jnp.maximum(nu[0] - 1, 0)), 0)

    wspec = pl.BlockSpec((1, D_MODEL, D_MODEL), lambda m, te, nu: (te[m], 0, 0))
    bspec = pl.BlockSpec((1, 1, D_MODEL), lambda m, te, nu: (te[m], 0, 0))
    grid_spec = pltpu.PrefetchScalarGridSpec(
        num_scalar_prefetch=2, grid=(n_tiles,),
        in_specs=[pl.BlockSpec((EXP_TILE, D_MODEL), xmap), wspec, bspec, wspec, bspec, wspec, bspec],
        out_specs=pl.BlockSpec((EXP_TILE, D_MODEL), lambda m, te, nu: (m, 0)),
        scratch_shapes=[pltpu.VMEM((D_MODEL, D_MODEL), BF16)] * 3)
    return pl.pallas_call(
        _expert_kernel,
        grid_spec=grid_spec,
        out_shape=jax.ShapeDtypeStruct((n_rows, D_MODEL), F32),
        compiler_params=_cparams(("arbitrary",)),
    )(tile_expert, n_used, xg, w_gate, b_gate.reshape(N_EXPERTS, 1, D_MODEL), w_up,
      b_up.reshape(N_EXPERTS, 1, D_MODEL), w_down, b_down.reshape(N_EXPERTS, 1, D_MODEL))


def _combine_kernel(x1_ref, ys_ref, dg_ref, g_ref, bt_ref, o_ref, *, alpha):
    dg = dg_ref[0]
    r_iota = lax.broadcasted_iota(jnp.int32, (ASG_TILE, TOK_TILE), 0)
    comb = jnp.zeros((ASG_TILE, TOK_TILE), F32)
    for k in range(TOPK):
        comb = comb + jnp.where(r_iota == dg[k:k + 1, :].astype(jnp.int32), dg[TOPK + k:TOPK + k + 1, :], 0.0)
    ffn = lax.dot_general(comb.astype(BF16), ys_ref[...].astype(BF16), (((0,), (0,)), ((), ())),
                          preferred_element_type=F32)
    o_ref[...] = _layer_norm_rows(alpha * x1_ref[...] + ffn, g_ref[...], bt_ref[...])


def _combine(x1, ys, dg, ln_g, ln_b, *, alpha):
    t = x1.shape[0]
    nt = t // TOK_TILE
    return pl.pallas_call(
        functools.partial(_combine_kernel, alpha=alpha),
        grid=(nt,),
        in_specs=[pl.BlockSpec((TOK_TILE, D_MODEL), lambda i: (i, 0)),
                  pl.BlockSpec((ASG_TILE, D_MODEL), lambda i: (i, 0)),
                  pl.BlockSpec((1, 2 * TOPK, TOK_TILE), lambda i: (i, 0, 0)),
                  pl.BlockSpec((1, D_MODEL), lambda i: (0, 0)),
                  pl.BlockSpec((1, D_MODEL), lambda i: (0, 0))],
        out_specs=pl.BlockSpec((TOK_TILE, D_MODEL), lambda i: (i, 0)),
        out_shape=jax.ShapeDtypeStruct((t, D_MODEL), F32),
        compiler_params=_cparams(("parallel",)),
    )(x1, ys, dg, ln_g, ln_b)


def _channel_mix(x, mix_a, mix_b, p, layer, alpha):
    t = x.shape[0]
    nt = t // TOK_TILE
    x1, xs, dg, cnt = _route(
        x, mix_a, mix_b, p['w_out'][layer].astype(BF16), p['ln_g'][layer, 0].reshape(1, D_MODEL),
        p['ln_b'][layer, 0].reshape(1, D_MODEL), p['router_w'][layer].T.astype(F32),
        p['router_b'][layer].reshape(N_EXPERTS, 1).astype(F32), alpha=alpha)
    cnt = cnt[:, :, 0].astype(jnp.int32)
    local_off = jnp.cumsum(cnt, axis=1) - cnt
    tile_rows = (jnp.arange(nt, dtype=jnp.int32) * ASG_TILE)[:, None] + local_off
    total = jnp.sum(cnt, axis=0)
    padded = (total + EXP_TILE - 1) // EXP_TILE * EXP_TILE
    pend = jnp.cumsum(padded)
    expert_rows = (pend - padded)[None, :] + jnp.cumsum(cnt, axis=0) - cnt
    n_tiles = -(-(nt * (TOK_TILE * TOPK + N_EXPERTS * (SEG_ALIGN - 1))) // EXP_TILE) + N_EXPERTS
    tile_expert = jnp.minimum(jnp.searchsorted(pend, jnp.arange(n_tiles, dtype=jnp.int32) * EXP_TILE, side='right'),
                              N_EXPERTS - 1).astype(jnp.int32)
    n_used = (pend[-1:] // EXP_TILE).astype(jnp.int32)
    tile_rows, expert_rows, lens = tile_rows.reshape(-1), expert_rows.reshape(-1).astype(jnp.int32), cnt.reshape(-1)
    xg = _segcopy(tile_rows, expert_rows, lens, xs, jnp.zeros((n_tiles * EXP_TILE, D_MODEL), F32))
    yg = _experts(tile_expert, n_used, xg, p['moe_w_gate'][layer], p['moe_b_gate'][layer], p['moe_w_up'][layer],
                  p['moe_b_up'][layer], p['moe_w_down'][layer], p['moe_b_down'][layer])
    ys = _segcopy(expert_rows, tile_rows, lens, yg, jnp.zeros((nt * ASG_TILE, D_MODEL), F32))
    return _combine(x1, ys, dg, p['ln_g'][layer, 1].reshape(1, D_MODEL), p['ln_b'][layer, 1].reshape(1, D_MODEL),
                    alpha=alpha)


def kernel(x_prompt, x_sample, state_s5, state_pool, cache_k, cache_v, page_table, w_in_ab, s5_lambda_re, s5_lambda_im, s5_b_re, s5_b_im, s5_c_re, s5_c_im, s5_d, s5_log_dt, s5_w_glu, s5_b_glu, gm_norm_g, gm_w_s, gm_b_s, w_in_cd, pool_w, pool_scale, w_out, ln_g, ln_b, router_w, router_b, moe_w_gate, moe_b_gate, moe_w_up, moe_b_up, moe_w_down, moe_b_down):
    p = dict(w_in_ab=w_in_ab, s5_lambda_re=s5_lambda_re, s5_lambda_im=s5_lambda_im,
             s5_b_re=s5_b_re, s5_b_im=s5_b_im, s5_c_re=s5_c_re, s5_c_im=s5_c_im, s5_d=s5_d,
             s5_log_dt=s5_log_dt, s5_w_glu=s5_w_glu, s5_b_glu=s5_b_glu, gm_norm_g=gm_norm_g,
             gm_w_s=gm_w_s, gm_b_s=gm_b_s, w_in_cd=w_in_cd, pool_w=pool_w, pool_scale=pool_scale,
             w_out=w_out, ln_g=ln_g, ln_b=ln_b, router_w=router_w, router_b=router_b,
             moe_w_gate=moe_w_gate, moe_b_gate=moe_b_gate, moe_w_up=moe_w_up, moe_b_up=moe_b_up,
             moe_w_down=moe_w_down, moe_b_down=moe_b_down)
    n_bp, n_sp, _ = x_prompt.shape
    n_bs, n_ss, _ = x_sample.shape
    t_p, t_s = n_bp * n_sp, n_bs * n_ss
    depth = w_out.shape[0]
    alpha = (2 * depth) ** 0.25
    past_len = page_table.shape[1] * PAGE_SIZE
    x = jnp.concatenate([x_prompt.reshape(t_p, D_MODEL), x_sample.reshape(t_s, D_MODEL)], axis=0)
    zero_s5 = jnp.zeros((n_bp, S5_GROUPS, S5_STATE, 2), F32)
    pos = jnp.concatenate([jnp.tile(jnp.arange(n_sp), n_bp), jnp.tile(past_len + jnp.arange(n_ss), n_bs)])
    rope_tables = _rope_tables(pos)
    zeros_half = jnp.zeros((t_p + t_s, MIX_HALF), F32)
    s5_p, s5_s, gmv_s, pool_p, pool_s, k_p, v_p, k_s, v_s = [], [], [], [], [], [], [], [], []
    for layer in range(depth):
        i = layer // 2
        if layer % 2 == 0:
            proj = _proj(x, w_in_ab[i].astype(BF16), PROJ_TILE)
            a, b, _, st_p = _even_layer_mix(proj, zero_s5, p, i, n_b=n_bp, n_s=n_sp, sample=False,
                                            prev=(zeros_half, zeros_half))
            a, b, vn, st_s = _even_layer_mix(proj, state_s5[i], p, i, n_b=n_bs, n_s=n_ss, sample=True,
                                             row0=t_p, prev=(a, b))
            s5_p.append(st_p)
            s5_s.append(st_s)
            gmv_s.append(vn.reshape(n_bs, n_ss, MIX_HALF))
        else:
            proj = _proj(x, w_in_cd[i].astype(BF16), PROJ_TILE)
            pw = pool_w[i].astype(BF16)
            ps = pool_scale[i].reshape(1, MIX_HALF).astype(F32)
            a = _pool_mixer(proj, jnp.zeros((n_bp, POOL_HALO, MIX_HALF), F32), pw, ps, n_b=n_bp, n_s=n_sp, base=0,
                            prev=(zeros_half,))
            halo = jnp.concatenate([jnp.zeros((n_bs, POOL_HALO - POOL_BUF, MIX_HALF), F32),
                                    state_pool[i].astype(F32)], axis=1)
            a = _pool_mixer(proj, halo, pw, ps, n_b=n_bs, n_s=n_ss, base=POOL_BUF, row0=t_p, prev=(a,))
            q_rot, k_rot, qb, kb, vb, kmean = _rope(proj, rope_tables)
            b = _moba_prompt(q_rot, qb, kb, vb, kmean, zeros_half, n_b=n_bp, n_s=n_sp)
            n_pool = cache_k.shape[1]
            b = _moba_sample(page_table, cache_k[i].reshape(n_pool, PAGE_SIZE, MIX_HALF),
                             cache_v[i].reshape(n_pool, PAGE_SIZE, MIX_HALF), q_rot, k_rot, proj, b,
                             n_b=n_bs, n_q=n_ss, row0=t_p)
            c_p = proj[:t_p, :MIX_HALF].reshape(n_bp, n_sp, MIX_HALF)
            c_s = proj[t_p:, :MIX_HALF].reshape(n_bs, n_ss, MIX_HALF)
            pool_p.append(c_p[:, -POOL_BUF:])
            pool_s.append(jnp.concatenate([state_pool[i].astype(F32), c_s], axis=1)[:, -POOL_BUF:])
            k_p.append(k_rot[:t_p].reshape(n_bp, n_sp, ATT_HEADS, HEAD_DIM))
            k_s.append(k_rot[t_p:].reshape(n_bs, n_ss, ATT_HEADS, HEAD_DIM))
            v_p.append(proj[:t_p, 3 * MIX_HALF:].reshape(n_bp, n_sp, ATT_HEADS, HEAD_DIM))
            v_s.append(proj[t_p:, 3 * MIX_HALF:].reshape(n_bs, n_ss, ATT_HEADS, HEAD_DIM))
        x = _channel_mix(x, a, b, p, layer, alpha)
    return (x[:t_p].reshape(n_bp, n_sp, D_MODEL), x[t_p:].reshape(n_bs, n_ss, D_MODEL),
            jnp.stack(s5_p), jnp.stack(s5_s), jnp.stack(gmv_s), jnp.stack(pool_p), jnp.stack(pool_s),
            jnp.stack(k_p), jnp.stack(v_p), jnp.stack(k_s), jnp.stack(v_s))
```

```python
import functools
import math

import jax
import jax.numpy as jnp
from jax import lax
from jax.experimental import pallas as pl
from jax.experimental.pallas import tpu as pltpu

F32 = jnp.float32
BF16 = jnp.bfloat16

D_MODEL = 1024
MIX_HALF = D_MODEL // 2
S5_GROUP_CH = 16
S5_GROUPS = MIX_HALF // S5_GROUP_CH
S5_STATE = 64
GM_CHUNK = 128
GM_GROUPS = 4
GM_CH = MIX_HALF // GM_GROUPS
POOL_WINDOWS = (2, 4, 8, 16)
POOL_CH = MIX_HALF // len(POOL_WINDOWS)
POOL_BUF = max(POOL_WINDOWS) - 1
ATT_HEADS = 8
HEAD_DIM = MIX_HALF // ATT_HEADS
ROT_DIM = HEAD_DIM // 4
ROPE_THETA = 500000.0
MOBA_BLOCK = 256
MOBA_TOPK = 3
N_EXPERTS = 32
TOPK = 4
SWIGLU_LIMIT = 7.0
SWIGLU_ALPHA = 1.702
LN_EPS = 1e-5
PAGE_SIZE = 128

LANES = 128
SUBLANES = 8
VMEM_LIMIT = 56 * 1024 * 1024

S5_OCT = 4
S5_PAIRS = S5_GROUPS * S5_STATE // LANES
NEG_INF = float("-inf")


def _cparams(sem):
    return pltpu.CompilerParams(dimension_semantics=sem, vmem_limit_bytes=VMEM_LIMIT)


def _gelu(x):
    return 0.5 * x * (1.0 + jnp.tanh(math.sqrt(2.0 / math.pi) * (x + 0.044715 * (x * x * x))))


def _sigmoid(x):
    return 1.0 / (1.0 + jnp.exp(-x))


PROJ_TILE = 640


def _proj_kernel(x_ref, w_ref, o_ref):
    o_ref[...] = jnp.dot(x_ref[...].astype(BF16), w_ref[...], preferred_element_type=F32)


def _proj(x, w_bf16, tm):
    t, k = x.shape
    n = w_bf16.shape[1]
    return pl.pallas_call(
        _proj_kernel,
        grid=(t // tm,),
        in_specs=[pl.BlockSpec((tm, k), lambda i: (i, 0)),
                  pl.BlockSpec((k, n), lambda i: (0, 0))],
        out_specs=pl.BlockSpec((tm, n), lambda i: (i, 0)),
        out_shape=jax.ShapeDtypeStruct((t, n), F32),
        compiler_params=_cparams(("parallel",)),
    )(x, w_bf16)


def _s5_params(lam_re, lam_im, b_re, b_im, c_re, c_im, log_dt):
    dt = jnp.exp(log_dt.astype(F32))[:, None]
    lam = lax.complex(lam_re.astype(F32), lam_im.astype(F32))
    lam_bar = jnp.exp(lam * dt)
    b_bar = ((lam_bar - 1.0) / lam)[..., None] * lax.complex(b_re.astype(F32), b_im.astype(F32))
    eye = jnp.eye(SUBLANES, dtype=F32)
    bb = b_bar.reshape(S5_OCT, 8, S5_STATE, S5_GROUP_CH)

    def bdiag_b(t):
        return jnp.einsum('qgph,gk->qghkp', t, eye).reshape(S5_OCT, 128, 512)

    bw = jnp.concatenate([bdiag_b(bb.real), bdiag_b(bb.imag)], axis=-1).astype(BF16)
    cc_re = c_re.astype(F32).reshape(S5_OCT, 8, S5_GROUP_CH, S5_STATE)
    cc_im = c_im.astype(F32).reshape(S5_OCT, 8, S5_GROUP_CH, S5_STATE)

    def bdiag_c(t):
        return jnp.einsum('qghp,gk->qgpkh', t, eye).reshape(S5_OCT, 512, 128)

    cw = jnp.concatenate([bdiag_c(cc_re), -bdiag_c(cc_im)], axis=1).astype(BF16)
    rows = jnp.arange(SUBLANES)
    planes = []
    for d in (1, 2, 4):
        pw = jnp.exp(lam * dt * float(d)).reshape(S5_PAIRS, 1, LANES)
        m = (rows >= d).astype(F32)[None, :, None]
        planes += [pw.real * m, pw.imag * m]
    pw = jnp.exp((lam * dt).reshape(S5_PAIRS, 1, LANES) * (rows + 1).astype(F32)[None, :, None])
    planes += [pw.real, pw.imag]
    coef = jnp.stack(planes, axis=1).astype(F32)
    return bw, cw, coef


def _even_kernel(u_ref, gu_ref, gv_ref, x0_ref, bw_ref, coef_ref, cw_ref, d_ref, wglu_ref, bglu_ref,
                 ng_ref, m_ref, bias_ref, *rest, ts, chunk, per_block_init, with_vn, n_prev):
    rest = rest[n_prev:]
    if with_vn:
        a_ref, b_ref, vn_ref, st_out_ref, st_ref, carry_ref = rest
    else:
        a_ref, b_ref, st_out_ref, st_ref, carry_ref = rest
        vn_ref = None
    n_rb = ts // SUBLANES

    if not per_block_init:
        @pl.when(pl.program_id(1) == 0)
        def _():
            carry_ref[...] = x0_ref[0]

    u = u_ref[...]
    ub = u.astype(BF16)
    for q in range(S5_OCT):
        bu = jnp.dot(ub[:, q * 128:(q + 1) * 128], bw_ref[q], preferred_element_type=F32)
        for c in range(4):
            st_ref[q * 4 + c] = bu[:, c * 128:(c + 1) * 128]
            st_ref[S5_PAIRS + q * 4 + c] = bu[:, 512 + c * 128:512 + (c + 1) * 128]

    def pair_body(j, _):
        cf = coef_ref[j]
        a1r, a1i, a2r, a2i, a4r, a4i, pr, pi = [cf[k] for k in range(8)]

        def rb_body(r, carry):
            cr, ci = carry
            row = pl.multiple_of(r * SUBLANES, SUBLANES)
            xr = st_ref[j, pl.ds(row, SUBLANES), :]
            xi = st_ref[S5_PAIRS + j, pl.ds(row, SUBLANES), :]
            for d, ar, ai in ((1, a1r, a1i), (2, a2r, a2i), (4, a4r, a4i)):
                sr = pltpu.roll(xr, d, 0)
                si = pltpu.roll(xi, d, 0)
                xr, xi = xr + ar * sr - ai * si, xi + ar * si + ai * sr
            if per_block_init:
                cr = x0_ref[j, r]
                ci = x0_ref[S5_PAIRS + j, r]
            xr, xi = xr + pr * cr - pi * ci, xi + pr * ci + pi * cr
            st_ref[j, pl.ds(row, SUBLANES), :] = xr
            st_ref[S5_PAIRS + j, pl.ds(row, SUBLANES), :] = xi
            ncr = jnp.broadcast_to(xr[SUBLANES - 1:SUBLANES, :], (SUBLANES, LANES))
            nci = jnp.broadcast_to(xi[SUBLANES - 1:SUBLANES, :], (SUBLANES, LANES))
            if per_block_init:
                st_out_ref[j, r] = ncr
                st_out_ref[S5_PAIRS + j, r] = nci
            return ncr, nci

        cr, ci = lax.fori_loop(0, n_rb, rb_body, (carry_ref[j], carry_ref[S5_PAIRS + j]),
                               unroll=min(4, n_rb))
        carry_ref[j] = cr
        carry_ref[S5_PAIRS + j] = ci
        return 0

    lax.fori_loop(0, S5_PAIRS, pair_body, 0)

    if not per_block_init:
        @pl.when(pl.program_id(1) == pl.num_programs(1) - 1)
        def _():
            st_out_ref[0] = carry_ref[...]

    ys = []
    for q in range(S5_OCT):
        xq = jnp.concatenate([st_ref[q * 4 + c] for c in range(4)]
                             + [st_ref[S5_PAIRS + q * 4 + c] for c in range(4)], axis=-1)
        ys.append(jnp.dot(xq.astype(BF16), cw_ref[q], preferred_element_type=F32))
    y = jnp.concatenate(ys, axis=-1) + d_ref[...] * u
    g = _gelu(y)
    z = jnp.dot(g.astype(BF16), wglu_ref[...], preferred_element_type=F32) + bglu_ref[...]
    a_ref[...] = g * _sigmoid(z)

    gu = _gelu(gu_ref[...])
    gv = _gelu(gv_ref[...])
    for gi in range(GM_GROUPS):
        sl = slice(gi * GM_CH, (gi + 1) * GM_CH)
        v = gv[:, sl]
        mu = jnp.mean(v, axis=-1, keepdims=True)
        vc = v - mu
        var = jnp.mean(vc * vc, axis=-1, keepdims=True)
        vn = vc * lax.rsqrt(var + LN_EPS) * ng_ref[:, sl]
        if with_vn:
            vn_ref[:, sl] = vn
        vnb = vn.astype(BF16)
        for c in range(ts // chunk):
            rs = slice(c * chunk, (c + 1) * chunk)
            s = jnp.dot(m_ref[gi], vnb[rs], preferred_element_type=F32) + bias_ref[:, sl]
            b_ref[rs, sl] = gu[rs, sl] * s


def _even_mixer(proj, x0, s5p, d_skip, w_glu, b_glu, norm_g, m_mix, bias, *, n_b, n_s, per_block_init,
                row0=0, prev=()):
    bw, cw, coef = s5p
    if per_block_init:
        ts, grid, chunk = n_b * n_s, (1, 1), n_b * n_s
        assert n_s == SUBLANES
        n_rb = ts // SUBLANES
        x0_spec = pl.BlockSpec((2 * S5_PAIRS, n_rb, SUBLANES, LANES), lambda b, t: (0, 0, 0, 0))
        st_shape = (2 * S5_PAIRS, n_rb, SUBLANES, LANES)
        st_spec = x0_spec
    else:
        ts = min(512, n_s)
        grid, chunk = (n_b, n_s // ts), GM_CHUNK
        x0_spec = pl.BlockSpec((1, 2 * S5_PAIRS, SUBLANES, LANES), lambda b, t: (b, 0, 0, 0))
        st_shape = (n_b, 2 * S5_PAIRS, SUBLANES, LANES)
        st_spec = x0_spec
    nt = grid[1]
    with_vn = per_block_init
    blk0 = row0 // ts
    assert row0 % ts == 0

    def rows(col):
        return pl.BlockSpec((ts, MIX_HALF), lambda b, t, col=col: (blk0 + b * nt + t, col))

    def full(a):
        return pl.BlockSpec(a.shape, lambda b, t, nd=a.ndim: (0,) * nd)

    row_out = pl.BlockSpec((ts, MIX_HALF), lambda b, t: (blk0 + b * nt + t, 0))
    t_all = proj.shape[0]
    out_shape = [jax.ShapeDtypeStruct((t_all, MIX_HALF), F32), jax.ShapeDtypeStruct((t_all, MIX_HALF), F32)]
    out_specs = [row_out, row_out]
    if with_vn:
        out_shape.append(jax.ShapeDtypeStruct((n_b * n_s, MIX_HALF), F32))
        out_specs.append(pl.BlockSpec((ts, MIX_HALF), lambda b, t: (b * nt + t, 0)))
    out_shape.append(jax.ShapeDtypeStruct(st_shape, F32))
    out_specs.append(st_spec)
    weights = (bw, coef, cw, d_skip, w_glu, b_glu, norm_g, m_mix, bias)
    n_in = 4 + len(weights)
    outs = pl.pallas_call(
        functools.partial(_even_kernel, ts=ts, chunk=chunk, per_block_init=per_block_init, with_vn=with_vn,
                          n_prev=len(prev)),
        grid=grid,
        in_specs=[rows(0), rows(1), rows(2), x0_spec] + [full(w) for w in weights]
        + [pl.BlockSpec(memory_space=pl.ANY)] * len(prev),
        out_specs=out_specs,
        out_shape=out_shape,
        input_output_aliases={n_in + k: k for k in range(len(prev))},
        scratch_shapes=[pltpu.VMEM((2 * S5_PAIRS, ts, LANES), F32),
                        pltpu.VMEM((2 * S5_PAIRS, SUBLANES, LANES), F32)],
        compiler_params=_cparams(("arbitrary", "arbitrary")),
    )(proj, proj, proj, x0, *weights, *prev)
    if with_vn:
        return outs[0], outs[1], outs[2], outs[3]
    return outs[0], outs[1], None, outs[2]


def _state_to_lanes(x0, n_b):
    re = x0[..., 0].astype(F32).reshape(n_b, S5_PAIRS, LANES)
    im = x0[..., 1].astype(F32).reshape(n_b, S5_PAIRS, LANES)
    return jnp.concatenate([re, im], axis=1)


def _lanes_to_state(st, n_b):
    re = st[:, :S5_PAIRS].reshape(n_b, S5_GROUPS, S5_STATE)
    im = st[:, S5_PAIRS:].reshape(n_b, S5_GROUPS, S5_STATE)
    return jnp.stack([re, im], axis=-1)


def _even_layer_mix(proj, x0, p, i, *, n_b, n_s, sample, row0=0, prev=()):
    s5p = _s5_params(p['s5_lambda_re'][i], p['s5_lambda_im'][i], p['s5_b_re'][i], p['s5_b_im'][i],
                     p['s5_c_re'][i], p['s5_c_im'][i], p['s5_log_dt'][i])
    st0 = _state_to_lanes(x0, n_b)
    if sample:
        chunk = n_s
        x0k = jnp.broadcast_to(st0.transpose(1, 0, 2)[:, :, None, :], (2 * S5_PAIRS, n_b, SUBLANES, LANES))
        w = jnp.tril(p['gm_w_s'][i][:, :chunk, :chunk])
        m_mix = jnp.einsum('bc,gij->gbicj', jnp.eye(n_b, dtype=F32), w).reshape(GM_GROUPS, n_b * chunk, n_b * chunk)
        bias_rows = jnp.tile(p['gm_b_s'][i][:, :chunk].T, (n_b, 1))
    else:
        x0k = jnp.broadcast_to(st0[:, :, None, :], (n_b, 2 * S5_PAIRS, SUBLANES, LANES))
        m_mix = jnp.tril(p['gm_w_s'][i][:, :GM_CHUNK, :GM_CHUNK])
        bias_rows = p['gm_b_s'][i][:, :GM_CHUNK].T
    bias = jnp.repeat(bias_rows.astype(F32), GM_CH, axis=1)
    a, b, vn, st = _even_mixer(
        proj, x0k, s5p, p['s5_d'][i].reshape(1, MIX_HALF).astype(F32), p['s5_w_glu'][i].astype(BF16),
        p['s5_b_glu'][i].reshape(1, MIX_HALF).astype(F32), p['gm_norm_g'][i].reshape(1, MIX_HALF).astype(F32),
        m_mix.astype(BF16), bias, n_b=n_b, n_s=n_s, per_block_init=sample, row0=row0, prev=prev)
    if sample:
        st = st[:, :, 0, :].transpose(1, 0, 2)
    else:
        st = st[:, :, 0, :]
    return a, b, vn, _lanes_to_state(st, n_b)


POOL_HALO = 16


def _pool_kernel(c_ref, halo_ref, w_ref, scale_ref, *rest, ts, base):
    o_ref, hist_ref = rest[-2:]
    t = pl.program_id(1)

    @pl.when(t == 0)
    def _():
        hist_ref[0:POOL_HALO, :] = halo_ref[0]

    @pl.when(t > 0)
    def _():
        hist_ref[0:POOL_HALO, :] = hist_ref[ts:ts + POOL_HALO, :]

    hist_ref[POOL_HALO:POOL_HALO + ts, :] = c_ref[...]
    pos = base + t * ts + lax.broadcasted_iota(jnp.int32, (ts, 1), 0)
    for g, win in enumerate(POOL_WINDOWS):
        sl = slice(g * POOL_CH, (g + 1) * POOL_CH)
        x = hist_ref[POOL_HALO:POOL_HALO + ts, sl]
        acc = x
        for d in range(1, win):
            acc = acc + hist_ref[POOL_HALO - d:POOL_HALO - d + ts, sl]
        cnt = jnp.minimum(pos + 1, win).astype(F32)
        pooled = acc / cnt - x
        y = jnp.dot(pooled.astype(BF16), w_ref[g], preferred_element_type=F32)
        o_ref[:, sl] = y * scale_ref[:, sl]


def _pool_mixer(proj, halo, w, scale, *, n_b, n_s, base, row0=0, prev=()):
    ts = min(512, n_s)
    nt = n_s // ts
    blk0 = row0 // ts
    assert row0 % ts == 0
    return pl.pallas_call(
        functools.partial(_pool_kernel, ts=ts, base=base),
        grid=(n_b, nt),
        in_specs=[pl.BlockSpec((ts, MIX_HALF), lambda b, t: (blk0 + b * nt + t, 0)),
                  pl.BlockSpec((1, POOL_HALO, MIX_HALF), lambda b, t: (b, 0, 0)),
                  pl.BlockSpec(w.shape, lambda b, t: (0, 0, 0)),
                  pl.BlockSpec((1, MIX_HALF), lambda b, t: (0, 0))]
        + [pl.BlockSpec(memory_space=pl.ANY)] * len(prev),
        out_specs=pl.BlockSpec((ts, MIX_HALF), lambda b, t: (blk0 + b * nt + t, 0)),
        out_shape=jax.ShapeDtypeStruct((proj.shape[0], MIX_HALF), F32),
        input_output_aliases={4 + k: k for k in range(len(prev))},
        scratch_shapes=[pltpu.VMEM((POOL_HALO + ts, MIX_HALF), F32)],
        compiler_params=_cparams(("arbitrary", "arbitrary")),
    )(proj, halo, w, scale, *prev)


def _rope_tables(pos):
    half = ROT_DIM // 2
    inv = ROPE_THETA ** (-jnp.arange(half, dtype=F32) * 2.0 / ROT_DIM)
    ang = pos.astype(F32)[:, None] * inv[None, :]
    cos, sin = jnp.cos(ang), jnp.sin(ang)
    n = pos.shape[0]
    one = jnp.ones((n, HEAD_DIM - ROT_DIM), F32)
    zero = jnp.zeros((n, HEAD_DIM - ROT_DIM), F32)
    z8 = jnp.zeros((n, half), F32)
    ca = jnp.concatenate([cos, cos, one], axis=1)
    sp = jnp.concatenate([z8, sin, zero], axis=1)
    sm = jnp.concatenate([-sin, z8, zero], axis=1)
    return tuple(jnp.tile(t, (1, LANES // HEAD_DIM)) for t in (ca, sp, sm))


def _rope_kernel(q_ref, k_ref, v_ref, ca_ref, sp_ref, sm_ref, qo_ref, ko_ref, qb_ref, kb_ref, vb_ref, km_ref):
    ca, sp, sm = ca_ref[...], sp_ref[...], sm_ref[...]
    half = ROT_DIM // 2
    for c in range(MIX_HALF // LANES):
        sl = slice(c * LANES, (c + 1) * LANES)
        for src, dst in ((q_ref, qo_ref), (k_ref, ko_ref)):
            x = src[:, sl]
            dst[:, sl] = x * ca + pltpu.roll(x, half, 1) * sp + pltpu.roll(x, LANES - half, 1) * sm
    q = qo_ref[...]
    k = ko_ref[...]
    qb_ref[...] = (q * (HEAD_DIM ** -0.5)).astype(BF16)
    kb_ref[...] = k.astype(BF16)
    vb_ref[...] = v_ref[...].astype(BF16)
    km_ref[0] = jnp.mean(k, axis=0, keepdims=True)


def _rope(proj, tables):
    t_rows = proj.shape[0]
    ts = MOBA_BLOCK

    def col(c):
        return pl.BlockSpec((ts, MIX_HALF), lambda i, c=c: (i, c))

    tab = pl.BlockSpec((ts, LANES), lambda i: (i, 0))
    row = pl.BlockSpec((ts, MIX_HALF), lambda i: (i, 0))
    f32o = jax.ShapeDtypeStruct((t_rows, MIX_HALF), F32)
    bfo = jax.ShapeDtypeStruct((t_rows, MIX_HALF), BF16)
    outs = pl.pallas_call(
        _rope_kernel,
        grid=(t_rows // ts,),
        in_specs=[col(1), col(2), col(3), tab, tab, tab],
        out_specs=[row, row, row, row, row, pl.BlockSpec((1, 1, MIX_HALF), lambda i: (i, 0, 0))],
        out_shape=[f32o, f32o, bfo, bfo, bfo, jax.ShapeDtypeStruct((t_rows // ts, 1, MIX_HALF), F32)],
        compiler_params=_cparams(("parallel",)),
    )(proj, proj, proj, *tables)
    return list(outs[:5]) + [outs[5].reshape(t_rows // ts, MIX_HALF)]


def _top_rows_mask(gate, n_valid_rows, k_top):
    n = gate.shape[0]
    row = lax.broadcasted_iota(jnp.int32, gate.shape, 0)
    live = row < n_valid_rows
    sel = jnp.zeros(gate.shape, jnp.bool_)
    for _ in range(k_top):
        g = jnp.where(live, gate, NEG_INF)
        mx = jnp.max(g, axis=0, keepdims=True)
        first = jnp.min(jnp.where(live & (g == mx), row, n), axis=0, keepdims=True)
        pick = row == first
        sel = sel | pick
        live = live & jnp.logical_not(pick)
    return sel


HEAD_PAIRS = MIX_HALF // LANES


def _moba_prompt_kernel(q_ref, qb_ref, kb_ref, vb_ref, km_ref, _, o_ref,
                        qbd_ref, sel_ref, m_ref, l_ref, acc_ref, *, n_blk):
    qi = pl.program_id(1)
    tq = MOBA_BLOCK
    lane = lax.broadcasted_iota(jnp.int32, (tq, LANES), 1)
    krow = lax.broadcasted_iota(jnp.int32, (tq, 2 * tq), 0)
    qcol = lax.broadcasted_iota(jnp.int32, (tq, 2 * tq), 1) % tq
    causal = krow <= qcol
    nt_dims = (((1,), (1,)), ((), ()))
    tn_dims = (((0,), (0,)), ((), ()))
    row0 = pl.multiple_of(qi * tq, tq)

    def attend(pr, r0, s_mask, first):
        ps = slice(pr * LANES, (pr + 1) * LANES)
        s = lax.dot_general(kb_ref[pl.ds(r0, tq), ps], qbd_ref[pr], nt_dims, preferred_element_type=F32)
        s = jnp.where(s_mask, s, NEG_INF)
        m_blk = jnp.max(s, axis=0, keepdims=True)
        m_new = m_blk if first else jnp.maximum(m_ref[pr], m_blk)
        p = jnp.exp(s - m_new)
        pv = lax.dot_general(vb_ref[pl.ds(r0, tq), ps], p.astype(BF16), tn_dims,
                             preferred_element_type=F32)
        pv_a, pv_b = pv[:HEAD_DIM, :tq], pv[HEAD_DIM:, tq:]
        if first:
            l_ref[pr] = jnp.sum(p, axis=0, keepdims=True)
            acc_ref[pr, 0] = pv_a
            acc_ref[pr, 1] = pv_b
        else:
            alpha = jnp.exp(m_ref[pr] - m_new)
            l_ref[pr] = alpha * l_ref[pr] + jnp.sum(p, axis=0, keepdims=True)
            acc_ref[pr, 0] = alpha[:, :tq] * acc_ref[pr, 0] + pv_a
            acc_ref[pr, 1] = alpha[:, tq:] * acc_ref[pr, 1] + pv_b
        m_ref[pr] = m_new

    for pr in range(HEAD_PAIRS):
        ps = slice(pr * LANES, (pr + 1) * LANES)
        qf = q_ref[:, ps] * (HEAD_DIM ** -0.5)
        qbd_f = jnp.concatenate([jnp.where(lane < HEAD_DIM, qf, 0.0), jnp.where(lane >= HEAD_DIM, qf, 0.0)], axis=0)
        qb = qb_ref[:, ps]
        zero = jnp.zeros_like(qb)
        qbd_ref[pr] = jnp.concatenate([jnp.where(lane < HEAD_DIM, qb, zero), jnp.where(lane >= HEAD_DIM, qb, zero)],
                                      axis=0)
        gate = lax.dot_general(km_ref[:, ps], qbd_f, nt_dims, precision=lax.Precision.HIGHEST,
                               preferred_element_type=F32)
        sel_ref[pr] = _top_rows_mask(gate, qi, MOBA_TOPK).astype(F32)
        attend(pr, row0, causal, True)

    def blk_body(j, _):
        r0 = pl.multiple_of(j * tq, tq)
        for pr in range(HEAD_PAIRS):
            attend(pr, r0, sel_ref[pr, pl.ds(j, 1), :] > 0.5, False)
        return 0

    lax.fori_loop(0, qi, blk_body, 0)
    for pr in range(HEAD_PAIRS):
        l = l_ref[pr]
        out_t = jnp.concatenate([acc_ref[pr, 0] / l[:, :tq], acc_ref[pr, 1] / l[:, tq:]], axis=0)
        o_ref[:, pr * LANES:(pr + 1) * LANES] = out_t.T


def _moba_prompt(q_rot, qb, kb, vb, kmean, d_prev, *, n_b, n_s):
    n_blk = n_s // MOBA_BLOCK
    tq = MOBA_BLOCK
    qspec = pl.BlockSpec((tq, MIX_HALF), lambda b, i: (b * n_blk + i, 0))
    kvspec = pl.BlockSpec((n_s, MIX_HALF), lambda b, i: (b, 0))
    return pl.pallas_call(
        functools.partial(_moba_prompt_kernel, n_blk=n_blk),
        grid=(n_b, n_blk),
        in_specs=[qspec, qspec, kvspec, kvspec, pl.BlockSpec((n_blk, MIX_HALF), lambda b, i: (b, 0)),
                  pl.BlockSpec(memory_space=pl.ANY)],
        out_specs=qspec,
        out_shape=jax.ShapeDtypeStruct(d_prev.shape, F32),
        scratch_shapes=[pltpu.VMEM((HEAD_PAIRS, 2 * tq, LANES), BF16),
                        pltpu.VMEM((HEAD_PAIRS, n_blk, 2 * tq), F32),
                        pltpu.VMEM((HEAD_PAIRS, 1, 2 * tq), F32),
                        pltpu.VMEM((HEAD_PAIRS, 1, 2 * tq), F32),
                        pltpu.VMEM((HEAD_PAIRS, 2, HEAD_DIM, tq), F32)],
        input_output_aliases={5: 0},
        compiler_params=_cparams(("arbitrary", "arbitrary")),
    )(q_rot, qb, kb, vb, kmean, d_prev)


PAGES_PER_STEP = 8
BLOCK_PAGES = MOBA_BLOCK // PAGE_SIZE


def _moba_sample_kernel(pt_ref, *refs, n_blk, n_q):
    del pt_ref
    kp = refs[:PAGES_PER_STEP]
    vp = refs[PAGES_PER_STEP:2 * PAGES_PER_STEP]
    qbd_ref, qbdf_ref, kn_ref, vn_ref, _, o_ref, oacc_ref, m_ref, l_ref, km_ref = refs[2 * PAGES_PER_STEP:]
    c = pl.program_id(1)
    ncol = ATT_HEADS * n_q
    blocks_per_step = PAGES_PER_STEP // BLOCK_PAGES
    qbd = qbd_ref[0]
    for blk in range(blocks_per_step):
        n = c * blocks_per_step + blk
        kf = jnp.concatenate([kp[blk * BLOCK_PAGES + j][0, 0] for j in range(BLOCK_PAGES)], axis=0)
        vf = jnp.concatenate([vp[blk * BLOCK_PAGES + j][0, 0] for j in range(BLOCK_PAGES)], axis=0)
        km_ref[pl.ds(n, 1), :] = jnp.mean(kf, axis=0, keepdims=True)
        s = jnp.dot(kf.astype(BF16), qbd, preferred_element_type=F32)
        m = jnp.max(s, axis=0, keepdims=True)
        p = jnp.exp(s - m)
        m_ref[pl.ds(n, 1), :] = m
        l_ref[pl.ds(n, 1), :] = jnp.sum(p, axis=0, keepdims=True)
        oacc_ref[n] = lax.dot_general(p.astype(BF16), vf.astype(BF16), (((0,), (0,)), ((), ())),
                                      preferred_element_type=F32)

    @pl.when(c == pl.num_programs(1) - 1)
    def _():
        gate = jnp.dot(km_ref[...], qbdf_ref[0], precision=lax.Precision.HIGHEST,
                       preferred_element_type=F32)
        sel = _top_rows_mask(gate, n_blk, MOBA_TOPK)
        s_own = jnp.dot(kn_ref[...].astype(BF16), qbd, preferred_element_type=F32)
        krow = lax.broadcasted_iota(jnp.int32, (n_q, ncol), 0)
        qidx = lax.broadcasted_iota(jnp.int32, (n_q, ncol), 1) % n_q
        s_own = jnp.where(krow <= qidx, s_own, NEG_INF)
        m_all = m_ref[...]
        m_fin = jnp.maximum(jnp.max(jnp.where(sel, m_all, NEG_INF), axis=0, keepdims=True),
                            jnp.max(s_own, axis=0, keepdims=True))
        w = jnp.where(sel, jnp.exp(m_all - m_fin), 0.0)
        p_own = jnp.exp(s_own - m_fin)
        l_fin = jnp.sum(w * l_ref[...], axis=0, keepdims=True) + jnp.sum(p_own, axis=0, keepdims=True)
        w = w / l_fin
        p_own = p_own / l_fin
        wpad = jnp.concatenate([w, jnp.zeros((LANES - n_blk, ncol), F32)], axis=0)
        wpad = jnp.concatenate([wpad, jnp.zeros((LANES, LANES - ncol), F32)], axis=1)
        wt = wpad.T
        acc = lax.dot_general(p_own.astype(BF16), vn_ref[...].astype(BF16), (((0,), (0,)), ((), ())),
                              preferred_element_type=F32)
        for n in range(n_blk):
            acc = acc + wt[:ncol, n:n + 1] * oacc_ref[n]
        head = lax.broadcasted_iota(jnp.int32, (n_q, MIX_HALF), 1) // HEAD_DIM
        out = jnp.zeros((n_q, MIX_HALF), F32)
        for h in range(ATT_HEADS):
            out = out + jnp.where(head == h, acc[h * n_q:(h + 1) * n_q], 0.0)
        o_ref[...] = out


def _moba_sample(page_table, cache_k, cache_v, layer_i, q_rot, k_rot, proj, d_prev, *, n_b, n_q, row0):
    n_pages = page_table.shape[1]
    n_blk = n_pages // BLOCK_PAGES
    ncol = ATT_HEADS * n_q
    assert n_blk + 1 <= LANES and ncol <= LANES and n_pages % PAGES_PER_STEP == 0 and row0 % n_q == 0
    blk0 = row0 // n_q
    q4 = (q_rot[row0:row0 + n_b * n_q] * (HEAD_DIM ** -0.5)).reshape(n_b, n_q, ATT_HEADS, HEAD_DIM)
    qbdf = jnp.einsum('bihd,hg->bhdgi', q4, jnp.eye(ATT_HEADS, dtype=F32)).reshape(n_b, MIX_HALF, ncol)

    def page_spec(j):
        return pl.BlockSpec((1, 1, PAGE_SIZE, MIX_HALF),
                            lambda b, c, pt, j=j: (layer_i, pt[b, c * PAGES_PER_STEP + j], 0, 0))

    per_b3 = lambda shape: pl.BlockSpec(shape, lambda b, c, pt: (b, 0, 0))
    grid_spec = pltpu.PrefetchScalarGridSpec(
        num_scalar_prefetch=1,
        grid=(n_b, n_pages // PAGES_PER_STEP),
        in_specs=[page_spec(j) for j in range(PAGES_PER_STEP)] * 2
        + [per_b3((1, MIX_HALF, ncol)), per_b3((1, MIX_HALF, ncol)),
           pl.BlockSpec((n_q, MIX_HALF), lambda b, c, pt: (blk0 + b, 0)),
           pl.BlockSpec((n_q, MIX_HALF), lambda b, c, pt: (blk0 + b, 3)),
           pl.BlockSpec(memory_space=pl.ANY)],
        out_specs=pl.BlockSpec((n_q, MIX_HALF), lambda b, c, pt: (blk0 + b, 0)),
        scratch_shapes=[pltpu.VMEM((n_blk, ncol, MIX_HALF), F32), pltpu.VMEM((n_blk, ncol), F32),
                        pltpu.VMEM((n_blk, ncol), F32), pltpu.VMEM((n_blk, MIX_HALF), F32)])
    return pl.pallas_call(
        functools.partial(_moba_sample_kernel, n_blk=n_blk, n_q=n_q),
        grid_spec=grid_spec,
        out_shape=jax.ShapeDtypeStruct(d_prev.shape, F32),
        input_output_aliases={1 + 2 * PAGES_PER_STEP + 4: 0},
        compiler_params=_cparams(("arbitrary", "arbitrary")),
    )(page_table, *([cache_k] * PAGES_PER_STEP), *([cache_v] * PAGES_PER_STEP), qbdf.astype(BF16), qbdf,
      k_rot, proj, d_prev)


TOK_TILE = 256
SEG_ALIGN = SUBLANES
ASG_TILE = -(-(TOK_TILE * TOPK + N_EXPERTS * (SEG_ALIGN - 1)) // LANES) * LANES
EXP_TILE = 256


def _layer_norm_rows(h, g, b):
    mu = jnp.mean(h, axis=-1, keepdims=True)
    hc = h - mu
    var = jnp.mean(hc * hc, axis=-1, keepdims=True)
    return hc * lax.rsqrt(var + LN_EPS) * g + b


def _route_kernel(x_ref, a_ref, b_ref, wo_ref, g_ref, bt_ref, wr_ref, br_ref,
                  x1_ref, xs_ref, dg_ref, cnt_ref, *, alpha):
    h = (alpha * x_ref[...]
         + jnp.dot(a_ref[...].astype(BF16), wo_ref[0:MIX_HALF, :], preferred_element_type=F32)
         + jnp.dot(b_ref[...].astype(BF16), wo_ref[MIX_HALF:, :], preferred_element_type=F32))
    x1 = _layer_norm_rows(h, g_ref[...], bt_ref[...])
    x1_ref[...] = x1
    logits = lax.dot_general(wr_ref[...], x1, (((1,), (1,)), ((), ())), precision=lax.Precision.HIGHEST,
                             preferred_element_type=F32) + br_ref[...]
    row = lax.broadcasted_iota(jnp.int32, logits.shape, 0)
    g = logits
    picks, vals = [], []
    for _ in range(TOPK):
        mx = jnp.max(g, axis=0, keepdims=True)
        first = jnp.min(jnp.where(g == mx, row, N_EXPERTS), axis=0, keepdims=True)
        pick = row == first
        picks.append(pick)
        vals.append(mx)
        g = jnp.where(pick, NEG_INF, g)
    es = [jnp.exp(v - vals[0]) for v in vals]
    den = es[0] + es[1] + es[2] + es[3]
    onehot = [p.astype(F32) for p in picks]
    member = onehot[0] + onehot[1] + onehot[2] + onehot[3]
    t_r = lax.broadcasted_iota(jnp.int32, (TOK_TILE, TOK_TILE), 0)
    t_c = lax.broadcasted_iota(jnp.int32, (TOK_TILE, TOK_TILE), 1)
    before = (t_r < t_c).astype(BF16)
    rank = jnp.dot(member.astype(BF16), before, preferred_element_type=F32)
    cnt = jnp.sum(member, axis=1, keepdims=True)
    cnt = jnp.ceil(cnt * (1.0 / SEG_ALIGN)) * SEG_ALIGN
    e_r = lax.broadcasted_iota(jnp.int32, (N_EXPERTS, N_EXPERTS), 0)
    e_c = lax.broadcasted_iota(jnp.int32, (N_EXPERTS, N_EXPERTS), 1)
    lower = (e_c < e_r).astype(F32)
    off = jnp.dot(lower, jnp.broadcast_to(cnt, (N_EXPERTS, TOK_TILE)), precision=lax.Precision.HIGHEST,
                  preferred_element_type=F32)
    slot = off + rank
    dests = [jnp.sum(oh * slot, axis=0, keepdims=True) for oh in onehot]
    r_iota = lax.broadcasted_iota(jnp.int32, (ASG_TILE, TOK_TILE), 0)
    perm = jnp.zeros((ASG_TILE, TOK_TILE), F32)
    for d in dests:
        perm = perm + (r_iota == d.astype(jnp.int32)).astype(F32)
    xs_ref[...] = jnp.dot(perm.astype(BF16), x1.astype(BF16), preferred_element_type=F32)
    dg_ref[0] = jnp.concatenate(dests + [e / den for e in es], axis=0)
    cnt_ref[0] = jnp.broadcast_to(cnt, (N_EXPERTS, LANES))


def _route(x, mix_a, mix_b, w_out_bf, ln_g, ln_b, wr_t, br, *, alpha):
    t = x.shape[0]
    nt = t // TOK_TILE
    full2 = lambda a: pl.BlockSpec(a.shape, lambda i: (0, 0))
    return pl.pallas_call(
        functools.partial(_route_kernel, alpha=alpha),
        grid=(nt,),
        in_specs=[pl.BlockSpec((TOK_TILE, D_MODEL), lambda i: (i, 0)),
                  pl.BlockSpec((TOK_TILE, MIX_HALF), lambda i: (i, 0)),
                  pl.BlockSpec((TOK_TILE, MIX_HALF), lambda i: (i, 0)),
                  full2(w_out_bf), full2(ln_g), full2(ln_b), full2(wr_t), full2(br)],
        out_specs=[pl.BlockSpec((TOK_TILE, D_MODEL), lambda i: (i, 0)),
                   pl.BlockSpec((ASG_TILE, D_MODEL), lambda i: (i, 0)),
                   pl.BlockSpec((1, 2 * TOPK, TOK_TILE), lambda i: (i, 0, 0)),
                   pl.BlockSpec((1, N_EXPERTS, LANES), lambda i: (i, 0, 0))],
        out_shape=[jax.ShapeDtypeStruct((t, D_MODEL), F32),
                   jax.ShapeDtypeStruct((nt * ASG_TILE, D_MODEL), F32),
                   jax.ShapeDtypeStruct((nt, 2 * TOPK, TOK_TILE), F32),
                   jax.ShapeDtypeStruct((nt, N_EXPERTS, LANES), F32)],
        compiler_params=_cparams(("parallel",)),
    )(x, mix_a, mix_b, w_out_bf, ln_g, ln_b, wr_t, br)


SEG_PIECES = tuple(1 << b for b in range(TOK_TILE.bit_length() - 1, SEG_ALIGN.bit_length() - 2, -1))


def _expert_kernel(te_ref, nu_ref, lo_ref, hi_ref, src_ref, dst_ref, len_ref, tot_ref,
                   xs_ref, wg_ref, bg_ref, wu_ref, bu_ref, wd_ref, bd_ref, ys_ref,
                   xbuf, ybuf, zbuf, wgb_ref, wub_ref, wdb_ref, gsem, ssem, zsem, *, n_tok_tiles):
    m = pl.program_id(0)
    n_used = nu_ref[0]

    def zero_tail(i, start):
        n = ASG_TILE - tot_ref[i]
        for size in SEG_PIECES:
            @pl.when((n & size) != 0)
            def _():
                row = pl.multiple_of(i * ASG_TILE + tot_ref[i] + (n & ~(2 * size - 1)), SEG_ALIGN)
                cp = pltpu.make_async_copy(zbuf.at[pl.ds(0, size)], ys_ref.at[pl.ds(row, size)], zsem)
                cp.start() if start else cp.wait()

    @pl.when(m == 0)
    def _():
        zbuf[...] = jnp.zeros(zbuf.shape, F32)

    @pl.when((m >= 1) & (m - 1 < n_tok_tiles))
    def _():
        zero_tail(m - 1, False)

    @pl.when(m < n_tok_tiles)
    def _():
        zero_tail(m, True)

    def for_pieces(t, fn):
        row0 = t * EXP_TILE

        def seg_body(s, _):
            start = jnp.maximum(dst_ref[s], row0)
            n = jnp.minimum(dst_ref[s] + len_ref[s], row0 + EXP_TILE) - start
            base_src = src_ref[s] + (start - dst_ref[s])
            base_dst = start - row0
            for size in SEG_PIECES:
                @pl.when((n & size) != 0)
                def _():
                    done = n & ~(2 * size - 1)
                    fn(pl.multiple_of(base_src + done, SEG_ALIGN), pl.multiple_of(base_dst + done, SEG_ALIGN), size)
            return 0

        lax.fori_loop(lo_ref[t], hi_ref[t], seg_body, 0)

    def gather(t, start):
        slot = t % 2

        def fn(row, r, size):
            cp = pltpu.make_async_copy(xs_ref.at[pl.ds(row, size)], xbuf.at[slot, pl.ds(r, size)], gsem.at[slot])
            cp.start() if start else cp.wait()
        for_pieces(t, fn)

    def scatter(t, start):
        slot = t % 2

        def fn(row, r, size):
            cp = pltpu.make_async_copy(ybuf.at[slot, pl.ds(r, size)], ys_ref.at[pl.ds(row, size)], ssem.at[slot])
            cp.start() if start else cp.wait()
        for_pieces(t, fn)

    @pl.when((m == 0) & (n_used > 0))
    def _():
        gather(0, True)

    @pl.when(m + 1 < n_used)
    def _():
        gather(m + 1, True)

    @pl.when((m >= 2) & (m - 2 < n_used))
    def _():
        scatter(m - 2, False)

    @pl.when(m < n_used)
    def _():
        gather(m, False)
        prev = te_ref[jnp.maximum(m - 1, 0)]

        @pl.when((m == 0) | (te_ref[m] != prev))
        def _():
            wgb_ref[...] = wg_ref[0, 0].astype(BF16)
            wub_ref[...] = wu_ref[0, 0].astype(BF16)
            wdb_ref[...] = wd_ref[0, 0].astype(BF16)

        slot = m % 2
        xb = xbuf[slot].astype(BF16)
        gt = jnp.minimum(jnp.dot(xb, wgb_ref[...], preferred_element_type=F32) + bg_ref[0, 0], SWIGLU_LIMIT)
        up = jnp.clip(jnp.dot(xb, wub_ref[...], preferred_element_type=F32) + bu_ref[0, 0],
                      -SWIGLU_LIMIT, SWIGLU_LIMIT)
        act = gt * _sigmoid(SWIGLU_ALPHA * gt) * (up + 1.0)
        ybuf[slot] = jnp.dot(act.astype(BF16), wdb_ref[...], preferred_element_type=F32) + bd_ref[0, 0]
        scatter(m, True)


def _experts(tile_expert, n_used, seg_lo, seg_hi, seg_src, seg_dst, seg_len, tile_total, xs, layer,
             w_gate, b_gate, w_up, b_up, w_down, b_down):
    n_tiles = tile_expert.shape[0]
    n_tok_tiles = tile_total.shape[0]
    assert n_tiles > n_tok_tiles
    wspec = pl.BlockSpec((1, 1, D_MODEL, D_MODEL), lambda m, te, *_: (layer, te[m], 0, 0))
    bspec = pl.BlockSpec((1, 1, 1, D_MODEL), lambda m, te, *_: (layer, te[m], 0, 0))
    grid_spec = pltpu.PrefetchScalarGridSpec(
        num_scalar_prefetch=8, grid=(n_tiles,),
        in_specs=[pl.BlockSpec(memory_space=pl.ANY), wspec, bspec, wspec, bspec, wspec, bspec],
        out_specs=pl.BlockSpec(memory_space=pl.ANY),
        scratch_shapes=[pltpu.VMEM((2, EXP_TILE, D_MODEL), F32), pltpu.VMEM((2, EXP_TILE, D_MODEL), F32),
                        pltpu.VMEM((TOK_TILE, D_MODEL), F32)]
        + [pltpu.VMEM((D_MODEL, D_MODEL), BF16)] * 3
        + [pltpu.SemaphoreType.DMA((2,)), pltpu.SemaphoreType.DMA((2,)), pltpu.SemaphoreType.DMA(())])
    depth = w_gate.shape[0]
    bshape = (depth, N_EXPERTS, 1, D_MODEL)
    return pl.pallas_call(
        functools.partial(_expert_kernel, n_tok_tiles=n_tok_tiles),
        grid_spec=grid_spec,
        out_shape=jax.ShapeDtypeStruct(xs.shape, F32),
        compiler_params=_cparams(("arbitrary",)),
    )(tile_expert, n_used, seg_lo, seg_hi, seg_src, seg_dst, seg_len, tile_total, xs,
      w_gate, b_gate.reshape(bshape), w_up, b_up.reshape(bshape), w_down, b_down.reshape(bshape))


def _combine_kernel(x1_ref, ys_ref, dg_ref, g_ref, bt_ref, o_ref, *, alpha):
    dg = dg_ref[0]
    r_iota = lax.broadcasted_iota(jnp.int32, (ASG_TILE, TOK_TILE), 0)
    comb = jnp.zeros((ASG_TILE, TOK_TILE), F32)
    for k in range(TOPK):
        comb = comb + jnp.where(r_iota == dg[k:k + 1, :].astype(jnp.int32), dg[TOPK + k:TOPK + k + 1, :], 0.0)
    ffn = lax.dot_general(comb.astype(BF16), ys_ref[...].astype(BF16), (((0,), (0,)), ((), ())),
                          preferred_element_type=F32)
    o_ref[...] = _layer_norm_rows(alpha * x1_ref[...] + ffn, g_ref[...], bt_ref[...])


def _combine(x1, ys, dg, ln_g, ln_b, *, alpha):
    t = x1.shape[0]
    nt = t // TOK_TILE
    return pl.pallas_call(
        functools.partial(_combine_kernel, alpha=alpha),
        grid=(nt,),
        in_specs=[pl.BlockSpec((TOK_TILE, D_MODEL), lambda i: (i, 0)),
                  pl.BlockSpec((ASG_TILE, D_MODEL), lambda i: (i, 0)),
                  pl.BlockSpec((1, 2 * TOPK, TOK_TILE), lambda i: (i, 0, 0)),
                  pl.BlockSpec((1, D_MODEL), lambda i: (0, 0)),
                  pl.BlockSpec((1, D_MODEL), lambda i: (0, 0))],
        out_specs=pl.BlockSpec((TOK_TILE, D_MODEL), lambda i: (i, 0)),
        out_shape=jax.ShapeDtypeStruct((t, D_MODEL), F32),
        compiler_params=_cparams(("parallel",)),
    )(x1, ys, dg, ln_g, ln_b)


def _channel_mix(x, mix_a, mix_b, p, layer, alpha):
    t = x.shape[0]
    nt = t // TOK_TILE
    x1, xs, dg, cnt = _route(
        x, mix_a, mix_b, p['w_out'][layer].astype(BF16), p['ln_g'][layer, 0].reshape(1, D_MODEL),
        p['ln_b'][layer, 0].reshape(1, D_MODEL), p['router_w'][layer].T.astype(F32),
        p['router_b'][layer].reshape(N_EXPERTS, 1).astype(F32), alpha=alpha)
    cnt = cnt[:, :, 0].astype(jnp.int32)
    local_off = jnp.cumsum(cnt, axis=1) - cnt
    tile_rows = (jnp.arange(nt, dtype=jnp.int32) * ASG_TILE)[:, None] + local_off
    total = jnp.sum(cnt, axis=0)
    padded = (total + EXP_TILE - 1) // EXP_TILE * EXP_TILE
    pend = jnp.cumsum(padded)
    expert_rows = (pend - padded)[None, :] + jnp.cumsum(cnt, axis=0) - cnt
    n_tiles = -(-(nt * (TOK_TILE * TOPK + N_EXPERTS * (SEG_ALIGN - 1))) // EXP_TILE) + N_EXPERTS + 2
    tile_start = (jnp.arange(n_tiles, dtype=jnp.int32) * EXP_TILE)[:, None]
    count_below = lambda a, bound: jnp.sum((a[None, :] < bound).astype(jnp.int32), axis=1)
    tile_expert = jnp.minimum(count_below(pend, tile_start + 1), N_EXPERTS - 1)
    n_used = (pend[-1:] // EXP_TILE).astype(jnp.int32)
    seg_src = tile_rows.T.reshape(-1)
    seg_dst = expert_rows.T.reshape(-1).astype(jnp.int32)
    seg_len = cnt.T.reshape(-1)
    seg_lo = count_below(seg_dst + seg_len, tile_start + 1)
    seg_hi = count_below(seg_dst, tile_start + EXP_TILE)
    ys = _experts(tile_expert, n_used, seg_lo, seg_hi, seg_src, seg_dst, seg_len, jnp.sum(cnt, axis=1), xs, layer,
                  p['moe_w_gate'], p['moe_b_gate'], p['moe_w_up'], p['moe_b_up'], p['moe_w_down'], p['moe_b_down'])
    return _combine(x1, ys, dg, p['ln_g'][layer, 1].reshape(1, D_MODEL), p['ln_b'][layer, 1].reshape(1, D_MODEL),
                    alpha=alpha)


def kernel(x_prompt, x_sample, state_s5, state_pool, cache_k, cache_v, page_table, w_in_ab, s5_lambda_re, s5_lambda_im, s5_b_re, s5_b_im, s5_c_re, s5_c_im, s5_d, s5_log_dt, s5_w_glu, s5_b_glu, gm_norm_g, gm_w_s, gm_b_s, w_in_cd, pool_w, pool_scale, w_out, ln_g, ln_b, router_w, router_b, moe_w_gate, moe_b_gate, moe_w_up, moe_b_up, moe_w_down, moe_b_down):
    p = dict(w_in_ab=w_in_ab, s5_lambda_re=s5_lambda_re, s5_lambda_im=s5_lambda_im,
             s5_b_re=s5_b_re, s5_b_im=s5_b_im, s5_c_re=s5_c_re, s5_c_im=s5_c_im, s5_d=s5_d,
             s5_log_dt=s5_log_dt, s5_w_glu=s5_w_glu, s5_b_glu=s5_b_glu, gm_norm_g=gm_norm_g,
             gm_w_s=gm_w_s, gm_b_s=gm_b_s, w_in_cd=w_in_cd, pool_w=pool_w, pool_scale=pool_scale,
             w_out=w_out, ln_g=ln_g, ln_b=ln_b, router_w=router_w, router_b=router_b,
             moe_w_gate=moe_w_gate, moe_b_gate=moe_b_gate, moe_w_up=moe_w_up, moe_b_up=moe_b_up,
             moe_w_down=moe_w_down, moe_b_down=moe_b_down)
    n_bp, n_sp, _ = x_prompt.shape
    n_bs, n_ss, _ = x_sample.shape
    t_p, t_s = n_bp * n_sp, n_bs * n_ss
    depth = w_out.shape[0]
    alpha = (2 * depth) ** 0.25
    past_len = page_table.shape[1] * PAGE_SIZE
    x = jnp.concatenate([x_prompt.reshape(t_p, D_MODEL), x_sample.reshape(t_s, D_MODEL)], axis=0)
    zero_s5 = jnp.zeros((n_bp, S5_GROUPS, S5_STATE, 2), F32)
    pos = jnp.concatenate([jnp.tile(jnp.arange(n_sp), n_bp), jnp.tile(past_len + jnp.arange(n_ss), n_bs)])
    rope_tables = _rope_tables(pos)
    zeros_half = jnp.zeros((t_p + t_s, MIX_HALF), F32)
    cache_k2 = cache_k.reshape(cache_k.shape[0], cache_k.shape[1], PAGE_SIZE, MIX_HALF)
    cache_v2 = cache_v.reshape(cache_v.shape[0], cache_v.shape[1], PAGE_SIZE, MIX_HALF)
    s5_p, s5_s, gmv_s, pool_p, pool_s, k_p, v_p, k_s, v_s = [], [], [], [], [], [], [], [], []
    for layer in range(depth):
        i = layer // 2
        if layer % 2 == 0:
            proj = _proj(x, w_in_ab[i].astype(BF16), PROJ_TILE)
            a, b, _, st_p = _even_layer_mix(proj, zero_s5, p, i, n_b=n_bp, n_s=n_sp, sample=False,
                                            prev=(zeros_half, zeros_half))
            a, b, vn, st_s = _even_layer_mix(proj, state_s5[i], p, i, n_b=n_bs, n_s=n_ss, sample=True,
                                             row0=t_p, prev=(a, b))
            s5_p.append(st_p)
            s5_s.append(st_s)
            gmv_s.append(vn.reshape(n_bs, n_ss, MIX_HALF))
        else:
            proj = _proj(x, w_in_cd[i].astype(BF16), PROJ_TILE)
            pw = pool_w[i].astype(BF16)
            ps = pool_scale[i].reshape(1, MIX_HALF).astype(F32)
            a = _pool_mixer(proj, jnp.zeros((n_bp, POOL_HALO, MIX_HALF), F32), pw, ps, n_b=n_bp, n_s=n_sp, base=0,
                            prev=(zeros_half,))
            halo = jnp.concatenate([jnp.zeros((n_bs, POOL_HALO - POOL_BUF, MIX_HALF), F32),
                                    state_pool[i].astype(F32)], axis=1)
            a = _pool_mixer(proj, halo, pw, ps, n_b=n_bs, n_s=n_ss, base=POOL_BUF, row0=t_p, prev=(a,))
            q_rot, k_rot, qb, kb, vb, kmean = _rope(proj, rope_tables)
            b = _moba_prompt(q_rot, qb, kb, vb, kmean, zeros_half, n_b=n_bp, n_s=n_sp)
            b = _moba_sample(page_table, cache_k2, cache_v2, i, q_rot, k_rot, proj, b, n_b=n_bs, n_q=n_ss, row0=t_p)
            c_p = proj[:t_p, :MIX_HALF].reshape(n_bp, n_sp, MIX_HALF)
            c_s = proj[t_p:, :MIX_HALF].reshape(n_bs, n_ss, MIX_HALF)
            pool_p.append(c_p[:, -POOL_BUF:])
            pool_s.append(jnp.concatenate([state_pool[i].astype(F32), c_s], axis=1)[:, -POOL_BUF:])
            k_p.append(k_rot[:t_p].reshape(n_bp, n_sp, ATT_HEADS, HEAD_DIM))
            k_s.append(k_rot[t_p:].reshape(n_bs, n_ss, ATT_HEADS, HEAD_DIM))
            v_p.append(proj[:t_p, 3 * MIX_HALF:].reshape(n_bp, n_sp, ATT_HEADS, HEAD_DIM))
            v_s.append(proj[t_p:, 3 * MIX_HALF:].reshape(n_bs, n_ss, ATT_HEADS, HEAD_DIM))
        x = _channel_mix(x, a, b, p, layer, alpha)
    return (x[:t_p].reshape(n_bp, n_sp, D_MODEL), x[t_p:].reshape(n_bs, n_ss, D_MODEL),
            jnp.stack(s5_p), jnp.stack(s5_s), jnp.stack(gmv_s), jnp.stack(pool_p), jnp.stack(pool_s),
            jnp.stack(k_p), jnp.stack(v_p), jnp.stack(k_s), jnp.stack(v_s))
```

```python
import functools
import math

import jax
import jax.numpy as jnp
from jax import lax
from jax.experimental import pallas as pl
from jax.experimental.pallas import tpu as pltpu

F32 = jnp.float32
BF16 = jnp.bfloat16

D_MODEL = 1024
MIX_HALF = D_MODEL // 2
S5_GROUP_CH = 16
S5_GROUPS = MIX_HALF // S5_GROUP_CH
S5_STATE = 64
GM_CHUNK = 128
GM_GROUPS = 4
GM_CH = MIX_HALF // GM_GROUPS
POOL_WINDOWS = (2, 4, 8, 16)
POOL_CH = MIX_HALF // len(POOL_WINDOWS)
POOL_BUF = max(POOL_WINDOWS) - 1
ATT_HEADS = 8
HEAD_DIM = MIX_HALF // ATT_HEADS
ROT_DIM = HEAD_DIM // 4
ROPE_THETA = 500000.0
MOBA_BLOCK = 256
MOBA_TOPK = 3
N_EXPERTS = 32
TOPK = 4
SWIGLU_LIMIT = 7.0
SWIGLU_ALPHA = 1.702
LN_EPS = 1e-5
PAGE_SIZE = 128

LANES = 128
SUBLANES = 8
VMEM_LIMIT = 56 * 1024 * 1024

S5_OCT = 4
S5_PAIRS = S5_GROUPS * S5_STATE // LANES
NEG_INF = float("-inf")


def _cparams(sem):
    return pltpu.CompilerParams(dimension_semantics=sem, vmem_limit_bytes=VMEM_LIMIT)


def _gelu(x):
    return 0.5 * x * (1.0 + jnp.tanh(math.sqrt(2.0 / math.pi) * (x + 0.044715 * (x * x * x))))


def _sigmoid(x):
    return 1.0 / (1.0 + jnp.exp(-x))


PROJ_TILE = 640


def _proj_kernel(x_ref, w_ref, o_ref):
    o_ref[...] = jnp.dot(x_ref[...].astype(BF16), w_ref[...], preferred_element_type=F32)


def _proj(x, w_bf16, tm):
    t, k = x.shape
    n = w_bf16.shape[1]
    return pl.pallas_call(
        _proj_kernel,
        grid=(t // tm,),
        in_specs=[pl.BlockSpec((tm, k), lambda i: (i, 0)),
                  pl.BlockSpec((k, n), lambda i: (0, 0))],
        out_specs=pl.BlockSpec((tm, n), lambda i: (i, 0)),
        out_shape=jax.ShapeDtypeStruct((t, n), F32),
        compiler_params=_cparams(("parallel",)),
    )(x, w_bf16)


def _s5_params(lam_re, lam_im, b_re, b_im, c_re, c_im, log_dt):
    dt = jnp.exp(log_dt.astype(F32))[:, None]
    lam = lax.complex(lam_re.astype(F32), lam_im.astype(F32))
    lam_bar = jnp.exp(lam * dt)
    b_bar = ((lam_bar - 1.0) / lam)[..., None] * lax.complex(b_re.astype(F32), b_im.astype(F32))
    eye = jnp.eye(SUBLANES, dtype=F32)
    bb = b_bar.reshape(S5_OCT, 8, S5_STATE, S5_GROUP_CH)

    def bdiag_b(t):
        return jnp.einsum('qgph,gk->qghkp', t, eye).reshape(S5_OCT, 128, 512)

    bw = jnp.concatenate([bdiag_b(bb.real), bdiag_b(bb.imag)], axis=-1).astype(BF16)
    cc_re = c_re.astype(F32).reshape(S5_OCT, 8, S5_GROUP_CH, S5_STATE)
    cc_im = c_im.astype(F32).reshape(S5_OCT, 8, S5_GROUP_CH, S5_STATE)

    def bdiag_c(t):
        return jnp.einsum('qghp,gk->qgpkh', t, eye).reshape(S5_OCT, 512, 128)

    cw = jnp.concatenate([bdiag_c(cc_re), -bdiag_c(cc_im)], axis=1).astype(BF16)
    rows = jnp.arange(SUBLANES)
    planes = []
    for d in (1, 2, 4):
        pw = jnp.exp(lam * dt * float(d)).reshape(S5_PAIRS, 1, LANES)
        m = (rows >= d).astype(F32)[None, :, None]
        planes += [pw.real * m, pw.imag * m]
    pw = jnp.exp((lam * dt).reshape(S5_PAIRS, 1, LANES) * (rows + 1).astype(F32)[None, :, None])
    planes += [pw.real, pw.imag]
    coef = jnp.stack(planes, axis=1).astype(F32)
    return bw, cw, coef


def _even_kernel(u_ref, gu_ref, gv_ref, x0_ref, bw_ref, coef_ref, cw_ref, d_ref, wglu_ref, bglu_ref,
                 ng_ref, m_ref, bias_ref, *rest, ts, chunk, per_block_init, with_vn, n_prev):
    rest = rest[n_prev:]
    if with_vn:
        a_ref, b_ref, vn_ref, st_out_ref, st_ref, carry_ref = rest
    else:
        a_ref, b_ref, st_out_ref, st_ref, carry_ref = rest
        vn_ref = None
    n_rb = ts // SUBLANES

    if not per_block_init:
        @pl.when(pl.program_id(1) == 0)
        def _():
            carry_ref[...] = x0_ref[0]

    u = u_ref[...]
    ub = u.astype(BF16)
    for q in range(S5_OCT):
        bu = jnp.dot(ub[:, q * 128:(q + 1) * 128], bw_ref[q], preferred_element_type=F32)
        for c in range(4):
            st_ref[q * 4 + c] = bu[:, c * 128:(c + 1) * 128]
            st_ref[S5_PAIRS + q * 4 + c] = bu[:, 512 + c * 128:512 + (c + 1) * 128]

    def pair_body(j, _):
        cf = coef_ref[j]
        a1r, a1i, a2r, a2i, a4r, a4i, pr, pi = [cf[k] for k in range(8)]

        def rb_body(r, carry):
            cr, ci = carry
            row = pl.multiple_of(r * SUBLANES, SUBLANES)
            xr = st_ref[j, pl.ds(row, SUBLANES), :]
            xi = st_ref[S5_PAIRS + j, pl.ds(row, SUBLANES), :]
            for d, ar, ai in ((1, a1r, a1i), (2, a2r, a2i), (4, a4r, a4i)):
                sr = pltpu.roll(xr, d, 0)
                si = pltpu.roll(xi, d, 0)
                xr, xi = xr + ar * sr - ai * si, xi + ar * si + ai * sr
            if per_block_init:
                cr = x0_ref[j, r]
                ci = x0_ref[S5_PAIRS + j, r]
            xr, xi = xr + pr * cr - pi * ci, xi + pr * ci + pi * cr
            st_ref[j, pl.ds(row, SUBLANES), :] = xr
            st_ref[S5_PAIRS + j, pl.ds(row, SUBLANES), :] = xi
            ncr = jnp.broadcast_to(xr[SUBLANES - 1:SUBLANES, :], (SUBLANES, LANES))
            nci = jnp.broadcast_to(xi[SUBLANES - 1:SUBLANES, :], (SUBLANES, LANES))
            if per_block_init:
                st_out_ref[j, r] = ncr
                st_out_ref[S5_PAIRS + j, r] = nci
            return ncr, nci

        cr, ci = lax.fori_loop(0, n_rb, rb_body, (carry_ref[j], carry_ref[S5_PAIRS + j]),
                               unroll=min(4, n_rb))
        carry_ref[j] = cr
        carry_ref[S5_PAIRS + j] = ci
        return 0

    lax.fori_loop(0, S5_PAIRS, pair_body, 0)

    if not per_block_init:
        @pl.when(pl.program_id(1) == pl.num_programs(1) - 1)
        def _():
            st_out_ref[0] = carry_ref[...]

    ys = []
    for q in range(S5_OCT):
        xq = jnp.concatenate([st_ref[q * 4 + c] for c in range(4)]
                             + [st_ref[S5_PAIRS + q * 4 + c] for c in range(4)], axis=-1)
        ys.append(jnp.dot(xq.astype(BF16), cw_ref[q], preferred_element_type=F32))
    y = jnp.concatenate(ys, axis=-1) + d_ref[...] * u
    g = _gelu(y)
    z = jnp.dot(g.astype(BF16), wglu_ref[...], preferred_element_type=F32) + bglu_ref[...]
    a_ref[...] = g * _sigmoid(z)

    gu = _gelu(gu_ref[...])
    gv = _gelu(gv_ref[...])
    for gi in range(GM_GROUPS):
        sl = slice(gi * GM_CH, (gi + 1) * GM_CH)
        v = gv[:, sl]
        mu = jnp.mean(v, axis=-1, keepdims=True)
        vc = v - mu
        var = jnp.mean(vc * vc, axis=-1, keepdims=True)
        vn = vc * lax.rsqrt(var + LN_EPS) * ng_ref[:, sl]
        if with_vn:
            vn_ref[:, sl] = vn
        vnb = vn.astype(BF16)
        for c in range(ts // chunk):
            rs = slice(c * chunk, (c + 1) * chunk)
            s = jnp.dot(m_ref[gi], vnb[rs], preferred_element_type=F32) + bias_ref[:, sl]
            b_ref[rs, sl] = gu[rs, sl] * s


def _even_mixer(proj, x0, s5p, d_skip, w_glu, b_glu, norm_g, m_mix, bias, *, n_b, n_s, per_block_init,
                row0=0, prev=()):
    bw, cw, coef = s5p
    if per_block_init:
        ts, grid, chunk = n_b * n_s, (1, 1), n_b * n_s
        assert n_s == SUBLANES
        n_rb = ts // SUBLANES
        x0_spec = pl.BlockSpec((2 * S5_PAIRS, n_rb, SUBLANES, LANES), lambda b, t: (0, 0, 0, 0))
        st_shape = (2 * S5_PAIRS, n_rb, SUBLANES, LANES)
        st_spec = x0_spec
    else:
        ts = min(512, n_s)
        grid, chunk = (n_b, n_s // ts), GM_CHUNK
        x0_spec = pl.BlockSpec((1, 2 * S5_PAIRS, SUBLANES, LANES), lambda b, t: (b, 0, 0, 0))
        st_shape = (n_b, 2 * S5_PAIRS, SUBLANES, LANES)
        st_spec = x0_spec
    nt = grid[1]
    with_vn = per_block_init
    blk0 = row0 // ts
    assert row0 % ts == 0

    def rows(col):
        return pl.BlockSpec((ts, MIX_HALF), lambda b, t, col=col: (blk0 + b * nt + t, col))

    def full(a):
        return pl.BlockSpec(a.shape, lambda b, t, nd=a.ndim: (0,) * nd)

    row_out = pl.BlockSpec((ts, MIX_HALF), lambda b, t: (blk0 + b * nt + t, 0))
    t_all = proj.shape[0]
    out_shape = [jax.ShapeDtypeStruct((t_all, MIX_HALF), F32), jax.ShapeDtypeStruct((t_all, MIX_HALF), F32)]
    out_specs = [row_out, row_out]
    if with_vn:
        out_shape.append(jax.ShapeDtypeStruct((n_b * n_s, MIX_HALF), F32))
        out_specs.append(pl.BlockSpec((ts, MIX_HALF), lambda b, t: (b * nt + t, 0)))
    out_shape.append(jax.ShapeDtypeStruct(st_shape, F32))
    out_specs.append(st_spec)
    weights = (bw, coef, cw, d_skip, w_glu, b_glu, norm_g, m_mix, bias)
    n_in = 4 + len(weights)
    outs = pl.pallas_call(
        functools.partial(_even_kernel, ts=ts, chunk=chunk, per_block_init=per_block_init, with_vn=with_vn,
                          n_prev=len(prev)),
        grid=grid,
        in_specs=[rows(0), rows(1), rows(2), x0_spec] + [full(w) for w in weights]
        + [pl.BlockSpec(memory_space=pl.ANY)] * len(prev),
        out_specs=out_specs,
        out_shape=out_shape,
        input_output_aliases={n_in + k: k for k in range(len(prev))},
        scratch_shapes=[pltpu.VMEM((2 * S5_PAIRS, ts, LANES), F32),
                        pltpu.VMEM((2 * S5_PAIRS, SUBLANES, LANES), F32)],
        compiler_params=_cparams(("arbitrary", "arbitrary")),
    )(proj, proj, proj, x0, *weights, *prev)
    if with_vn:
        return outs[0], outs[1], outs[2], outs[3]
    return outs[0], outs[1], None, outs[2]


def _state_to_lanes(x0, n_b):
    re = x0[..., 0].astype(F32).reshape(n_b, S5_PAIRS, LANES)
    im = x0[..., 1].astype(F32).reshape(n_b, S5_PAIRS, LANES)
    return jnp.concatenate([re, im], axis=1)


def _lanes_to_state(st, n_b):
    re = st[:, :S5_PAIRS].reshape(n_b, S5_GROUPS, S5_STATE)
    im = st[:, S5_PAIRS:].reshape(n_b, S5_GROUPS, S5_STATE)
    return jnp.stack([re, im], axis=-1)


def _even_layer_mix(proj, x0, p, i, *, n_b, n_s, sample, row0=0, prev=()):
    s5p = _s5_params(p['s5_lambda_re'][i], p['s5_lambda_im'][i], p['s5_b_re'][i], p['s5_b_im'][i],
                     p['s5_c_re'][i], p['s5_c_im'][i], p['s5_log_dt'][i])
    st0 = _state_to_lanes(x0, n_b)
    if sample:
        chunk = n_s
        x0k = jnp.broadcast_to(st0.transpose(1, 0, 2)[:, :, None, :], (2 * S5_PAIRS, n_b, SUBLANES, LANES))
        w = jnp.tril(p['gm_w_s'][i][:, :chunk, :chunk])
        m_mix = jnp.einsum('bc,gij->gbicj', jnp.eye(n_b, dtype=F32), w).reshape(GM_GROUPS, n_b * chunk, n_b * chunk)
        bias_rows = jnp.tile(p['gm_b_s'][i][:, :chunk].T, (n_b, 1))
    else:
        x0k = jnp.broadcast_to(st0[:, :, None, :], (n_b, 2 * S5_PAIRS, SUBLANES, LANES))
        m_mix = jnp.tril(p['gm_w_s'][i][:, :GM_CHUNK, :GM_CHUNK])
        bias_rows = p['gm_b_s'][i][:, :GM_CHUNK].T
    bias = jnp.repeat(bias_rows.astype(F32), GM_CH, axis=1)
    a, b, vn, st = _even_mixer(
        proj, x0k, s5p, p['s5_d'][i].reshape(1, MIX_HALF).astype(F32), p['s5_w_glu'][i].astype(BF16),
        p['s5_b_glu'][i].reshape(1, MIX_HALF).astype(F32), p['gm_norm_g'][i].reshape(1, MIX_HALF).astype(F32),
        m_mix.astype(BF16), bias, n_b=n_b, n_s=n_s, per_block_init=sample, row0=row0, prev=prev)
    if sample:
        st = st[:, :, 0, :].transpose(1, 0, 2)
    else:
        st = st[:, :, 0, :]
    return a, b, vn, _lanes_to_state(st, n_b)


POOL_HALO = 16


def _pool_kernel(c_ref, halo_ref, w_ref, scale_ref, *rest, ts, base):
    o_ref, hist_ref = rest[-2:]
    t = pl.program_id(1)

    @pl.when(t == 0)
    def _():
        hist_ref[0:POOL_HALO, :] = halo_ref[0]

    @pl.when(t > 0)
    def _():
        hist_ref[0:POOL_HALO, :] = hist_ref[ts:ts + POOL_HALO, :]

    hist_ref[POOL_HALO:POOL_HALO + ts, :] = c_ref[...]
    pos = base + t * ts + lax.broadcasted_iota(jnp.int32, (ts, 1), 0)
    for g, win in enumerate(POOL_WINDOWS):
        sl = slice(g * POOL_CH, (g + 1) * POOL_CH)
        x = hist_ref[POOL_HALO:POOL_HALO + ts, sl]
        acc = x
        for d in range(1, win):
            acc = acc + hist_ref[POOL_HALO - d:POOL_HALO - d + ts, sl]
        cnt = jnp.minimum(pos + 1, win).astype(F32)
        pooled = acc / cnt - x
        y = jnp.dot(pooled.astype(BF16), w_ref[g], preferred_element_type=F32)
        o_ref[:, sl] = y * scale_ref[:, sl]


def _pool_mixer(proj, halo, w, scale, *, n_b, n_s, base, row0=0, prev=()):
    ts = min(512, n_s)
    nt = n_s // ts
    blk0 = row0 // ts
    assert row0 % ts == 0
    return pl.pallas_call(
        functools.partial(_pool_kernel, ts=ts, base=base),
        grid=(n_b, nt),
        in_specs=[pl.BlockSpec((ts, MIX_HALF), lambda b, t: (blk0 + b * nt + t, 0)),
                  pl.BlockSpec((1, POOL_HALO, MIX_HALF), lambda b, t: (b, 0, 0)),
                  pl.BlockSpec(w.shape, lambda b, t: (0, 0, 0)),
                  pl.BlockSpec((1, MIX_HALF), lambda b, t: (0, 0))]
        + [pl.BlockSpec(memory_space=pl.ANY)] * len(prev),
        out_specs=pl.BlockSpec((ts, MIX_HALF), lambda b, t: (blk0 + b * nt + t, 0)),
        out_shape=jax.ShapeDtypeStruct((proj.shape[0], MIX_HALF), F32),
        input_output_aliases={4 + k: k for k in range(len(prev))},
        scratch_shapes=[pltpu.VMEM((POOL_HALO + ts, MIX_HALF), F32)],
        compiler_params=_cparams(("arbitrary", "arbitrary")),
    )(proj, halo, w, scale, *prev)


def _rope_tables(pos):
    half = ROT_DIM // 2
    inv = ROPE_THETA ** (-jnp.arange(half, dtype=F32) * 2.0 / ROT_DIM)
    ang = pos.astype(F32)[:, None] * inv[None, :]
    cos, sin = jnp.cos(ang), jnp.sin(ang)
    n = pos.shape[0]
    one = jnp.ones((n, HEAD_DIM - ROT_DIM), F32)
    zero = jnp.zeros((n, HEAD_DIM - ROT_DIM), F32)
    z8 = jnp.zeros((n, half), F32)
    ca = jnp.concatenate([cos, cos, one], axis=1)
    sp = jnp.concatenate([z8, sin, zero], axis=1)
    sm = jnp.concatenate([-sin, z8, zero], axis=1)
    return tuple(jnp.tile(t, (1, LANES // HEAD_DIM)) for t in (ca, sp, sm))


def _rope_kernel(q_ref, k_ref, v_ref, ca_ref, sp_ref, sm_ref, qo_ref, ko_ref, qb_ref, kb_ref, vb_ref, km_ref):
    ca, sp, sm = ca_ref[...], sp_ref[...], sm_ref[...]
    half = ROT_DIM // 2
    for c in range(MIX_HALF // LANES):
        sl = slice(c * LANES, (c + 1) * LANES)
        for src, dst in ((q_ref, qo_ref), (k_ref, ko_ref)):
            x = src[:, sl]
            dst[:, sl] = x * ca + pltpu.roll(x, half, 1) * sp + pltpu.roll(x, LANES - half, 1) * sm
    q = qo_ref[...]
    k = ko_ref[...]
    qb_ref[...] = (q * (HEAD_DIM ** -0.5)).astype(BF16)
    kb_ref[...] = k.astype(BF16)
    vb_ref[...] = v_ref[...].astype(BF16)
    km_ref[0] = jnp.mean(k, axis=0, keepdims=True)


def _rope(proj, tables):
    t_rows = proj.shape[0]
    ts = MOBA_BLOCK

    def col(c):
        return pl.BlockSpec((ts, MIX_HALF), lambda i, c=c: (i, c))

    tab = pl.BlockSpec((ts, LANES), lambda i: (i, 0))
    row = pl.BlockSpec((ts, MIX_HALF), lambda i: (i, 0))
    f32o = jax.ShapeDtypeStruct((t_rows, MIX_HALF), F32)
    bfo = jax.ShapeDtypeStruct((t_rows, MIX_HALF), BF16)
    outs = pl.pallas_call(
        _rope_kernel,
        grid=(t_rows // ts,),
        in_specs=[col(1), col(2), col(3), tab, tab, tab],
        out_specs=[row, row, row, row, row, pl.BlockSpec((1, 1, MIX_HALF), lambda i: (i, 0, 0))],
        out_shape=[f32o, f32o, bfo, bfo, bfo, jax.ShapeDtypeStruct((t_rows // ts, 1, MIX_HALF), F32)],
        compiler_params=_cparams(("parallel",)),
    )(proj, proj, proj, *tables)
    return list(outs[:5]) + [outs[5].reshape(t_rows // ts, MIX_HALF)]


def _top_rows_mask(gate, n_valid_rows, k_top):
    n = gate.shape[0]
    row = lax.broadcasted_iota(jnp.int32, gate.shape, 0)
    live = row < n_valid_rows
    sel = jnp.zeros(gate.shape, jnp.bool_)
    for _ in range(k_top):
        g = jnp.where(live, gate, NEG_INF)
        mx = jnp.max(g, axis=0, keepdims=True)
        first = jnp.min(jnp.where(live & (g == mx), row, n), axis=0, keepdims=True)
        pick = row == first
        sel = sel | pick
        live = live & jnp.logical_not(pick)
    return sel


def _top_lanes_mask(gate, n_valid, k_top):
    n = gate.shape[1]
    lane = lax.broadcasted_iota(jnp.int32, gate.shape, 1)
    live = lane < n_valid
    sel = jnp.zeros(gate.shape, jnp.bool_)
    for _ in range(k_top):
        g = jnp.where(live, gate, NEG_INF)
        mx = jnp.max(g, axis=1, keepdims=True)
        first = jnp.min(jnp.where(live & (g == mx), lane, n), axis=1, keepdims=True)
        pick = lane == first
        sel = sel | pick
        live = live & jnp.logical_not(pick)
    return sel


HEAD_PAIRS = MIX_HALF // LANES


def _moba_prompt_kernel(q_ref, qb_ref, kb_ref, vb_ref, km_ref, _, o_ref,
                        qbd_ref, sel_ref, m_ref, l_ref, acc_ref, *, n_blk):
    qi = pl.program_id(1)
    tq = MOBA_BLOCK
    lane = lax.broadcasted_iota(jnp.int32, (tq, LANES), 1)
    krow = lax.broadcasted_iota(jnp.int32, (tq, 2 * tq), 0)
    qcol = lax.broadcasted_iota(jnp.int32, (tq, 2 * tq), 1) % tq
    causal = krow <= qcol
    nt_dims = (((1,), (1,)), ((), ()))
    tn_dims = (((0,), (0,)), ((), ()))
    row0 = pl.multiple_of(qi * tq, tq)

    def attend(pr, r0, s_mask, first):
        ps = slice(pr * LANES, (pr + 1) * LANES)
        s = lax.dot_general(kb_ref[pl.ds(r0, tq), ps], qbd_ref[pr], nt_dims, preferred_element_type=F32)
        s = jnp.where(s_mask, s, NEG_INF)
        m_blk = jnp.max(s, axis=0, keepdims=True)
        m_new = m_blk if first else jnp.maximum(m_ref[pr], m_blk)
        p = jnp.exp(s - m_new)
        pv = lax.dot_general(vb_ref[pl.ds(r0, tq), ps], p.astype(BF16), tn_dims,
                             preferred_element_type=F32)
        pv_a, pv_b = pv[:HEAD_DIM, :tq], pv[HEAD_DIM:, tq:]
        if first:
            l_ref[pr] = jnp.sum(p, axis=0, keepdims=True)
            acc_ref[pr, 0] = pv_a
            acc_ref[pr, 1] = pv_b
        else:
            alpha = jnp.exp(m_ref[pr] - m_new)
            l_ref[pr] = alpha * l_ref[pr] + jnp.sum(p, axis=0, keepdims=True)
            acc_ref[pr, 0] = alpha[:, :tq] * acc_ref[pr, 0] + pv_a
            acc_ref[pr, 1] = alpha[:, tq:] * acc_ref[pr, 1] + pv_b
        m_ref[pr] = m_new

    for pr in range(HEAD_PAIRS):
        ps = slice(pr * LANES, (pr + 1) * LANES)
        qf = q_ref[:, ps] * (HEAD_DIM ** -0.5)
        qbd_f = jnp.concatenate([jnp.where(lane < HEAD_DIM, qf, 0.0), jnp.where(lane >= HEAD_DIM, qf, 0.0)], axis=0)
        qb = qb_ref[:, ps]
        zero = jnp.zeros_like(qb)
        qbd_ref[pr] = jnp.concatenate([jnp.where(lane < HEAD_DIM, qb, zero), jnp.where(lane >= HEAD_DIM, qb, zero)],
                                      axis=0)
        gate = lax.dot_general(km_ref[:, ps], qbd_f, nt_dims, precision=lax.Precision.HIGHEST,
                               preferred_element_type=F32)
        sel_ref[pr] = _top_rows_mask(gate, qi, MOBA_TOPK).astype(F32)
        attend(pr, row0, causal, True)

    def blk_body(j, _):
        r0 = pl.multiple_of(j * tq, tq)
        for pr in range(HEAD_PAIRS):
            attend(pr, r0, sel_ref[pr, pl.ds(j, 1), :] > 0.5, False)
        return 0

    lax.fori_loop(0, qi, blk_body, 0)
    for pr in range(HEAD_PAIRS):
        l = l_ref[pr]
        out_t = jnp.concatenate([acc_ref[pr, 0] / l[:, :tq], acc_ref[pr, 1] / l[:, tq:]], axis=0)
        o_ref[:, pr * LANES:(pr + 1) * LANES] = out_t.T


def _moba_prompt(q_rot, qb, kb, vb, kmean, d_prev, *, n_b, n_s):
    n_blk = n_s // MOBA_BLOCK
    tq = MOBA_BLOCK
    qspec = pl.BlockSpec((tq, MIX_HALF), lambda b, i: (b * n_blk + i, 0))
    kvspec = pl.BlockSpec((n_s, MIX_HALF), lambda b, i: (b, 0))
    return pl.pallas_call(
        functools.partial(_moba_prompt_kernel, n_blk=n_blk),
        grid=(n_b, n_blk),
        in_specs=[qspec, qspec, kvspec, kvspec, pl.BlockSpec((n_blk, MIX_HALF), lambda b, i: (b, 0)),
                  pl.BlockSpec(memory_space=pl.ANY)],
        out_specs=qspec,
        out_shape=jax.ShapeDtypeStruct(d_prev.shape, F32),
        scratch_shapes=[pltpu.VMEM((HEAD_PAIRS, 2 * tq, LANES), BF16),
                        pltpu.VMEM((HEAD_PAIRS, n_blk, 2 * tq), F32),
                        pltpu.VMEM((HEAD_PAIRS, 1, 2 * tq), F32),
                        pltpu.VMEM((HEAD_PAIRS, 1, 2 * tq), F32),
                        pltpu.VMEM((HEAD_PAIRS, 2, HEAD_DIM, tq), F32)],
        input_output_aliases={5: 0},
        compiler_params=_cparams(("arbitrary", "arbitrary")),
    )(q_rot, qb, kb, vb, kmean, d_prev)


PAGES_PER_STEP = 8
BLOCK_PAGES = MOBA_BLOCK // PAGE_SIZE


def _moba_sample_kernel(pt_ref, *refs, n_blk, n_q):
    del pt_ref
    kp = refs[:PAGES_PER_STEP]
    vp = refs[PAGES_PER_STEP:2 * PAGES_PER_STEP]
    qbt_ref, qbtf_ref, kn_ref, vn_ref, _, o_ref, oacc_ref, m_ref, l_ref, km_ref = refs[2 * PAGES_PER_STEP:]
    c = pl.program_id(1)
    ncol = ATT_HEADS * n_q
    blocks_per_step = PAGES_PER_STEP // BLOCK_PAGES
    nt_dims = (((1,), (1,)), ((), ()))
    qbt = qbt_ref[0]
    lane_c = lax.broadcasted_iota(jnp.int32, (ncol, LANES), 1)
    lane_k = lax.broadcasted_iota(jnp.int32, (MIX_HALF, LANES), 1)

    @pl.when(c == 0)
    def _():
        m_ref[...] = jnp.zeros(m_ref.shape, F32)
        l_ref[...] = jnp.zeros(l_ref.shape, F32)
        km_ref[...] = jnp.zeros(km_ref.shape, F32)

    for blk in range(blocks_per_step):
        n = c * blocks_per_step + blk
        kt = jnp.concatenate([kp[blk * BLOCK_PAGES + j][0, 0] for j in range(BLOCK_PAGES)], axis=1)
        vt = jnp.concatenate([vp[blk * BLOCK_PAGES + j][0, 0] for j in range(BLOCK_PAGES)], axis=1)
        kmean = jnp.sum(kt, axis=1, keepdims=True) * (1.0 / MOBA_BLOCK)
        km_ref[...] = jnp.where(lane_k == n, kmean, km_ref[...])
        s = jnp.dot(qbt, kt.astype(BF16), preferred_element_type=F32)
        m = jnp.max(s, axis=1, keepdims=True)
        p = jnp.exp(s - m)
        m_ref[...] = jnp.where(lane_c == n, m, m_ref[...])
        l_ref[...] = jnp.where(lane_c == n, jnp.sum(p, axis=1, keepdims=True), l_ref[...])
        oacc_ref[n] = lax.dot_general(p.astype(BF16), vt.astype(BF16), nt_dims,
                                      preferred_element_type=F32)

    @pl.when(c == pl.num_programs(1) - 1)
    def _():
        gate = jnp.dot(qbtf_ref[0], km_ref[...], precision=lax.Precision.HIGHEST,
                       preferred_element_type=F32)
        sel = _top_lanes_mask(gate, n_blk, MOBA_TOPK)
        s_own = lax.dot_general(qbt, kn_ref[...].astype(BF16), nt_dims, preferred_element_type=F32)
        qidx = lax.broadcasted_iota(jnp.int32, (ncol, n_q), 0) % n_q
        kidx = lax.broadcasted_iota(jnp.int32, (ncol, n_q), 1)
        s_own = jnp.where(kidx <= qidx, s_own, NEG_INF)
        m_all = m_ref[...]
        m_fin = jnp.maximum(jnp.max(jnp.where(sel, m_all, NEG_INF), axis=1, keepdims=True),
                            jnp.max(s_own, axis=1, keepdims=True))
        w = jnp.where(sel, jnp.exp(m_all - m_fin), 0.0)
        p_own = jnp.exp(s_own - m_fin)
        l_fin = jnp.sum(w * l_ref[...], axis=1, keepdims=True) + jnp.sum(p_own, axis=1, keepdims=True)
        w = w / l_fin
        p_own = p_own / l_fin
        acc = jnp.dot(p_own, vn_ref[...], preferred_element_type=F32)
        for n in range(n_blk):
            acc = acc + w[:, n:n + 1] * oacc_ref[n]
        head = lax.broadcasted_iota(jnp.int32, (n_q, MIX_HALF), 1) // HEAD_DIM
        out = jnp.zeros((n_q, MIX_HALF), F32)
        for h in range(ATT_HEADS):
            out = out + jnp.where(head == h, acc[h * n_q:(h + 1) * n_q], 0.0)
        o_ref[...] = out


def _moba_sample(page_table, cache_k, cache_v, layer_i, q_rot, k_rot, proj, d_prev, *, n_b, n_q, row0):
    n_pages = page_table.shape[1]
    n_blk = n_pages // BLOCK_PAGES
    ncol = ATT_HEADS * n_q
    assert n_blk <= LANES and n_pages % PAGES_PER_STEP == 0 and row0 % n_q == 0
    blk0 = row0 // n_q
    q4 = (q_rot[row0:row0 + n_b * n_q] * (HEAD_DIM ** -0.5)).reshape(n_b, n_q, ATT_HEADS, HEAD_DIM)
    qbtf = jnp.einsum('bihd,hg->bhigd', q4, jnp.eye(ATT_HEADS, dtype=F32)).reshape(n_b, ncol, MIX_HALF)

    def page_spec(j):
        return pl.BlockSpec((1, 1, MIX_HALF, PAGE_SIZE),
                            lambda b, c, pt, j=j: (layer_i, pt[b, c * PAGES_PER_STEP + j], 0, 0))

    per_b3 = lambda shape: pl.BlockSpec(shape, lambda b, c, pt: (b, 0, 0))
    grid_spec = pltpu.PrefetchScalarGridSpec(
        num_scalar_prefetch=1,
        grid=(n_b, n_pages // PAGES_PER_STEP),
        in_specs=[page_spec(j) for j in range(PAGES_PER_STEP)] * 2
        + [per_b3((1, ncol, MIX_HALF)), per_b3((1, ncol, MIX_HALF)),
           pl.BlockSpec((n_q, MIX_HALF), lambda b, c, pt: (blk0 + b, 0)),
           pl.BlockSpec((n_q, MIX_HALF), lambda b, c, pt: (blk0 + b, 3)),
           pl.BlockSpec(memory_space=pl.ANY)],
        out_specs=pl.BlockSpec((n_q, MIX_HALF), lambda b, c, pt: (blk0 + b, 0)),
        scratch_shapes=[pltpu.VMEM((n_blk, ncol, MIX_HALF), F32),
                        pltpu.VMEM((ncol, LANES), F32), pltpu.VMEM((ncol, LANES), F32),
                        pltpu.VMEM((MIX_HALF, LANES), F32)])
    return pl.pallas_call(
        functools.partial(_moba_sample_kernel, n_blk=n_blk, n_q=n_q),
        grid_spec=grid_spec,
        out_shape=jax.ShapeDtypeStruct(d_prev.shape, F32),
        input_output_aliases={1 + 2 * PAGES_PER_STEP + 4: 0},
        compiler_params=_cparams(("arbitrary", "arbitrary")),
    )(page_table, *([cache_k] * PAGES_PER_STEP), *([cache_v] * PAGES_PER_STEP), qbtf.astype(BF16), qbtf,
      k_rot, proj, d_prev)


TOK_TILE = 256
SEG_ALIGN = SUBLANES
ASG_TILE = -(-(TOK_TILE * TOPK + N_EXPERTS * (SEG_ALIGN - 1)) // LANES) * LANES
EXP_TILE = 512


def _layer_norm_rows(h, g, b):
    mu = jnp.mean(h, axis=-1, keepdims=True)
    hc = h - mu
    var = jnp.mean(hc * hc, axis=-1, keepdims=True)
    return hc * lax.rsqrt(var + LN_EPS) * g + b


def _route_kernel(x_ref, a_ref, b_ref, wo_ref, g_ref, bt_ref, wr_ref, br_ref,
                  x1_ref, xs_ref, dg_ref, cnt_ref, *, alpha):
    h = (alpha * x_ref[...]
         + jnp.dot(a_ref[...].astype(BF16), wo_ref[0:MIX_HALF, :], preferred_element_type=F32)
         + jnp.dot(b_ref[...].astype(BF16), wo_ref[MIX_HALF:, :], preferred_element_type=F32))
    x1 = _layer_norm_rows(h, g_ref[...], bt_ref[...])
    x1_ref[...] = x1
    logits = lax.dot_general(wr_ref[...], x1, (((1,), (1,)), ((), ())), precision=lax.Precision.HIGHEST,
                             preferred_element_type=F32) + br_ref[...]
    row = lax.broadcasted_iota(jnp.int32, logits.shape, 0)
    g = logits
    picks, vals = [], []
    for _ in range(TOPK):
        mx = jnp.max(g, axis=0, keepdims=True)
        first = jnp.min(jnp.where(g == mx, row, N_EXPERTS), axis=0, keepdims=True)
        pick = row == first
        picks.append(pick)
        vals.append(mx)
        g = jnp.where(pick, NEG_INF, g)
    es = [jnp.exp(v - vals[0]) for v in vals]
    den = es[0] + es[1] + es[2] + es[3]
    onehot = [p.astype(F32) for p in picks]
    member = onehot[0] + onehot[1] + onehot[2] + onehot[3]
    t_r = lax.broadcasted_iota(jnp.int32, (TOK_TILE, TOK_TILE), 0)
    t_c = lax.broadcasted_iota(jnp.int32, (TOK_TILE, TOK_TILE), 1)
    before = (t_r < t_c).astype(BF16)
    rank = jnp.dot(member.astype(BF16), before, preferred_element_type=F32)
    cnt = jnp.sum(member, axis=1, keepdims=True)
    cnt = jnp.ceil(cnt * (1.0 / SEG_ALIGN)) * SEG_ALIGN
    e_r = lax.broadcasted_iota(jnp.int32, (N_EXPERTS, N_EXPERTS), 0)
    e_c = lax.broadcasted_iota(jnp.int32, (N_EXPERTS, N_EXPERTS), 1)
    lower = (e_c < e_r).astype(F32)
    off = jnp.dot(lower, jnp.broadcast_to(cnt, (N_EXPERTS, TOK_TILE)), precision=lax.Precision.HIGHEST,
                  preferred_element_type=F32)
    slot = off + rank
    dests = [jnp.sum(oh * slot, axis=0, keepdims=True) for oh in onehot]
    r_iota = lax.broadcasted_iota(jnp.int32, (ASG_TILE, TOK_TILE), 0)
    perm = jnp.zeros((ASG_TILE, TOK_TILE), F32)
    for d in dests:
        perm = jnp.where(r_iota == d.astype(jnp.int32), 1.0, perm)
    xs_ref[...] = jnp.dot(perm.astype(BF16), x1.astype(BF16), preferred_element_type=F32)
    dg_ref[0] = jnp.concatenate(dests + [e / den for e in es], axis=0)
    cnt_ref[0] = jnp.broadcast_to(cnt, (N_EXPERTS, LANES))


def _route(x, mix_a, mix_b, w_out_bf, ln_g, ln_b, wr_t, br, *, alpha):
    t = x.shape[0]
    nt = t // TOK_TILE
    full2 = lambda a: pl.BlockSpec(a.shape, lambda i: (0, 0))
    return pl.pallas_call(
        functools.partial(_route_kernel, alpha=alpha),
        grid=(nt,),
        in_specs=[pl.BlockSpec((TOK_TILE, D_MODEL), lambda i: (i, 0)),
                  pl.BlockSpec((TOK_TILE, MIX_HALF), lambda i: (i, 0)),
                  pl.BlockSpec((TOK_TILE, MIX_HALF), lambda i: (i, 0)),
                  full2(w_out_bf), full2(ln_g), full2(ln_b), full2(wr_t), full2(br)],
        out_specs=[pl.BlockSpec((TOK_TILE, D_MODEL), lambda i: (i, 0)),
                   pl.BlockSpec((ASG_TILE, D_MODEL), lambda i: (i, 0)),
                   pl.BlockSpec((1, 2 * TOPK, TOK_TILE), lambda i: (i, 0, 0)),
                   pl.BlockSpec((1, N_EXPERTS, LANES), lambda i: (i, 0, 0))],
        out_shape=[jax.ShapeDtypeStruct((t, D_MODEL), F32),
                   jax.ShapeDtypeStruct((nt * ASG_TILE, D_MODEL), F32),
                   jax.ShapeDtypeStruct((nt, 2 * TOPK, TOK_TILE), F32),
                   jax.ShapeDtypeStruct((nt, N_EXPERTS, LANES), F32)],
        compiler_params=_cparams(("parallel",)),
    )(x, mix_a, mix_b, w_out_bf, ln_g, ln_b, wr_t, br)


SEG_PIECES = tuple(1 << b for b in range(TOK_TILE.bit_length() - 1, SEG_ALIGN.bit_length() - 2, -1))


def _expert_kernel(te_ref, nu_ref, lo_ref, hi_ref, src_ref, dst_ref, len_ref, tot_ref,
                   xs_ref, wg_ref, bg_ref, wu_ref, bu_ref, wd_ref, bd_ref, ys_ref,
                   xbuf, ybuf, zbuf, wgb_ref, wub_ref, wdb_ref, gsem, ssem, zsem, *, n_tok_tiles):
    m = pl.program_id(0)
    n_used = nu_ref[0]

    def zero_tail(i, start):
        n = ASG_TILE - tot_ref[i]
        for size in SEG_PIECES:
            @pl.when((n & size) != 0)
            def _():
                row = pl.multiple_of(i * ASG_TILE + tot_ref[i] + (n & ~(2 * size - 1)), SEG_ALIGN)
                cp = pltpu.make_async_copy(zbuf.at[pl.ds(0, size)], ys_ref.at[pl.ds(row, size)], zsem)
                cp.start() if start else cp.wait()

    @pl.when(m == 0)
    def _():
        zbuf[...] = jnp.zeros(zbuf.shape, F32)

    @pl.when((m >= 1) & (m - 1 < n_tok_tiles))
    def _():
        zero_tail(m - 1, False)

    @pl.when(m < n_tok_tiles)
    def _():
        zero_tail(m, True)

    def for_pieces(t, fn):
        row0 = t * EXP_TILE

        def seg_body(s, _):
            start = jnp.maximum(dst_ref[s], row0)
            n = jnp.minimum(dst_ref[s] + len_ref[s], row0 + EXP_TILE) - start
            base_src = src_ref[s] + (start - dst_ref[s])
            base_dst = start - row0
            for size in SEG_PIECES:
                @pl.when((n & size) != 0)
                def _():
                    done = n & ~(2 * size - 1)
                    fn(pl.multiple_of(base_src + done, SEG_ALIGN), pl.multiple_of(base_dst + done, SEG_ALIGN), size)
            return 0

        lax.fori_loop(lo_ref[t], hi_ref[t], seg_body, 0)

    def gather(t, start):
        slot = t % 2

        def fn(row, r, size):
            cp = pltpu.make_async_copy(xs_ref.at[pl.ds(row, size)], xbuf.at[slot, pl.ds(r, size)], gsem.at[slot])
            cp.start() if start else cp.wait()
        for_pieces(t, fn)

    def scatter(t, start):
        slot = t % 2

        def fn(row, r, size):
            cp = pltpu.make_async_copy(ybuf.at[slot, pl.ds(r, size)], ys_ref.at[pl.ds(row, size)], ssem.at[slot])
            cp.start() if start else cp.wait()
        for_pieces(t, fn)

    @pl.when((m == 0) & (n_used > 0))
    def _():
        gather(0, True)

    @pl.when(m + 1 < n_used)
    def _():
        gather(m + 1, True)

    @pl.when((m >= 2) & (m - 2 < n_used))
    def _():
        scatter(m - 2, False)

    @pl.when(m < n_used)
    def _():
        gather(m, False)
        prev = te_ref[jnp.maximum(m - 1, 0)]

        @pl.when((m == 0) | (te_ref[m] != prev))
        def _():
            wgb_ref[...] = wg_ref[0, 0].astype(BF16)
            wub_ref[...] = wu_ref[0, 0].astype(BF16)
            wdb_ref[...] = wd_ref[0, 0].astype(BF16)

        slot = m % 2
        xb = xbuf[slot].astype(BF16)
        gt = jnp.minimum(jnp.dot(xb, wgb_ref[...], preferred_element_type=F32) + bg_ref[0, 0], SWIGLU_LIMIT)
        up = jnp.clip(jnp.dot(xb, wub_ref[...], preferred_element_type=F32) + bu_ref[0, 0],
                      -SWIGLU_LIMIT, SWIGLU_LIMIT)
        act = gt * _sigmoid(SWIGLU_ALPHA * gt) * (up + 1.0)
        ybuf[slot] = jnp.dot(act.astype(BF16), wdb_ref[...], preferred_element_type=F32) + bd_ref[0, 0]
        scatter(m, True)


def _experts(tile_expert, n_used, seg_lo, seg_hi, seg_src, seg_dst, seg_len, tile_total, xs, layer,
             w_gate, b_gate, w_up, b_up, w_down, b_down):
    n_tiles = tile_expert.shape[0]
    n_tok_tiles = tile_total.shape[0]
    assert n_tiles > n_tok_tiles
    wspec = pl.BlockSpec((1, 1, D_MODEL, D_MODEL), lambda m, te, *_: (layer, te[m], 0, 0))
    bspec = pl.BlockSpec((1, 1, 1, D_MODEL), lambda m, te, *_: (layer, te[m], 0, 0))
    grid_spec = pltpu.PrefetchScalarGridSpec(
        num_scalar_prefetch=8, grid=(n_tiles,),
        in_specs=[pl.BlockSpec(memory_space=pl.ANY), wspec, bspec, wspec, bspec, wspec, bspec],
        out_specs=pl.BlockSpec(memory_space=pl.ANY),
        scratch_shapes=[pltpu.VMEM((2, EXP_TILE, D_MODEL), F32), pltpu.VMEM((2, EXP_TILE, D_MODEL), F32),
                        pltpu.VMEM((TOK_TILE, D_MODEL), F32)]
        + [pltpu.VMEM((D_MODEL, D_MODEL), BF16)] * 3
        + [pltpu.SemaphoreType.DMA((2,)), pltpu.SemaphoreType.DMA((2,)), pltpu.SemaphoreType.DMA(())])
    depth = w_gate.shape[0]
    bshape = (depth, N_EXPERTS, 1, D_MODEL)
    return pl.pallas_call(
        functools.partial(_expert_kernel, n_tok_tiles=n_tok_tiles),
        grid_spec=grid_spec,
        out_shape=jax.ShapeDtypeStruct(xs.shape, F32),
        compiler_params=_cparams(("arbitrary",)),
    )(tile_expert, n_used, seg_lo, seg_hi, seg_src, seg_dst, seg_len, tile_total, xs,
      w_gate, b_gate.reshape(bshape), w_up, b_up.reshape(bshape), w_down, b_down.reshape(bshape))


def _combine_kernel(x1_ref, ys_ref, dg_ref, g_ref, bt_ref, o_ref, *, alpha):
    dg = dg_ref[0]
    r_iota = lax.broadcasted_iota(jnp.int32, (ASG_TILE, TOK_TILE), 0)
    comb = jnp.zeros((ASG_TILE, TOK_TILE), F32)
    for k in range(TOPK):
        comb = jnp.where(r_iota == dg[k:k + 1, :].astype(jnp.int32), dg[TOPK + k:TOPK + k + 1, :], comb)
    ffn = lax.dot_general(comb.astype(BF16), ys_ref[...].astype(BF16), (((0,), (0,)), ((), ())),
                          preferred_element_type=F32)
    o_ref[...] = _layer_norm_rows(alpha * x1_ref[...] + ffn, g_ref[...], bt_ref[...])


def _combine(x1, ys, dg, ln_g, ln_b, *, alpha):
    t = x1.shape[0]
    nt = t // TOK_TILE
    return pl.pallas_call(
        functools.partial(_combine_kernel, alpha=alpha),
        grid=(nt,),
        in_specs=[pl.BlockSpec((TOK_TILE, D_MODEL), lambda i: (i, 0)),
                  pl.BlockSpec((ASG_TILE, D_MODEL), lambda i: (i, 0)),
                  pl.BlockSpec((1, 2 * TOPK, TOK_TILE), lambda i: (i, 0, 0)),
                  pl.BlockSpec((1, D_MODEL), lambda i: (0, 0)),
                  pl.BlockSpec((1, D_MODEL), lambda i: (0, 0))],
        out_specs=pl.BlockSpec((TOK_TILE, D_MODEL), lambda i: (i, 0)),
        out_shape=jax.ShapeDtypeStruct((t, D_MODEL), F32),
        compiler_params=_cparams(("parallel",)),
    )(x1, ys, dg, ln_g, ln_b)


def _channel_mix(x, mix_a, mix_b, p, layer, alpha):
    t = x.shape[0]
    nt = t // TOK_TILE
    x1, xs, dg, cnt = _route(
        x, mix_a, mix_b, p['w_out'][layer].astype(BF16), p['ln_g'][layer, 0].reshape(1, D_MODEL),
        p['ln_b'][layer, 0].reshape(1, D_MODEL), p['router_w'][layer].T.astype(F32),
        p['router_b'][layer].reshape(N_EXPERTS, 1).astype(F32), alpha=alpha)
    cnt = cnt[:, :, 0].astype(jnp.int32)
    local_off = jnp.cumsum(cnt, axis=1) - cnt
    tile_rows = (jnp.arange(nt, dtype=jnp.int32) * ASG_TILE)[:, None] + local_off
    total = jnp.sum(cnt, axis=0)
    padded = (total + EXP_TILE - 1) // EXP_TILE * EXP_TILE
    pend = jnp.cumsum(padded)
    expert_rows = (pend - padded)[None, :] + jnp.cumsum(cnt, axis=0) - cnt
    n_tiles = -(-(nt * (TOK_TILE * TOPK + N_EXPERTS * (SEG_ALIGN - 1))) // EXP_TILE) + N_EXPERTS + 2
    tile_start = (jnp.arange(n_tiles, dtype=jnp.int32) * EXP_TILE)[:, None]
    count_below = lambda a, bound: jnp.sum((a[None, :] < bound).astype(jnp.int32), axis=1)
    tile_expert = jnp.minimum(count_below(pend, tile_start + 1), N_EXPERTS - 1)
    n_used = (pend[-1:] // EXP_TILE).astype(jnp.int32)
    seg_src = tile_rows.T.reshape(-1)
    seg_dst = expert_rows.T.reshape(-1).astype(jnp.int32)
    seg_len = cnt.T.reshape(-1)
    seg_lo = count_below(seg_dst + seg_len, tile_start + 1)
    seg_hi = count_below(seg_dst, tile_start + EXP_TILE)
    ys = _experts(tile_expert, n_used, seg_lo, seg_hi, seg_src, seg_dst, seg_len, jnp.sum(cnt, axis=1), xs, layer,
                  p['moe_w_gate'], p['moe_b_gate'], p['moe_w_up'], p['moe_b_up'], p['moe_w_down'], p['moe_b_down'])
    return _combine(x1, ys, dg, p['ln_g'][layer, 1].reshape(1, D_MODEL), p['ln_b'][layer, 1].reshape(1, D_MODEL),
                    alpha=alpha)


def kernel(x_prompt, x_sample, state_s5, state_pool, cache_k, cache_v, page_table, w_in_ab, s5_lambda_re, s5_lambda_im, s5_b_re, s5_b_im, s5_c_re, s5_c_im, s5_d, s5_log_dt, s5_w_glu, s5_b_glu, gm_norm_g, gm_w_s, gm_b_s, w_in_cd, pool_w, pool_scale, w_out, ln_g, ln_b, router_w, router_b, moe_w_gate, moe_b_gate, moe_w_up, moe_b_up, moe_w_down, moe_b_down):
    p = dict(w_in_ab=w_in_ab, s5_lambda_re=s5_lambda_re, s5_lambda_im=s5_lambda_im,
             s5_b_re=s5_b_re, s5_b_im=s5_b_im, s5_c_re=s5_c_re, s5_c_im=s5_c_im, s5_d=s5_d,
             s5_log_dt=s5_log_dt, s5_w_glu=s5_w_glu, s5_b_glu=s5_b_glu, gm_norm_g=gm_norm_g,
             gm_w_s=gm_w_s, gm_b_s=gm_b_s, w_in_cd=w_in_cd, pool_w=pool_w, pool_scale=pool_scale,
             w_out=w_out, ln_g=ln_g, ln_b=ln_b, router_w=router_w, router_b=router_b,
             moe_w_gate=moe_w_gate, moe_b_gate=moe_b_gate, moe_w_up=moe_w_up, moe_b_up=moe_b_up,
             moe_w_down=moe_w_down, moe_b_down=moe_b_down)
    n_bp, n_sp, _ = x_prompt.shape
    n_bs, n_ss, _ = x_sample.shape
    t_p, t_s = n_bp * n_sp, n_bs * n_ss
    depth = w_out.shape[0]
    alpha = (2 * depth) ** 0.25
    past_len = page_table.shape[1] * PAGE_SIZE
    x = jnp.concatenate([x_prompt.reshape(t_p, D_MODEL), x_sample.reshape(t_s, D_MODEL)], axis=0)
    zero_s5 = jnp.zeros((n_bp, S5_GROUPS, S5_STATE, 2), F32)
    pos = jnp.concatenate([jnp.tile(jnp.arange(n_sp), n_bp), jnp.tile(past_len + jnp.arange(n_ss), n_bs)])
    rope_tables = _rope_tables(pos)
    zeros_half = jnp.zeros((t_p + t_s, MIX_HALF), F32)
    cache_k2 = cache_k.transpose(0, 1, 3, 4, 2).reshape(cache_k.shape[0], cache_k.shape[1], MIX_HALF, PAGE_SIZE)
    cache_v2 = cache_v.transpose(0, 1, 3, 4, 2).reshape(cache_v.shape[0], cache_v.shape[1], MIX_HALF, PAGE_SIZE)
    s5_p, s5_s, gmv_s, pool_p, pool_s, k_p, v_p, k_s, v_s = [], [], [], [], [], [], [], [], []
    for layer in range(depth):
        i = layer // 2
        if layer % 2 == 0:
            proj = _proj(x, w_in_ab[i].astype(BF16), PROJ_TILE)
            a, b, _, st_p = _even_layer_mix(proj, zero_s5, p, i, n_b=n_bp, n_s=n_sp, sample=False,
                                            prev=(zeros_half, zeros_half))
            a, b, vn, st_s = _even_layer_mix(proj, state_s5[i], p, i, n_b=n_bs, n_s=n_ss, sample=True,
                                             row0=t_p, prev=(a, b))
            s5_p.append(st_p)
            s5_s.append(st_s)
            gmv_s.append(vn.reshape(n_bs, n_ss, MIX_HALF))
        else:
            proj = _proj(x, w_in_cd[i].astype(BF16), PROJ_TILE)
            pw = pool_w[i].astype(BF16)
            ps = pool_scale[i].reshape(1, MIX_HALF).astype(F32)
            a = _pool_mixer(proj, jnp.zeros((n_bp, POOL_HALO, MIX_HALF), F32), pw, ps, n_b=n_bp, n_s=n_sp, base=0,
                            prev=(zeros_half,))
            halo = jnp.concatenate([jnp.zeros((n_bs, POOL_HALO - POOL_BUF, MIX_HALF), F32),
                                    state_pool[i].astype(F32)], axis=1)
            a = _pool_mixer(proj, halo, pw, ps, n_b=n_bs, n_s=n_ss, base=POOL_BUF, row0=t_p, prev=(a,))
            q_rot, k_rot, qb, kb, vb, kmean = _rope(proj, rope_tables)
            b = _moba_prompt(q_rot, qb, kb, vb, kmean, zeros_half, n_b=n_bp, n_s=n_sp)
            b = _moba_sample(page_table, cache_k2, cache_v2, i, q_rot, k_rot, proj, b, n_b=n_bs, n_q=n_ss, row0=t_p)
            c_p = proj[:t_p, :MIX_HALF].reshape(n_bp, n_sp, MIX_HALF)
            c_s = proj[t_p:, :MIX_HALF].reshape(n_bs, n_ss, MIX_HALF)
            pool_p.append(c_p[:, -POOL_BUF:])
            pool_s.append(jnp.concatenate([state_pool[i].astype(F32), c_s], axis=1)[:, -POOL_BUF:])
            k_p.append(k_rot[:t_p].reshape(n_bp, n_sp, ATT_HEADS, HEAD_DIM))
            k_s.append(k_rot[t_p:].reshape(n_bs, n_ss, ATT_HEADS, HEAD_DIM))
            v_p.append(proj[:t_p, 3 * MIX_HALF:].reshape(n_bp, n_sp, ATT_HEADS, HEAD_DIM))
            v_s.append(proj[t_p:, 3 * MIX_HALF:].reshape(n_bs, n_ss, ATT_HEADS, HEAD_DIM))
        x = _channel_mix(x, a, b, p, layer, alpha)
    return (x[:t_p].reshape(n_bp, n_sp, D_MODEL), x[t_p:].reshape(n_bs, n_ss, D_MODEL),
            jnp.stack(s5_p), jnp.stack(s5_s), jnp.stack(gmv_s), jnp.stack(pool_p), jnp.stack(pool_s),
            jnp.stack(k_p), jnp.stack(v_p), jnp.stack(k_s), jnp.stack(v_s))
```

```python
import functools
import math

import jax
import jax.numpy as jnp
from jax import lax
from jax.experimental import pallas as pl
from jax.experimental.pallas import tpu as pltpu

F32 = jnp.float32
BF16 = jnp.bfloat16

D_MODEL = 1024
MIX_HALF = D_MODEL // 2
S5_GROUP_CH = 16
S5_GROUPS = MIX_HALF // S5_GROUP_CH
S5_STATE = 64
GM_CHUNK = 128
GM_GROUPS = 4
GM_CH = MIX_HALF // GM_GROUPS
POOL_WINDOWS = (2, 4, 8, 16)
POOL_CH = MIX_HALF // len(POOL_WINDOWS)
POOL_BUF = max(POOL_WINDOWS) - 1
ATT_HEADS = 8
HEAD_DIM = MIX_HALF // ATT_HEADS
ROT_DIM = HEAD_DIM // 4
ROPE_THETA = 500000.0
MOBA_BLOCK = 256
MOBA_TOPK = 3
N_EXPERTS = 32
TOPK = 4
SWIGLU_LIMIT = 7.0
SWIGLU_ALPHA = 1.702
LN_EPS = 1e-5
PAGE_SIZE = 128

LANES = 128
SUBLANES = 8
VMEM_LIMIT = 56 * 1024 * 1024

S5_OCT = 4
S5_PAIRS = S5_GROUPS * S5_STATE // LANES
NEG_INF = float("-inf")


def _cparams(sem):
    return pltpu.CompilerParams(dimension_semantics=sem, vmem_limit_bytes=VMEM_LIMIT)


def _gelu(x):
    return 0.5 * x * (1.0 + jnp.tanh(math.sqrt(2.0 / math.pi) * (x + 0.044715 * (x * x * x))))


def _sigmoid(x):
    return 1.0 / (1.0 + jnp.exp(-x))


PROJ_TILE = 640


def _proj_kernel(x_ref, w_ref, o_ref):
    o_ref[...] = jnp.dot(x_ref[...].astype(BF16), w_ref[...], preferred_element_type=F32)


def _proj(x, w_bf16, tm):
    t, k = x.shape
    n = w_bf16.shape[1]
    return pl.pallas_call(
        _proj_kernel,
        grid=(t // tm,),
        in_specs=[pl.BlockSpec((tm, k), lambda i: (i, 0)),
                  pl.BlockSpec((k, n), lambda i: (0, 0))],
        out_specs=pl.BlockSpec((tm, n), lambda i: (i, 0)),
        out_shape=jax.ShapeDtypeStruct((t, n), F32),
        compiler_params=_cparams(("parallel",)),
    )(x, w_bf16)


def _s5_params(lam_re, lam_im, b_re, b_im, c_re, c_im, log_dt):
    dt = jnp.exp(log_dt.astype(F32))[:, None]
    lam = lax.complex(lam_re.astype(F32), lam_im.astype(F32))
    lam_bar = jnp.exp(lam * dt)
    b_bar = ((lam_bar - 1.0) / lam)[..., None] * lax.complex(b_re.astype(F32), b_im.astype(F32))
    eye = jnp.eye(SUBLANES, dtype=F32)
    bb = b_bar.reshape(S5_OCT, 8, S5_STATE, S5_GROUP_CH)

    def bdiag_b(t):
        return jnp.einsum('qgph,gk->qghkp', t, eye).reshape(S5_OCT, 128, 512)

    bw = jnp.concatenate([bdiag_b(bb.real), bdiag_b(bb.imag)], axis=-1).astype(BF16)
    cc_re = c_re.astype(F32).reshape(S5_OCT, 8, S5_GROUP_CH, S5_STATE)
    cc_im = c_im.astype(F32).reshape(S5_OCT, 8, S5_GROUP_CH, S5_STATE)

    def bdiag_c(t):
        return jnp.einsum('qghp,gk->qgpkh', t, eye).reshape(S5_OCT, 512, 128)

    cw = jnp.concatenate([bdiag_c(cc_re), -bdiag_c(cc_im)], axis=1).astype(BF16)
    rows = jnp.arange(SUBLANES)
    planes = []
    for d in (1, 2, 4):
        pw = jnp.exp(lam * dt * float(d)).reshape(S5_PAIRS, 1, LANES)
        m = (rows >= d).astype(F32)[None, :, None]
        planes += [pw.real * m, pw.imag * m]
    pw = jnp.exp((lam * dt).reshape(S5_PAIRS, 1, LANES) * (rows + 1).astype(F32)[None, :, None])
    planes += [pw.real, pw.imag]
    coef = jnp.stack(planes, axis=1).astype(F32)
    return bw, cw, coef


def _even_kernel(u_ref, gu_ref, gv_ref, x0_ref, bw_ref, coef_ref, cw_ref, d_ref, wglu_ref, bglu_ref,
                 ng_ref, m_ref, bias_ref, *rest, ts, chunk, per_block_init, with_vn, n_prev):
    rest = rest[n_prev:]
    if with_vn:
        a_ref, b_ref, vn_ref, st_out_ref, st_ref, carry_ref = rest
    else:
        a_ref, b_ref, st_out_ref, st_ref, carry_ref = rest
        vn_ref = None
    n_rb = ts // SUBLANES

    if not per_block_init:
        @pl.when(pl.program_id(1) == 0)
        def _():
            carry_ref[...] = x0_ref[0]

    u = u_ref[...]
    ub = u.astype(BF16)
    for q in range(S5_OCT):
        bu = jnp.dot(ub[:, q * 128:(q + 1) * 128], bw_ref[q], preferred_element_type=F32)
        for c in range(4):
            st_ref[q * 4 + c] = bu[:, c * 128:(c + 1) * 128]
            st_ref[S5_PAIRS + q * 4 + c] = bu[:, 512 + c * 128:512 + (c + 1) * 128]

    def pair_body(j, _):
        cf = coef_ref[j]
        a1r, a1i, a2r, a2i, a4r, a4i, pr, pi = [cf[k] for k in range(8)]

        def rb_body(r, carry):
            cr, ci = carry
            row = pl.multiple_of(r * SUBLANES, SUBLANES)
            xr = st_ref[j, pl.ds(row, SUBLANES), :]
            xi = st_ref[S5_PAIRS + j, pl.ds(row, SUBLANES), :]
            for d, ar, ai in ((1, a1r, a1i), (2, a2r, a2i), (4, a4r, a4i)):
                sr = pltpu.roll(xr, d, 0)
                si = pltpu.roll(xi, d, 0)
                xr, xi = xr + ar * sr - ai * si, xi + ar * si + ai * sr
            if per_block_init:
                cr = x0_ref[j, r]
                ci = x0_ref[S5_PAIRS + j, r]
            xr, xi = xr + pr * cr - pi * ci, xi + pr * ci + pi * cr
            st_ref[j, pl.ds(row, SUBLANES), :] = xr
            st_ref[S5_PAIRS + j, pl.ds(row, SUBLANES), :] = xi
            ncr = jnp.broadcast_to(xr[SUBLANES - 1:SUBLANES, :], (SUBLANES, LANES))
            nci = jnp.broadcast_to(xi[SUBLANES - 1:SUBLANES, :], (SUBLANES, LANES))
            if per_block_init:
                st_out_ref[j, r] = ncr
                st_out_ref[S5_PAIRS + j, r] = nci
            return ncr, nci

        cr, ci = lax.fori_loop(0, n_rb, rb_body, (carry_ref[j], carry_ref[S5_PAIRS + j]),
                               unroll=min(4, n_rb))
        carry_ref[j] = cr
        carry_ref[S5_PAIRS + j] = ci
        return 0

    lax.fori_loop(0, S5_PAIRS, pair_body, 0)

    if not per_block_init:
        @pl.when(pl.program_id(1) == pl.num_programs(1) - 1)
        def _():
            st_out_ref[0] = carry_ref[...]

    ys = []
    for q in range(S5_OCT):
        xq = jnp.concatenate([st_ref[q * 4 + c] for c in range(4)]
                             + [st_ref[S5_PAIRS + q * 4 + c] for c in range(4)], axis=-1)
        ys.append(jnp.dot(xq.astype(BF16), cw_ref[q], preferred_element_type=F32))
    y = jnp.concatenate(ys, axis=-1) + d_ref[...] * u
    g = _gelu(y)
    z = jnp.dot(g.astype(BF16), wglu_ref[...], preferred_element_type=F32) + bglu_ref[...]
    a_ref[...] = g * _sigmoid(z)

    gu = _gelu(gu_ref[...])
    gv = _gelu(gv_ref[...])
    for gi in range(GM_GROUPS):
        sl = slice(gi * GM_CH, (gi + 1) * GM_CH)
        v = gv[:, sl]
        mu = jnp.mean(v, axis=-1, keepdims=True)
        vc = v - mu
        var = jnp.mean(vc * vc, axis=-1, keepdims=True)
        vn = vc * lax.rsqrt(var + LN_EPS) * ng_ref[:, sl]
        if with_vn:
            vn_ref[:, sl] = vn
        vnb = vn.astype(BF16)
        for c in range(ts // chunk):
            rs = slice(c * chunk, (c + 1) * chunk)
            s = jnp.dot(m_ref[gi], vnb[rs], preferred_element_type=F32) + bias_ref[:, sl]
            b_ref[rs, sl] = gu[rs, sl] * s


def _even_mixer(proj, x0, s5p, d_skip, w_glu, b_glu, norm_g, m_mix, bias, *, n_b, n_s, per_block_init,
                row0=0, prev=()):
    bw, cw, coef = s5p
    if per_block_init:
        ts, grid, chunk = n_b * n_s, (1, 1), n_b * n_s
        assert n_s == SUBLANES
        n_rb = ts // SUBLANES
        x0_spec = pl.BlockSpec((2 * S5_PAIRS, n_rb, SUBLANES, LANES), lambda b, t: (0, 0, 0, 0))
        st_shape = (2 * S5_PAIRS, n_rb, SUBLANES, LANES)
        st_spec = x0_spec
    else:
        ts = min(512, n_s)
        grid, chunk = (n_b, n_s // ts), GM_CHUNK
        x0_spec = pl.BlockSpec((1, 2 * S5_PAIRS, SUBLANES, LANES), lambda b, t: (b, 0, 0, 0))
        st_shape = (n_b, 2 * S5_PAIRS, SUBLANES, LANES)
        st_spec = x0_spec
    nt = grid[1]
    with_vn = per_block_init
    blk0 = row0 // ts
    assert row0 % ts == 0

    def rows(col):
        return pl.BlockSpec((ts, MIX_HALF), lambda b, t, col=col: (blk0 + b * nt + t, col))

    def full(a):
        return pl.BlockSpec(a.shape, lambda b, t, nd=a.ndim: (0,) * nd)

    row_out = pl.BlockSpec((ts, MIX_HALF), lambda b, t: (blk0 + b * nt + t, 0))
    t_all = proj.shape[0]
    out_shape = [jax.ShapeDtypeStruct((t_all, MIX_HALF), F32), jax.ShapeDtypeStruct((t_all, MIX_HALF), F32)]
    out_specs = [row_out, row_out]
    if with_vn:
        out_shape.append(jax.ShapeDtypeStruct((n_b * n_s, MIX_HALF), F32))
        out_specs.append(pl.BlockSpec((ts, MIX_HALF), lambda b, t: (b * nt + t, 0)))
    out_shape.append(jax.ShapeDtypeStruct(st_shape, F32))
    out_specs.append(st_spec)
    weights = (bw, coef, cw, d_skip, w_glu, b_glu, norm_g, m_mix, bias)
    n_in = 4 + len(weights)
    outs = pl.pallas_call(
        functools.partial(_even_kernel, ts=ts, chunk=chunk, per_block_init=per_block_init, with_vn=with_vn,
                          n_prev=len(prev)),
        grid=grid,
        in_specs=[rows(0), rows(1), rows(2), x0_spec] + [full(w) for w in weights]
        + [pl.BlockSpec(memory_space=pl.ANY)] * len(prev),
        out_specs=out_specs,
        out_shape=out_shape,
        input_output_aliases={n_in + k: k for k in range(len(prev))},
        scratch_shapes=[pltpu.VMEM((2 * S5_PAIRS, ts, LANES), F32),
                        pltpu.VMEM((2 * S5_PAIRS, SUBLANES, LANES), F32)],
        compiler_params=_cparams(("arbitrary", "arbitrary")),
    )(proj, proj, proj, x0, *weights, *prev)
    if with_vn:
        return outs[0], outs[1], outs[2], outs[3]
    return outs[0], outs[1], None, outs[2]


def _state_to_lanes(x0, n_b):
    re = x0[..., 0].astype(F32).reshape(n_b, S5_PAIRS, LANES)
    im = x0[..., 1].astype(F32).reshape(n_b, S5_PAIRS, LANES)
    return jnp.concatenate([re, im], axis=1)


def _lanes_to_state(st, n_b):
    re = st[:, :S5_PAIRS].reshape(n_b, S5_GROUPS, S5_STATE)
    im = st[:, S5_PAIRS:].reshape(n_b, S5_GROUPS, S5_STATE)
    return jnp.stack([re, im], axis=-1)


def _even_layer_mix(proj, x0, p, i, *, n_b, n_s, sample, row0=0, prev=()):
    s5p = _s5_params(p['s5_lambda_re'][i], p['s5_lambda_im'][i], p['s5_b_re'][i], p['s5_b_im'][i],
                     p['s5_c_re'][i], p['s5_c_im'][i], p['s5_log_dt'][i])
    st0 = _state_to_lanes(x0, n_b)
    if sample:
        chunk = n_s
        x0k = jnp.broadcast_to(st0.transpose(1, 0, 2)[:, :, None, :], (2 * S5_PAIRS, n_b, SUBLANES, LANES))
        w = jnp.tril(p['gm_w_s'][i][:, :chunk, :chunk])
        m_mix = jnp.einsum('bc,gij->gbicj', jnp.eye(n_b, dtype=F32), w).reshape(GM_GROUPS, n_b * chunk, n_b * chunk)
        bias_rows = jnp.tile(p['gm_b_s'][i][:, :chunk].T, (n_b, 1))
    else:
        x0k = jnp.broadcast_to(st0[:, :, None, :], (n_b, 2 * S5_PAIRS, SUBLANES, LANES))
        m_mix = jnp.tril(p['gm_w_s'][i][:, :GM_CHUNK, :GM_CHUNK])
        bias_rows = p['gm_b_s'][i][:, :GM_CHUNK].T
    bias = jnp.repeat(bias_rows.astype(F32), GM_CH, axis=1)
    a, b, vn, st = _even_mixer(
        proj, x0k, s5p, p['s5_d'][i].reshape(1, MIX_HALF).astype(F32), p['s5_w_glu'][i].astype(BF16),
        p['s5_b_glu'][i].reshape(1, MIX_HALF).astype(F32), p['gm_norm_g'][i].reshape(1, MIX_HALF).astype(F32),
        m_mix.astype(BF16), bias, n_b=n_b, n_s=n_s, per_block_init=sample, row0=row0, prev=prev)
    if sample:
        st = st[:, :, 0, :].transpose(1, 0, 2)
    else:
        st = st[:, :, 0, :]
    return a, b, vn, _lanes_to_state(st, n_b)


POOL_HALO = 16


def _pool_kernel(c_ref, halo_ref, w_ref, scale_ref, *rest, ts, base):
    o_ref, hist_ref = rest[-2:]
    t = pl.program_id(1)

    @pl.when(t == 0)
    def _():
        hist_ref[0:POOL_HALO, :] = halo_ref[0]

    @pl.when(t > 0)
    def _():
        hist_ref[0:POOL_HALO, :] = hist_ref[ts:ts + POOL_HALO, :]

    hist_ref[POOL_HALO:POOL_HALO + ts, :] = c_ref[...]
    pos = base + t * ts + lax.broadcasted_iota(jnp.int32, (ts, 1), 0)
    for g, win in enumerate(POOL_WINDOWS):
        sl = slice(g * POOL_CH, (g + 1) * POOL_CH)
        x = hist_ref[POOL_HALO:POOL_HALO + ts, sl]
        acc = x
        for d in range(1, win):
            acc = acc + hist_ref[POOL_HALO - d:POOL_HALO - d + ts, sl]
        cnt = jnp.minimum(pos + 1, win).astype(F32)
        pooled = acc / cnt - x
        y = jnp.dot(pooled.astype(BF16), w_ref[g], preferred_element_type=F32)
        o_ref[:, sl] = y * scale_ref[:, sl]


def _pool_mixer(proj, halo, w, scale, *, n_b, n_s, base, row0=0, prev=()):
    ts = min(512, n_s)
    nt = n_s // ts
    blk0 = row0 // ts
    assert row0 % ts == 0
    return pl.pallas_call(
        functools.partial(_pool_kernel, ts=ts, base=base),
        grid=(n_b, nt),
        in_specs=[pl.BlockSpec((ts, MIX_HALF), lambda b, t: (blk0 + b * nt + t, 0)),
                  pl.BlockSpec((1, POOL_HALO, MIX_HALF), lambda b, t: (b, 0, 0)),
                  pl.BlockSpec(w.shape, lambda b, t: (0, 0, 0)),
                  pl.BlockSpec((1, MIX_HALF), lambda b, t: (0, 0))]
        + [pl.BlockSpec(memory_space=pl.ANY)] * len(prev),
        out_specs=pl.BlockSpec((ts, MIX_HALF), lambda b, t: (blk0 + b * nt + t, 0)),
        out_shape=jax.ShapeDtypeStruct((proj.shape[0], MIX_HALF), F32),
        input_output_aliases={4 + k: k for k in range(len(prev))},
        scratch_shapes=[pltpu.VMEM((POOL_HALO + ts, MIX_HALF), F32)],
        compiler_params=_cparams(("arbitrary", "arbitrary")),
    )(proj, halo, w, scale, *prev)


def _rope_tables(pos):
    half = ROT_DIM // 2
    inv = ROPE_THETA ** (-jnp.arange(half, dtype=F32) * 2.0 / ROT_DIM)
    ang = pos.astype(F32)[:, None] * inv[None, :]
    cos, sin = jnp.cos(ang), jnp.sin(ang)
    n = pos.shape[0]
    one = jnp.ones((n, HEAD_DIM - ROT_DIM), F32)
    zero = jnp.zeros((n, HEAD_DIM - ROT_DIM), F32)
    z8 = jnp.zeros((n, half), F32)
    ca = jnp.concatenate([cos, cos, one], axis=1)
    sp = jnp.concatenate([z8, sin, zero], axis=1)
    sm = jnp.concatenate([-sin, z8, zero], axis=1)
    return tuple(jnp.tile(t, (1, LANES // HEAD_DIM)) for t in (ca, sp, sm))


def _rope_kernel(q_ref, k_ref, v_ref, ca_ref, sp_ref, sm_ref, qo_ref, ko_ref, qb_ref, kb_ref, vb_ref, km_ref):
    ca, sp, sm = ca_ref[...], sp_ref[...], sm_ref[...]
    half = ROT_DIM // 2
    for c in range(MIX_HALF // LANES):
        sl = slice(c * LANES, (c + 1) * LANES)
        for src, dst in ((q_ref, qo_ref), (k_ref, ko_ref)):
            x = src[:, sl]
            dst[:, sl] = x * ca + pltpu.roll(x, half, 1) * sp + pltpu.roll(x, LANES - half, 1) * sm
    q = qo_ref[...]
    k = ko_ref[...]
    qb_ref[...] = (q * (HEAD_DIM ** -0.5)).astype(BF16)
    kb_ref[...] = k.astype(BF16)
    vb_ref[...] = v_ref[...].astype(BF16)
    km_ref[0] = jnp.mean(k, axis=0, keepdims=True)


def _rope(proj, tables):
    t_rows = proj.shape[0]
    ts = MOBA_BLOCK

    def col(c):
        return pl.BlockSpec((ts, MIX_HALF), lambda i, c=c: (i, c))

    tab = pl.BlockSpec((ts, LANES), lambda i: (i, 0))
    row = pl.BlockSpec((ts, MIX_HALF), lambda i: (i, 0))
    f32o = jax.ShapeDtypeStruct((t_rows, MIX_HALF), F32)
    bfo = jax.ShapeDtypeStruct((t_rows, MIX_HALF), BF16)
    outs = pl.pallas_call(
        _rope_kernel,
        grid=(t_rows // ts,),
        in_specs=[col(1), col(2), col(3), tab, tab, tab],
        out_specs=[row, row, row, row, row, pl.BlockSpec((1, 1, MIX_HALF), lambda i: (i, 0, 0))],
        out_shape=[f32o, f32o, bfo, bfo, bfo, jax.ShapeDtypeStruct((t_rows // ts, 1, MIX_HALF), F32)],
        compiler_params=_cparams(("parallel",)),
    )(proj, proj, proj, *tables)
    return list(outs[:5]) + [outs[5].reshape(t_rows // ts, MIX_HALF)]


def _top_rows_mask(gate, n_valid_rows, k_top):
    n = gate.shape[0]
    row = lax.broadcasted_iota(jnp.int32, gate.shape, 0)
    live = row < n_valid_rows
    sel = jnp.zeros(gate.shape, jnp.bool_)
    for _ in range(k_top):
        g = jnp.where(live, gate, NEG_INF)
        mx = jnp.max(g, axis=0, keepdims=True)
        first = jnp.min(jnp.where(live & (g == mx), row, n), axis=0, keepdims=True)
        pick = row == first
        sel = sel | pick
        live = live & jnp.logical_not(pick)
    return sel


def _top_lanes_mask(gate, n_valid, k_top):
    n = gate.shape[1]
    lane = lax.broadcasted_iota(jnp.int32, gate.shape, 1)
    live = lane < n_valid
    sel = jnp.zeros(gate.shape, jnp.bool_)
    for _ in range(k_top):
        g = jnp.where(live, gate, NEG_INF)
        mx = jnp.max(g, axis=1, keepdims=True)
        first = jnp.min(jnp.where(live & (g == mx), lane, n), axis=1, keepdims=True)
        pick = lane == first
        sel = sel | pick
        live = live & jnp.logical_not(pick)
    return sel


HEAD_PAIRS = MIX_HALF // LANES


def _moba_prompt_kernel(q_ref, qb_ref, kb_ref, vb_ref, km_ref, _, o_ref,
                        qbd_ref, sel_ref, m_ref, l_ref, acc_ref, *, n_blk):
    qi = pl.program_id(1)
    tq = MOBA_BLOCK
    lane = lax.broadcasted_iota(jnp.int32, (tq, LANES), 1)
    krow = lax.broadcasted_iota(jnp.int32, (tq, 2 * tq), 0)
    qcol = lax.broadcasted_iota(jnp.int32, (tq, 2 * tq), 1) % tq
    causal = krow <= qcol
    nt_dims = (((1,), (1,)), ((), ()))
    tn_dims = (((0,), (0,)), ((), ()))
    row0 = pl.multiple_of(qi * tq, tq)

    def attend(pr, r0, mask_of, first):
        ps = slice(pr * LANES, (pr + 1) * LANES)
        kblk = kb_ref[pl.ds(r0, tq), ps]
        vblk = vb_ref[pl.ds(r0, tq), ps]
        for ck in range(2 * tq // LANES):
            cs = slice(ck * LANES, (ck + 1) * LANES)
            hd = ck * LANES // tq
            qs = slice(ck * LANES - hd * tq, (ck + 1) * LANES - hd * tq)
            s = lax.dot_general(kblk, qbd_ref[pr, cs, :], nt_dims, preferred_element_type=F32)
            s = jnp.where(mask_of(cs), s, NEG_INF)
            m_blk = jnp.max(s, axis=0, keepdims=True)
            m_old = m_ref[pr, :, cs]
            m_new = m_blk if first else jnp.maximum(m_old, m_blk)
            p = jnp.exp(s - m_new)
            pv = lax.dot_general(vblk, p.astype(BF16), tn_dims, preferred_element_type=F32)
            pv = pv[hd * HEAD_DIM:(hd + 1) * HEAD_DIM]
            if first:
                l_ref[pr, :, cs] = jnp.sum(p, axis=0, keepdims=True)
                acc_ref[pr, hd, :, qs] = pv
            else:
                alpha = jnp.exp(m_old - m_new)
                l_ref[pr, :, cs] = alpha * l_ref[pr, :, cs] + jnp.sum(p, axis=0, keepdims=True)
                acc_ref[pr, hd, :, qs] = alpha * acc_ref[pr, hd, :, qs] + pv
            m_ref[pr, :, cs] = m_new

    for pr in range(HEAD_PAIRS):
        ps = slice(pr * LANES, (pr + 1) * LANES)
        qf = q_ref[:, ps] * (HEAD_DIM ** -0.5)
        qbd_f = jnp.concatenate([jnp.where(lane < HEAD_DIM, qf, 0.0), jnp.where(lane >= HEAD_DIM, qf, 0.0)], axis=0)
        qb = qb_ref[:, ps]
        zero = jnp.zeros_like(qb)
        qbd_ref[pr] = jnp.concatenate([jnp.where(lane < HEAD_DIM, qb, zero), jnp.where(lane >= HEAD_DIM, qb, zero)],
                                      axis=0)
        gate = lax.dot_general(km_ref[:, ps], qbd_f, nt_dims, precision=lax.Precision.HIGHEST,
                               preferred_element_type=F32)
        sel_ref[pr] = _top_rows_mask(gate, qi, MOBA_TOPK).astype(F32)
        attend(pr, row0, lambda cs: causal[:, cs], True)

    def blk_body(j, _):
        r0 = pl.multiple_of(j * tq, tq)
        for pr in range(HEAD_PAIRS):
            picked = sel_ref[pr, pl.ds(j, 1), :] > 0.5
            attend(pr, r0, lambda cs, picked=picked: picked[:, cs], False)
        return 0

    lax.fori_loop(0, qi, blk_body, 0)
    for pr in range(HEAD_PAIRS):
        l = l_ref[pr]
        out_t = jnp.concatenate([acc_ref[pr, 0] / l[:, :tq], acc_ref[pr, 1] / l[:, tq:]], axis=0)
        o_ref[:, pr * LANES:(pr + 1) * LANES] = out_t.T


def _moba_prompt(q_rot, qb, kb, vb, kmean, d_prev, *, n_b, n_s):
    n_blk = n_s // MOBA_BLOCK
    tq = MOBA_BLOCK
    qspec = pl.BlockSpec((tq, MIX_HALF), lambda b, i: (b * n_blk + i, 0))
    kvspec = pl.BlockSpec((n_s, MIX_HALF), lambda b, i: (b, 0))
    return pl.pallas_call(
        functools.partial(_moba_prompt_kernel, n_blk=n_blk),
        grid=(n_b, n_blk),
        in_specs=[qspec, qspec, kvspec, kvspec, pl.BlockSpec((n_blk, MIX_HALF), lambda b, i: (b, 0)),
                  pl.BlockSpec(memory_space=pl.ANY)],
        out_specs=qspec,
        out_shape=jax.ShapeDtypeStruct(d_prev.shape, F32),
        scratch_shapes=[pltpu.VMEM((HEAD_PAIRS, 2 * tq, LANES), BF16),
                        pltpu.VMEM((HEAD_PAIRS, n_blk, 2 * tq), F32),
                        pltpu.VMEM((HEAD_PAIRS, 1, 2 * tq), F32),
                        pltpu.VMEM((HEAD_PAIRS, 1, 2 * tq), F32),
                        pltpu.VMEM((HEAD_PAIRS, 2, HEAD_DIM, tq), F32)],
        input_output_aliases={5: 0},
        compiler_params=_cparams(("arbitrary", "arbitrary")),
    )(q_rot, qb, kb, vb, kmean, d_prev)


PAGES_PER_STEP = 8
BLOCK_PAGES = MOBA_BLOCK // PAGE_SIZE


def _moba_sample_kernel(pt_ref, *refs, n_blk, n_q):
    del pt_ref
    kp = refs[:PAGES_PER_STEP]
    vp = refs[PAGES_PER_STEP:2 * PAGES_PER_STEP]
    qbt_ref, qbtf_ref, kn_ref, vn_ref, _, o_ref, oacc_ref, m_ref, l_ref, km_ref = refs[2 * PAGES_PER_STEP:]
    c = pl.program_id(1)
    ncol = ATT_HEADS * n_q
    blocks_per_step = PAGES_PER_STEP // BLOCK_PAGES
    nt_dims = (((1,), (1,)), ((), ()))
    qbt = qbt_ref[0]
    lane_c = lax.broadcasted_iota(jnp.int32, (ncol, LANES), 1)
    lane_k = lax.broadcasted_iota(jnp.int32, (MIX_HALF, LANES), 1)

    @pl.when(c == 0)
    def _():
        m_ref[...] = jnp.zeros(m_ref.shape, F32)
        l_ref[...] = jnp.zeros(l_ref.shape, F32)
        km_ref[...] = jnp.zeros(km_ref.shape, F32)

    for blk in range(blocks_per_step):
        n = c * blocks_per_step + blk
        kt = jnp.concatenate([kp[blk * BLOCK_PAGES + j][0, 0] for j in range(BLOCK_PAGES)], axis=1)
        vt = jnp.concatenate([vp[blk * BLOCK_PAGES + j][0, 0] for j in range(BLOCK_PAGES)], axis=1)
        kmean = jnp.sum(kt, axis=1, keepdims=True) * (1.0 / MOBA_BLOCK)
        km_ref[...] = jnp.where(lane_k == n, kmean, km_ref[...])
        s = jnp.dot(qbt, kt.astype(BF16), preferred_element_type=F32)
        m = jnp.max(s, axis=1, keepdims=True)
        p = jnp.exp(s - m)
        m_ref[...] = jnp.where(lane_c == n, m, m_ref[...])
        l_ref[...] = jnp.where(lane_c == n, jnp.sum(p, axis=1, keepdims=True), l_ref[...])
        oacc_ref[n] = lax.dot_general(p.astype(BF16), vt.astype(BF16), nt_dims,
                                      preferred_element_type=F32)

    @pl.when(c == pl.num_programs(1) - 1)
    def _():
        gate = jnp.dot(qbtf_ref[0], km_ref[...], precision=lax.Precision.HIGHEST,
                       preferred_element_type=F32)
        sel = _top_lanes_mask(gate, n_blk, MOBA_TOPK)
        s_own = lax.dot_general(qbt, kn_ref[...].astype(BF16), nt_dims, preferred_element_type=F32)
        qidx = lax.broadcasted_iota(jnp.int32, (ncol, n_q), 0) % n_q
        kidx = lax.broadcasted_iota(jnp.int32, (ncol, n_q), 1)
        s_own = jnp.where(kidx <= qidx, s_own, NEG_INF)
        m_all = m_ref[...]
        m_fin = jnp.maximum(jnp.max(jnp.where(sel, m_all, NEG_INF), axis=1, keepdims=True),
                            jnp.max(s_own, axis=1, keepdims=True))
        w = jnp.where(sel, jnp.exp(m_all - m_fin), 0.0)
        p_own = jnp.exp(s_own - m_fin)
        l_fin = jnp.sum(w * l_ref[...], axis=1, keepdims=True) + jnp.sum(p_own, axis=1, keepdims=True)
        w = w / l_fin
        p_own = p_own / l_fin
        acc = jnp.dot(p_own, vn_ref[...], preferred_element_type=F32)
        for n in range(n_blk):
            acc = acc + w[:, n:n + 1] * oacc_ref[n]
        head = lax.broadcasted_iota(jnp.int32, (n_q, MIX_HALF), 1) // HEAD_DIM
        out = jnp.zeros((n_q, MIX_HALF), F32)
        for h in range(ATT_HEADS):
            out = out + jnp.where(head == h, acc[h * n_q:(h + 1) * n_q], 0.0)
        o_ref[...] = out


def _moba_sample(page_table, cache_k, cache_v, layer_i, q_rot, k_rot, proj, d_prev, *, n_b, n_q, row0):
    n_pages = page_table.shape[1]
    n_blk = n_pages // BLOCK_PAGES
    ncol = ATT_HEADS * n_q
    assert n_blk <= LANES and n_pages % PAGES_PER_STEP == 0 and row0 % n_q == 0
    blk0 = row0 // n_q
    q4 = (q_rot[row0:row0 + n_b * n_q] * (HEAD_DIM ** -0.5)).reshape(n_b, n_q, ATT_HEADS, HEAD_DIM)
    qbtf = jnp.einsum('bihd,hg->bhigd', q4, jnp.eye(ATT_HEADS, dtype=F32)).reshape(n_b, ncol, MIX_HALF)

    def page_spec(j):
        return pl.BlockSpec((1, 1, MIX_HALF, PAGE_SIZE),
                            lambda b, c, pt, j=j: (layer_i, pt[b, c * PAGES_PER_STEP + j], 0, 0))

    per_b3 = lambda shape: pl.BlockSpec(shape, lambda b, c, pt: (b, 0, 0))
    grid_spec = pltpu.PrefetchScalarGridSpec(
        num_scalar_prefetch=1,
        grid=(n_b, n_pages // PAGES_PER_STEP),
        in_specs=[page_spec(j) for j in range(PAGES_PER_STEP)] * 2
        + [per_b3((1, ncol, MIX_HALF)), per_b3((1, ncol, MIX_HALF)),
           pl.BlockSpec((n_q, MIX_HALF), lambda b, c, pt: (blk0 + b, 0)),
           pl.BlockSpec((n_q, MIX_HALF), lambda b, c, pt: (blk0 + b, 3)),
           pl.BlockSpec(memory_space=pl.ANY)],
        out_specs=pl.BlockSpec((n_q, MIX_HALF), lambda b, c, pt: (blk0 + b, 0)),
        scratch_shapes=[pltpu.VMEM((n_blk, ncol, MIX_HALF), F32),
                        pltpu.VMEM((ncol, LANES), F32), pltpu.VMEM((ncol, LANES), F32),
                        pltpu.VMEM((MIX_HALF, LANES), F32)])
    return pl.pallas_call(
        functools.partial(_moba_sample_kernel, n_blk=n_blk, n_q=n_q),
        grid_spec=grid_spec,
        out_shape=jax.ShapeDtypeStruct(d_prev.shape, F32),
        input_output_aliases={1 + 2 * PAGES_PER_STEP + 4: 0},
        compiler_params=_cparams(("arbitrary", "arbitrary")),
    )(page_table, *([cache_k] * PAGES_PER_STEP), *([cache_v] * PAGES_PER_STEP), qbtf.astype(BF16), qbtf,
      k_rot, proj, d_prev)


TOK_TILE = 256
SEG_ALIGN = SUBLANES
ASG_TILE = -(-(TOK_TILE * TOPK + N_EXPERTS * (SEG_ALIGN - 1)) // LANES) * LANES
EXP_TILE = 512


def _layer_norm_rows(h, g, b):
    mu = jnp.mean(h, axis=-1, keepdims=True)
    hc = h - mu
    var = jnp.mean(hc * hc, axis=-1, keepdims=True)
    return hc * lax.rsqrt(var + LN_EPS) * g + b


def _route_kernel(x_ref, a_ref, b_ref, wo_ref, g_ref, bt_ref, wr_ref, br_ref,
                  x1_ref, xs_ref, dg_ref, cnt_ref, *, alpha):
    h = (alpha * x_ref[...]
         + jnp.dot(a_ref[...].astype(BF16), wo_ref[0:MIX_HALF, :], preferred_element_type=F32)
         + jnp.dot(b_ref[...].astype(BF16), wo_ref[MIX_HALF:, :], preferred_element_type=F32))
    x1 = _layer_norm_rows(h, g_ref[...], bt_ref[...])
    x1_ref[...] = x1
    logits = lax.dot_general(wr_ref[...], x1, (((1,), (1,)), ((), ())), precision=lax.Precision.HIGHEST,
                             preferred_element_type=F32) + br_ref[...]
    row = lax.broadcasted_iota(jnp.int32, logits.shape, 0)
    g = logits
    picks, vals = [], []
    for _ in range(TOPK):
        mx = jnp.max(g, axis=0, keepdims=True)
        first = jnp.min(jnp.where(g == mx, row, N_EXPERTS), axis=0, keepdims=True)
        pick = row == first
        picks.append(pick)
        vals.append(mx)
        g = jnp.where(pick, NEG_INF, g)
    es = [jnp.exp(v - vals[0]) for v in vals]
    den = es[0] + es[1] + es[2] + es[3]
    onehot = [p.astype(F32) for p in picks]
    member = onehot[0] + onehot[1] + onehot[2] + onehot[3]
    t_r = lax.broadcasted_iota(jnp.int32, (TOK_TILE, TOK_TILE), 0)
    t_c = lax.broadcasted_iota(jnp.int32, (TOK_TILE, TOK_TILE), 1)
    before = (t_r < t_c).astype(BF16)
    rank = jnp.dot(member.astype(BF16), before, preferred_element_type=F32)
    cnt = jnp.sum(member, axis=1, keepdims=True)
    cnt = jnp.ceil(cnt * (1.0 / SEG_ALIGN)) * SEG_ALIGN
    e_r = lax.broadcasted_iota(jnp.int32, (N_EXPERTS, N_EXPERTS), 0)
    e_c = lax.broadcasted_iota(jnp.int32, (N_EXPERTS, N_EXPERTS), 1)
    lower = (e_c < e_r).astype(F32)
    off = jnp.dot(lower, jnp.broadcast_to(cnt, (N_EXPERTS, TOK_TILE)), precision=lax.Precision.HIGHEST,
                  preferred_element_type=F32)
    slot = off + rank
    dests = [jnp.sum(oh * slot, axis=0, keepdims=True) for oh in onehot]
    r_iota = lax.broadcasted_iota(jnp.int32, (ASG_TILE, TOK_TILE), 0)
    perm = jnp.zeros((ASG_TILE, TOK_TILE), F32)
    for d in dests:
        perm = jnp.where(r_iota == d.astype(jnp.int32), 1.0, perm)
    xs_ref[...] = jnp.dot(perm.astype(BF16), x1.astype(BF16), preferred_element_type=F32)
    dg_ref[0] = jnp.concatenate(dests + [e / den for e in es], axis=0)
    cnt_ref[0] = jnp.broadcast_to(cnt, (N_EXPERTS, LANES))


def _route(x, mix_a, mix_b, w_out_bf, ln_g, ln_b, wr_t, br, *, alpha):
    t = x.shape[0]
    nt = t // TOK_TILE
    full2 = lambda a: pl.BlockSpec(a.shape, lambda i: (0, 0))
    return pl.pallas_call(
        functools.partial(_route_kernel, alpha=alpha),
        grid=(nt,),
        in_specs=[pl.BlockSpec((TOK_TILE, D_MODEL), lambda i: (i, 0)),
                  pl.BlockSpec((TOK_TILE, MIX_HALF), lambda i: (i, 0)),
                  pl.BlockSpec((TOK_TILE, MIX_HALF), lambda i: (i, 0)),
                  full2(w_out_bf), full2(ln_g), full2(ln_b), full2(wr_t), full2(br)],
        out_specs=[pl.BlockSpec((TOK_TILE, D_MODEL), lambda i: (i, 0)),
                   pl.BlockSpec((ASG_TILE, D_MODEL), lambda i: (i, 0)),
                   pl.BlockSpec((1, 2 * TOPK, TOK_TILE), lambda i: (i, 0, 0)),
                   pl.BlockSpec((1, N_EXPERTS, LANES), lambda i: (i, 0, 0))],
        out_shape=[jax.ShapeDtypeStruct((t, D_MODEL), F32),
                   jax.ShapeDtypeStruct((nt * ASG_TILE, D_MODEL), F32),
                   jax.ShapeDtypeStruct((nt, 2 * TOPK, TOK_TILE), F32),
                   jax.ShapeDtypeStruct((nt, N_EXPERTS, LANES), F32)],
        compiler_params=_cparams(("parallel",)),
    )(x, mix_a, mix_b, w_out_bf, ln_g, ln_b, wr_t, br)


SEG_PIECES = tuple(1 << b for b in range(TOK_TILE.bit_length() - 1, SEG_ALIGN.bit_length() - 2, -1))
TILE_PIECES = tuple(1 << b for b in range(EXP_TILE.bit_length() - 1, SEG_ALIGN.bit_length() - 2, -1))


def _expert_kernel(te_ref, nu_ref, lo_ref, hi_ref, valid_ref, src_ref, dst_ref, len_ref, tot_ref,
                   xs_ref, wg_ref, bg_ref, wu_ref, bu_ref, wd_ref, bd_ref, ys_ref,
                   xbuf, ybuf, zbuf, wgb_ref, wub_ref, wdb_ref, gsem, ssem, zsem, *, n_tok_tiles):
    m = pl.program_id(0)
    n_used = nu_ref[0]

    def zero_tail(i, start):
        n = ASG_TILE - tot_ref[i]
        for size in SEG_PIECES:
            @pl.when((n & size) != 0)
            def _():
                row = pl.multiple_of(i * ASG_TILE + tot_ref[i] + (n & ~(2 * size - 1)), SEG_ALIGN)
                cp = pltpu.make_async_copy(zbuf.at[pl.ds(0, size)], ys_ref.at[pl.ds(row, size)], zsem)
                cp.start() if start else cp.wait()

    @pl.when(m == 0)
    def _():
        zbuf[...] = jnp.zeros(zbuf.shape, F32)

    @pl.when((m >= 1) & (m - 1 < n_tok_tiles))
    def _():
        zero_tail(m - 1, False)

    @pl.when(m < n_tok_tiles)
    def _():
        zero_tail(m, True)

    def for_pieces(t, fn):
        row0 = t * EXP_TILE

        def seg_body(s, _):
            start = jnp.maximum(dst_ref[s], row0)
            n = jnp.minimum(dst_ref[s] + len_ref[s], row0 + EXP_TILE) - start
            base_src = src_ref[s] + (start - dst_ref[s])
            base_dst = start - row0
            for size in SEG_PIECES:
                @pl.when((n & size) != 0)
                def _():
                    done = n & ~(2 * size - 1)
                    fn(pl.multiple_of(base_src + done, SEG_ALIGN), pl.multiple_of(base_dst + done, SEG_ALIGN), size)
            return 0

        lax.fori_loop(lo_ref[t], hi_ref[t], seg_body, 0)

    def wait_rows(n, copy_of):
        for size in TILE_PIECES:
            @pl.when((n & size) != 0)
            def _():
                copy_of(size).wait()

    def gather(t, start):
        slot = t % 2
        if start:
            for_pieces(t, lambda row, r, size: pltpu.make_async_copy(
                xs_ref.at[pl.ds(row, size)], xbuf.at[slot, pl.ds(r, size)], gsem.at[slot]).start())
        else:
            wait_rows(valid_ref[t], lambda size: pltpu.make_async_copy(
                xs_ref.at[pl.ds(0, size)], xbuf.at[slot, pl.ds(0, size)], gsem.at[slot]))

    def scatter(t, start):
        slot = t % 2
        if start:
            for_pieces(t, lambda row, r, size: pltpu.make_async_copy(
                ybuf.at[slot, pl.ds(r, size)], ys_ref.at[pl.ds(row, size)], ssem.at[slot]).start())
        else:
            wait_rows(valid_ref[t], lambda size: pltpu.make_async_copy(
                ybuf.at[slot, pl.ds(0, size)], ys_ref.at[pl.ds(0, size)], ssem.at[slot]))

    @pl.when((m == 0) & (n_used > 0))
    def _():
        gather(0, True)

    @pl.when(m + 1 < n_used)
    def _():
        gather(m + 1, True)

    @pl.when((m >= 2) & (m - 2 < n_used))
    def _():
        scatter(m - 2, False)

    @pl.when(m < n_used)
    def _():
        gather(m, False)
        prev = te_ref[jnp.maximum(m - 1, 0)]

        @pl.when((m == 0) | (te_ref[m] != prev))
        def _():
            wgb_ref[...] = wg_ref[0, 0].astype(BF16)
            wub_ref[...] = wu_ref[0, 0].astype(BF16)
            wdb_ref[...] = wd_ref[0, 0].astype(BF16)

        slot = m % 2
        xb = xbuf[slot].astype(BF16)
        gt = jnp.minimum(jnp.dot(xb, wgb_ref[...], preferred_element_type=F32) + bg_ref[0, 0], SWIGLU_LIMIT)
        up = jnp.clip(jnp.dot(xb, wub_ref[...], preferred_element_type=F32) + bu_ref[0, 0],
                      -SWIGLU_LIMIT, SWIGLU_LIMIT)
        act = gt * _sigmoid(SWIGLU_ALPHA * gt) * (up + 1.0)
        ybuf[slot] = jnp.dot(act.astype(BF16), wdb_ref[...], preferred_element_type=F32) + bd_ref[0, 0]
        scatter(m, True)


def _experts(tile_expert, n_used, seg_lo, seg_hi, tile_valid, seg_src, seg_dst, seg_len, tile_total, xs, layer,
             w_gate, b_gate, w_up, b_up, w_down, b_down):
    n_tiles = tile_expert.shape[0]
    n_tok_tiles = tile_total.shape[0]
    assert n_tiles > n_tok_tiles
    wspec = pl.BlockSpec((1, 1, D_MODEL, D_MODEL), lambda m, te, *_: (layer, te[m], 0, 0))
    bspec = pl.BlockSpec((1, 1, 1, D_MODEL), lambda m, te, *_: (layer, te[m], 0, 0))
    grid_spec = pltpu.PrefetchScalarGridSpec(
        num_scalar_prefetch=9, grid=(n_tiles,),
        in_specs=[pl.BlockSpec(memory_space=pl.ANY), wspec, bspec, wspec, bspec, wspec, bspec],
        out_specs=pl.BlockSpec(memory_space=pl.ANY),
        scratch_shapes=[pltpu.VMEM((2, EXP_TILE, D_MODEL), F32), pltpu.VMEM((2, EXP_TILE, D_MODEL), F32),
                        pltpu.VMEM((TOK_TILE, D_MODEL), F32)]
        + [pltpu.VMEM((D_MODEL, D_MODEL), BF16)] * 3
        + [pltpu.SemaphoreType.DMA((2,)), pltpu.SemaphoreType.DMA((2,)), pltpu.SemaphoreType.DMA(())])
    depth = w_gate.shape[0]
    bshape = (depth, N_EXPERTS, 1, D_MODEL)
    return pl.pallas_call(
        functools.partial(_expert_kernel, n_tok_tiles=n_tok_tiles),
        grid_spec=grid_spec,
        out_shape=jax.ShapeDtypeStruct(xs.shape, F32),
        compiler_params=_cparams(("arbitrary",)),
    )(tile_expert, n_used, seg_lo, seg_hi, tile_valid, seg_src, seg_dst, seg_len, tile_total, xs,
      w_gate, b_gate.reshape(bshape), w_up, b_up.reshape(bshape), w_down, b_down.reshape(bshape))


def _combine_kernel(x1_ref, ys_ref, dg_ref, g_ref, bt_ref, o_ref, *, alpha):
    dg = dg_ref[0]
    r_iota = lax.broadcasted_iota(jnp.int32, (ASG_TILE, TOK_TILE), 0)
    comb = jnp.zeros((ASG_TILE, TOK_TILE), F32)
    for k in range(TOPK):
        comb = jnp.where(r_iota == dg[k:k + 1, :].astype(jnp.int32), dg[TOPK + k:TOPK + k + 1, :], comb)
    ffn = lax.dot_general(comb.astype(BF16), ys_ref[...].astype(BF16), (((0,), (0,)), ((), ())),
                          preferred_element_type=F32)
    o_ref[...] = _layer_norm_rows(alpha * x1_ref[...] + ffn, g_ref[...], bt_ref[...])


def _combine(x1, ys, dg, ln_g, ln_b, *, alpha):
    t = x1.shape[0]
    nt = t // TOK_TILE
    return pl.pallas_call(
        functools.partial(_combine_kernel, alpha=alpha),
        grid=(nt,),
        in_specs=[pl.BlockSpec((TOK_TILE, D_MODEL), lambda i: (i, 0)),
                  pl.BlockSpec((ASG_TILE, D_MODEL), lambda i: (i, 0)),
                  pl.BlockSpec((1, 2 * TOPK, TOK_TILE), lambda i: (i, 0, 0)),
                  pl.BlockSpec((1, D_MODEL), lambda i: (0, 0)),
                  pl.BlockSpec((1, D_MODEL), lambda i: (0, 0))],
        out_specs=pl.BlockSpec((TOK_TILE, D_MODEL), lambda i: (i, 0)),
        out_shape=jax.ShapeDtypeStruct((t, D_MODEL), F32),
        compiler_params=_cparams(("parallel",)),
    )(x1, ys, dg, ln_g, ln_b)


def _channel_mix(x, mix_a, mix_b, p, layer, alpha):
    t = x.shape[0]
    nt = t // TOK_TILE
    x1, xs, dg, cnt = _route(
        x, mix_a, mix_b, p['w_out'][layer].astype(BF16), p['ln_g'][layer, 0].reshape(1, D_MODEL),
        p['ln_b'][layer, 0].reshape(1, D_MODEL), p['router_w'][layer].T.astype(F32),
        p['router_b'][layer].reshape(N_EXPERTS, 1).astype(F32), alpha=alpha)
    cnt = cnt[:, :, 0].astype(jnp.int32)
    local_off = jnp.cumsum(cnt, axis=1) - cnt
    tile_rows = (jnp.arange(nt, dtype=jnp.int32) * ASG_TILE)[:, None] + local_off
    total = jnp.sum(cnt, axis=0)
    padded = (total + EXP_TILE - 1) // EXP_TILE * EXP_TILE
    pend = jnp.cumsum(padded)
    expert_rows = (pend - padded)[None, :] + jnp.cumsum(cnt, axis=0) - cnt
    n_tiles = -(-(nt * (TOK_TILE * TOPK + N_EXPERTS * (SEG_ALIGN - 1))) // EXP_TILE) + N_EXPERTS + 2
    tile_start = (jnp.arange(n_tiles, dtype=jnp.int32) * EXP_TILE)[:, None]
    count_below = lambda a, bound: jnp.sum((a[None, :] < bound).astype(jnp.int32), axis=1)
    tile_expert = jnp.minimum(count_below(pend, tile_start + 1), N_EXPERTS - 1)
    n_used = (pend[-1:] // EXP_TILE).astype(jnp.int32)
    seg_src = tile_rows.T.reshape(-1)
    seg_dst = expert_rows.T.reshape(-1).astype(jnp.int32)
    seg_len = cnt.T.reshape(-1)
    seg_lo = count_below(seg_dst + seg_len, tile_start + 1)
    seg_hi = count_below(seg_dst, tile_start + EXP_TILE)
    expert_end = (pend - padded + total)[tile_expert]
    tile_valid = jnp.clip(expert_end - tile_start[:, 0], 0, EXP_TILE).astype(jnp.int32)
    ys = _experts(tile_expert, n_used, seg_lo, seg_hi, tile_valid, seg_src, seg_dst, seg_len, jnp.sum(cnt, axis=1),
                  xs, layer,
                  p['moe_w_gate'], p['moe_b_gate'], p['moe_w_up'], p['moe_b_up'], p['moe_w_down'], p['moe_b_down'])
    return _combine(x1, ys, dg, p['ln_g'][layer, 1].reshape(1, D_MODEL), p['ln_b'][layer, 1].reshape(1, D_MODEL),
                    alpha=alpha)


def kernel(x_prompt, x_sample, state_s5, state_pool, cache_k, cache_v, page_table, w_in_ab, s5_lambda_re, s5_lambda_im, s5_b_re, s5_b_im, s5_c_re, s5_c_im, s5_d, s5_log_dt, s5_w_glu, s5_b_glu, gm_norm_g, gm_w_s, gm_b_s, w_in_cd, pool_w, pool_scale, w_out, ln_g, ln_b, router_w, router_b, moe_w_gate, moe_b_gate, moe_w_up, moe_b_up, moe_w_down, moe_b_down):
    p = dict(w_in_ab=w_in_ab, s5_lambda_re=s5_lambda_re, s5_lambda_im=s5_lambda_im,
             s5_b_re=s5_b_re, s5_b_im=s5_b_im, s5_c_re=s5_c_re, s5_c_im=s5_c_im, s5_d=s5_d,
             s5_log_dt=s5_log_dt, s5_w_glu=s5_w_glu, s5_b_glu=s5_b_glu, gm_norm_g=gm_norm_g,
             gm_w_s=gm_w_s, gm_b_s=gm_b_s, w_in_cd=w_in_cd, pool_w=pool_w, pool_scale=pool_scale,
             w_out=w_out, ln_g=ln_g, ln_b=ln_b, router_w=router_w, router_b=router_b,
             moe_w_gate=moe_w_gate, moe_b_gate=moe_b_gate, moe_w_up=moe_w_up, moe_b_up=moe_b_up,
             moe_w_down=moe_w_down, moe_b_down=moe_b_down)
    n_bp, n_sp, _ = x_prompt.shape
    n_bs, n_ss, _ = x_sample.shape
    t_p, t_s = n_bp * n_sp, n_bs * n_ss
    depth = w_out.shape[0]
    alpha = (2 * depth) ** 0.25
    past_len = page_table.shape[1] * PAGE_SIZE
    x = jnp.concatenate([x_prompt.reshape(t_p, D_MODEL), x_sample.reshape(t_s, D_MODEL)], axis=0)
    zero_s5 = jnp.zeros((n_bp, S5_GROUPS, S5_STATE, 2), F32)
    pos = jnp.concatenate([jnp.tile(jnp.arange(n_sp), n_bp), jnp.tile(past_len + jnp.arange(n_ss), n_bs)])
    rope_tables = _rope_tables(pos)
    zeros_half = jnp.zeros((t_p + t_s, MIX_HALF), F32)
    cache_k2 = cache_k.transpose(0, 1, 3, 4, 2).reshape(cache_k.shape[0], cache_k.shape[1], MIX_HALF, PAGE_SIZE)
    cache_v2 = cache_v.transpose(0, 1, 3, 4, 2).reshape(cache_v.shape[0], cache_v.shape[1], MIX_HALF, PAGE_SIZE)
    s5_p, s5_s, gmv_s, pool_p, pool_s, k_p, v_p, k_s, v_s = [], [], [], [], [], [], [], [], []
    for layer in range(depth):
        i = layer // 2
        if layer % 2 == 0:
            proj = _proj(x, w_in_ab[i].astype(BF16), PROJ_TILE)
            a, b, _, st_p = _even_layer_mix(proj, zero_s5, p, i, n_b=n_bp, n_s=n_sp, sample=False,
                                            prev=(zeros_half, zeros_half))
            a, b, vn, st_s = _even_layer_mix(proj, state_s5[i], p, i, n_b=n_bs, n_s=n_ss, sample=True,
                                             row0=t_p, prev=(a, b))
            s5_p.append(st_p)
            s5_s.append(st_s)
            gmv_s.append(vn.reshape(n_bs, n_ss, MIX_HALF))
        else:
            proj = _proj(x, w_in_cd[i].astype(BF16), PROJ_TILE)
            pw = pool_w[i].astype(BF16)
            ps = pool_scale[i].reshape(1, MIX_HALF).astype(F32)
            a = _pool_mixer(proj, jnp.zeros((n_bp, POOL_HALO, MIX_HALF), F32), pw, ps, n_b=n_bp, n_s=n_sp, base=0,
                            prev=(zeros_half,))
            halo = jnp.concatenate([jnp.zeros((n_bs, POOL_HALO - POOL_BUF, MIX_HALF), F32),
                                    state_pool[i].astype(F32)], axis=1)
            a = _pool_mixer(proj, halo, pw, ps, n_b=n_bs, n_s=n_ss, base=POOL_BUF, row0=t_p, prev=(a,))
            q_rot, k_rot, qb, kb, vb, kmean = _rope(proj, rope_tables)
            b = _moba_prompt(q_rot, qb, kb, vb, kmean, zeros_half, n_b=n_bp, n_s=n_sp)
            b = _moba_sample(page_table, cache_k2, cache_v2, i, q_rot, k_rot, proj, b, n_b=n_bs, n_q=n_ss, row0=t_p)
            c_p = proj[:t_p, :MIX_HALF].reshape(n_bp, n_sp, MIX_HALF)
            c_s = proj[t_p:, :MIX_HALF].reshape(n_bs, n_ss, MIX_HALF)
            pool_p.append(c_p[:, -POOL_BUF:])
            pool_s.append(jnp.concatenate([state_pool[i].astype(F32), c_s], axis=1)[:, -POOL_BUF:])
            k_p.append(k_rot[:t_p].reshape(n_bp, n_sp, ATT_HEADS, HEAD_DIM))
            k_s.append(k_rot[t_p:].reshape(n_bs, n_ss, ATT_HEADS, HEAD_DIM))
            v_p.append(proj[:t_p, 3 * MIX_HALF:].reshape(n_bp, n_sp, ATT_HEADS, HEAD_DIM))
            v_s.append(proj[t_p:, 3 * MIX_HALF:].reshape(n_bs, n_ss, ATT_HEADS, HEAD_DIM))
        x = _channel_mix(x, a, b, p, layer, alpha)
    return (x[:t_p].reshape(n_bp, n_sp, D_MODEL), x[t_p:].reshape(n_bs, n_ss, D_MODEL),
            jnp.stack(s5_p), jnp.stack(s5_s), jnp.stack(gmv_s), jnp.stack(pool_p), jnp.stack(pool_s),
            jnp.stack(k_p), jnp.stack(v_p), jnp.stack(k_s), jnp.stack(v_s))
```

```python
import functools
import math

import jax
import jax.numpy as jnp
from jax import lax
from jax.experimental import pallas as pl
from jax.experimental.pallas import tpu as pltpu

F32 = jnp.float32
BF16 = jnp.bfloat16

D_MODEL = 1024
MIX_HALF = D_MODEL // 2
S5_GROUP_CH = 16
S5_GROUPS = MIX_HALF // S5_GROUP_CH
S5_STATE = 64
GM_CHUNK = 128
GM_GROUPS = 4
GM_CH = MIX_HALF // GM_GROUPS
POOL_WINDOWS = (2, 4, 8, 16)
POOL_CH = MIX_HALF // len(POOL_WINDOWS)
POOL_BUF = max(POOL_WINDOWS) - 1
ATT_HEADS = 8
HEAD_DIM = MIX_HALF // ATT_HEADS
ROT_DIM = HEAD_DIM // 4
ROPE_THETA = 500000.0
MOBA_BLOCK = 256
MOBA_TOPK = 3
N_EXPERTS = 32
TOPK = 4
SWIGLU_LIMIT = 7.0
SWIGLU_ALPHA = 1.702
LN_EPS = 1e-5
PAGE_SIZE = 128

LANES = 128
SUBLANES = 8
VMEM_LIMIT = 56 * 1024 * 1024

S5_OCT = 4
S5_PAIRS = S5_GROUPS * S5_STATE // LANES
NEG_INF = float("-inf")


def _cparams(sem):
    return pltpu.CompilerParams(dimension_semantics=sem, vmem_limit_bytes=VMEM_LIMIT)


def _gelu(x):
    return 0.5 * x * (1.0 + jnp.tanh(math.sqrt(2.0 / math.pi) * (x + 0.044715 * (x * x * x))))


def _sigmoid(x):
    return 1.0 / (1.0 + jnp.exp(-x))


PROJ_TILE = 640


def _proj_kernel(x_ref, w_ref, o_ref):
    o_ref[...] = jnp.dot(x_ref[...].astype(BF16), w_ref[...], preferred_element_type=F32)


def _proj(x, w_bf16, tm):
    t, k = x.shape
    n = w_bf16.shape[1]
    return pl.pallas_call(
        _proj_kernel,
        grid=(t // tm,),
        in_specs=[pl.BlockSpec((tm, k), lambda i: (i, 0)),
                  pl.BlockSpec((k, n), lambda i: (0, 0))],
        out_specs=pl.BlockSpec((tm, n), lambda i: (i, 0)),
        out_shape=jax.ShapeDtypeStruct((t, n), F32),
        compiler_params=_cparams(("parallel",)),
    )(x, w_bf16)


def _s5_params(lam_re, lam_im, b_re, b_im, c_re, c_im, log_dt):
    dt = jnp.exp(log_dt.astype(F32))[:, None]
    lam = lax.complex(lam_re.astype(F32), lam_im.astype(F32))
    lam_bar = jnp.exp(lam * dt)
    b_bar = ((lam_bar - 1.0) / lam)[..., None] * lax.complex(b_re.astype(F32), b_im.astype(F32))
    eye = jnp.eye(SUBLANES, dtype=F32)
    bb = b_bar.reshape(S5_OCT, 8, S5_STATE, S5_GROUP_CH)

    def bdiag_b(t):
        return jnp.einsum('qgph,gk->qghkp', t, eye).reshape(S5_OCT, 128, 512)

    bw = jnp.concatenate([bdiag_b(bb.real), bdiag_b(bb.imag)], axis=-1).astype(BF16)
    cc_re = c_re.astype(F32).reshape(S5_OCT, 8, S5_GROUP_CH, S5_STATE)
    cc_im = c_im.astype(F32).reshape(S5_OCT, 8, S5_GROUP_CH, S5_STATE)

    def bdiag_c(t):
        return jnp.einsum('qghp,gk->qgpkh', t, eye).reshape(S5_OCT, 512, 128)

    cw = jnp.concatenate([bdiag_c(cc_re), -bdiag_c(cc_im)], axis=1).astype(BF16)
    rows = jnp.arange(SUBLANES)
    planes = []
    for d in (1, 2, 4):
        pw = jnp.exp(lam * dt * float(d)).reshape(S5_PAIRS, 1, LANES)
        m = (rows >= d).astype(F32)[None, :, None]
        planes += [pw.real * m, pw.imag * m]
    pw = jnp.exp((lam * dt).reshape(S5_PAIRS, 1, LANES) * (rows + 1).astype(F32)[None, :, None])
    planes += [pw.real, pw.imag]
    coef = jnp.stack(planes, axis=1).astype(F32)
    return bw, cw, coef


def _even_kernel(u_ref, gu_ref, gv_ref, x0_ref, bw_ref, coef_ref, cw_ref, d_ref, wglu_ref, bglu_ref,
                 ng_ref, m_ref, bias_ref, *rest, ts, chunk, per_block_init, with_vn):
    if with_vn:
        a_ref, b_ref, vn_ref, st_out_ref, st_ref, carry_ref = rest
    else:
        a_ref, b_ref, st_out_ref, st_ref, carry_ref = rest
        vn_ref = None
    n_rb = ts // SUBLANES

    if not per_block_init:
        @pl.when(pl.program_id(1) == 0)
        def _():
            carry_ref[...] = x0_ref[0]

    u = u_ref[...]
    ub = u.astype(BF16)
    for q in range(S5_OCT):
        bu = jnp.dot(ub[:, q * 128:(q + 1) * 128], bw_ref[q], preferred_element_type=F32)
        for c in range(4):
            st_ref[q * 4 + c] = bu[:, c * 128:(c + 1) * 128]
            st_ref[S5_PAIRS + q * 4 + c] = bu[:, 512 + c * 128:512 + (c + 1) * 128]

    def pair_body(j, _):
        cf = coef_ref[j]
        a1r, a1i, a2r, a2i, a4r, a4i, pr, pi = [cf[k] for k in range(8)]

        def rb_body(r, carry):
            cr, ci = carry
            row = pl.multiple_of(r * SUBLANES, SUBLANES)
            xr = st_ref[j, pl.ds(row, SUBLANES), :]
            xi = st_ref[S5_PAIRS + j, pl.ds(row, SUBLANES), :]
            for d, ar, ai in ((1, a1r, a1i), (2, a2r, a2i), (4, a4r, a4i)):
                sr = pltpu.roll(xr, d, 0)
                si = pltpu.roll(xi, d, 0)
                xr, xi = xr + ar * sr - ai * si, xi + ar * si + ai * sr
            if per_block_init:
                cr = x0_ref[j, r]
                ci = x0_ref[S5_PAIRS + j, r]
            xr, xi = xr + pr * cr - pi * ci, xi + pr * ci + pi * cr
            st_ref[j, pl.ds(row, SUBLANES), :] = xr
            st_ref[S5_PAIRS + j, pl.ds(row, SUBLANES), :] = xi
            ncr = jnp.broadcast_to(xr[SUBLANES - 1:SUBLANES, :], (SUBLANES, LANES))
            nci = jnp.broadcast_to(xi[SUBLANES - 1:SUBLANES, :], (SUBLANES, LANES))
            if per_block_init:
                st_out_ref[j, r] = ncr
                st_out_ref[S5_PAIRS + j, r] = nci
            return ncr, nci

        cr, ci = lax.fori_loop(0, n_rb, rb_body, (carry_ref[j], carry_ref[S5_PAIRS + j]),
                               unroll=min(4, n_rb))
        carry_ref[j] = cr
        carry_ref[S5_PAIRS + j] = ci
        return 0

    lax.fori_loop(0, S5_PAIRS, pair_body, 0)

    if not per_block_init:
        @pl.when(pl.program_id(1) == pl.num_programs(1) - 1)
        def _():
            st_out_ref[0] = carry_ref[...]

    ys = []
    for q in range(S5_OCT):
        xq = jnp.concatenate([st_ref[q * 4 + c] for c in range(4)]
                             + [st_ref[S5_PAIRS + q * 4 + c] for c in range(4)], axis=-1)
        ys.append(jnp.dot(xq.astype(BF16), cw_ref[q], preferred_element_type=F32))
    y = jnp.concatenate(ys, axis=-1) + d_ref[...] * u
    g = _gelu(y)
    z = jnp.dot(g.astype(BF16), wglu_ref[...], preferred_element_type=F32) + bglu_ref[...]
    a_ref[...] = g * _sigmoid(z)

    gu = _gelu(gu_ref[...])
    gv = _gelu(gv_ref[...])
    for gi in range(GM_GROUPS):
        sl = slice(gi * GM_CH, (gi + 1) * GM_CH)
        v = gv[:, sl]
        mu = jnp.mean(v, axis=-1, keepdims=True)
        vc = v - mu
        var = jnp.mean(vc * vc, axis=-1, keepdims=True)
        vn = vc * lax.rsqrt(var + LN_EPS) * ng_ref[:, sl]
        if with_vn:
            vn_ref[:, sl] = vn
        vnb = vn.astype(BF16)
        for c in range(ts // chunk):
            rs = slice(c * chunk, (c + 1) * chunk)
            s = jnp.dot(m_ref[gi], vnb[rs], preferred_element_type=F32) + bias_ref[:, sl]
            b_ref[rs, sl] = gu[rs, sl] * s


def _even_mixer(proj, x0, s5p, d_skip, w_glu, b_glu, norm_g, m_mix, bias, *, n_b, n_s, per_block_init, row0=0):
    bw, cw, coef = s5p
    if per_block_init:
        ts, grid, chunk = n_b * n_s, (1, 1), n_b * n_s
        assert n_s == SUBLANES
        n_rb = ts // SUBLANES
        x0_spec = pl.BlockSpec((2 * S5_PAIRS, n_rb, SUBLANES, LANES), lambda b, t: (0, 0, 0, 0))
        st_shape = (2 * S5_PAIRS, n_rb, SUBLANES, LANES)
        st_spec = x0_spec
    else:
        ts = min(512, n_s)
        grid, chunk = (n_b, n_s // ts), GM_CHUNK
        x0_spec = pl.BlockSpec((1, 2 * S5_PAIRS, SUBLANES, LANES), lambda b, t: (b, 0, 0, 0))
        st_shape = (n_b, 2 * S5_PAIRS, SUBLANES, LANES)
        st_spec = x0_spec
    nt = grid[1]
    with_vn = per_block_init
    blk0 = row0 // ts
    assert row0 % ts == 0

    def rows(col):
        return pl.BlockSpec((ts, MIX_HALF), lambda b, t, col=col: (blk0 + b * nt + t, col))

    def full(a):
        return pl.BlockSpec(a.shape, lambda b, t, nd=a.ndim: (0,) * nd)

    row_out = pl.BlockSpec((ts, MIX_HALF), lambda b, t: (b * nt + t, 0))
    rows_shape = jax.ShapeDtypeStruct((n_b * n_s, MIX_HALF), F32)
    n_row_outs = 3 if with_vn else 2
    out_shape = [rows_shape] * n_row_outs + [jax.ShapeDtypeStruct(st_shape, F32)]
    out_specs = [row_out] * n_row_outs + [st_spec]
    weights = (bw, coef, cw, d_skip, w_glu, b_glu, norm_g, m_mix, bias)
    outs = pl.pallas_call(
        functools.partial(_even_kernel, ts=ts, chunk=chunk, per_block_init=per_block_init, with_vn=with_vn),
        grid=grid,
        in_specs=[rows(0), rows(1), rows(2), x0_spec] + [full(w) for w in weights],
        out_specs=out_specs,
        out_shape=out_shape,
        scratch_shapes=[pltpu.VMEM((2 * S5_PAIRS, ts, LANES), F32),
                        pltpu.VMEM((2 * S5_PAIRS, SUBLANES, LANES), F32)],
        compiler_params=_cparams(("arbitrary", "arbitrary")),
    )(proj, proj, proj, x0, *weights)
    if with_vn:
        return outs[0], outs[1], outs[2], outs[3]
    return outs[0], outs[1], None, outs[2]


def _state_to_lanes(x0, n_b):
    re = x0[..., 0].astype(F32).reshape(n_b, S5_PAIRS, LANES)
    im = x0[..., 1].astype(F32).reshape(n_b, S5_PAIRS, LANES)
    return jnp.concatenate([re, im], axis=1)


def _lanes_to_state(st, n_b):
    re = st[:, :S5_PAIRS].reshape(n_b, S5_GROUPS, S5_STATE)
    im = st[:, S5_PAIRS:].reshape(n_b, S5_GROUPS, S5_STATE)
    return jnp.stack([re, im], axis=-1)


def _even_layer_mix(proj, x0, p, i, *, n_b, n_s, sample, row0=0):
    s5p = _s5_params(p['s5_lambda_re'][i], p['s5_lambda_im'][i], p['s5_b_re'][i], p['s5_b_im'][i],
                     p['s5_c_re'][i], p['s5_c_im'][i], p['s5_log_dt'][i])
    st0 = _state_to_lanes(x0, n_b)
    if sample:
        chunk = n_s
        x0k = jnp.broadcast_to(st0.transpose(1, 0, 2)[:, :, None, :], (2 * S5_PAIRS, n_b, SUBLANES, LANES))
        w = jnp.tril(p['gm_w_s'][i][:, :chunk, :chunk])
        m_mix = jnp.einsum('bc,gij->gbicj', jnp.eye(n_b, dtype=F32), w).reshape(GM_GROUPS, n_b * chunk, n_b * chunk)
        bias_rows = jnp.tile(p['gm_b_s'][i][:, :chunk].T, (n_b, 1))
    else:
        x0k = jnp.broadcast_to(st0[:, :, None, :], (n_b, 2 * S5_PAIRS, SUBLANES, LANES))
        m_mix = jnp.tril(p['gm_w_s'][i][:, :GM_CHUNK, :GM_CHUNK])
        bias_rows = p['gm_b_s'][i][:, :GM_CHUNK].T
    bias = jnp.repeat(bias_rows.astype(F32), GM_CH, axis=1)
    a, b, vn, st = _even_mixer(
        proj, x0k, s5p, p['s5_d'][i].reshape(1, MIX_HALF).astype(F32), p['s5_w_glu'][i].astype(BF16),
        p['s5_b_glu'][i].reshape(1, MIX_HALF).astype(F32), p['gm_norm_g'][i].reshape(1, MIX_HALF).astype(F32),
        m_mix.astype(BF16), bias, n_b=n_b, n_s=n_s, per_block_init=sample, row0=row0)
    if sample:
        st = st[:, :, 0, :].transpose(1, 0, 2)
    else:
        st = st[:, :, 0, :]
    return a, b, vn, _lanes_to_state(st, n_b)


POOL_HALO = 16


def _pool_kernel(c_ref, halo_ref, w_ref, scale_ref, o_ref, hist_ref, *, ts, base):
    t = pl.program_id(1)

    @pl.when(t == 0)
    def _():
        hist_ref[0:POOL_HALO, :] = halo_ref[0]

    @pl.when(t > 0)
    def _():
        hist_ref[0:POOL_HALO, :] = hist_ref[ts:ts + POOL_HALO, :]

    hist_ref[POOL_HALO:POOL_HALO + ts, :] = c_ref[...]
    pos = base + t * ts + lax.broadcasted_iota(jnp.int32, (ts, 1), 0)
    for g, win in enumerate(POOL_WINDOWS):
        sl = slice(g * POOL_CH, (g + 1) * POOL_CH)
        x = hist_ref[POOL_HALO:POOL_HALO + ts, sl]
        acc = x
        for d in range(1, win):
            acc = acc + hist_ref[POOL_HALO - d:POOL_HALO - d + ts, sl]
        cnt = jnp.minimum(pos + 1, win).astype(F32)
        pooled = acc / cnt - x
        y = jnp.dot(pooled.astype(BF16), w_ref[g], preferred_element_type=F32)
        o_ref[:, sl] = y * scale_ref[:, sl]


def _pool_mixer(proj, halo, w, scale, *, n_b, n_s, base, row0=0):
    ts = min(512, n_s)
    nt = n_s // ts
    blk0 = row0 // ts
    assert row0 % ts == 0
    return pl.pallas_call(
        functools.partial(_pool_kernel, ts=ts, base=base),
        grid=(n_b, nt),
        in_specs=[pl.BlockSpec((ts, MIX_HALF), lambda b, t: (blk0 + b * nt + t, 0)),
                  pl.BlockSpec((1, POOL_HALO, MIX_HALF), lambda b, t: (b, 0, 0)),
                  pl.BlockSpec(w.shape, lambda b, t: (0, 0, 0)),
                  pl.BlockSpec((1, MIX_HALF), lambda b, t: (0, 0))],
        out_specs=pl.BlockSpec((ts, MIX_HALF), lambda b, t: (b * nt + t, 0)),
        out_shape=jax.ShapeDtypeStruct((n_b * n_s, MIX_HALF), F32),
        scratch_shapes=[pltpu.VMEM((POOL_HALO + ts, MIX_HALF), F32)],
        compiler_params=_cparams(("arbitrary", "arbitrary")),
    )(proj, halo, w, scale)


def _rope_tables(pos):
    half = ROT_DIM // 2
    inv = ROPE_THETA ** (-jnp.arange(half, dtype=F32) * 2.0 / ROT_DIM)
    ang = pos.astype(F32)[:, None] * inv[None, :]
    cos, sin = jnp.cos(ang), jnp.sin(ang)
    n = pos.shape[0]
    one = jnp.ones((n, HEAD_DIM - ROT_DIM), F32)
    zero = jnp.zeros((n, HEAD_DIM - ROT_DIM), F32)
    z8 = jnp.zeros((n, half), F32)
    ca = jnp.concatenate([cos, cos, one], axis=1)
    sp = jnp.concatenate([z8, sin, zero], axis=1)
    sm = jnp.concatenate([-sin, z8, zero], axis=1)
    return tuple(jnp.tile(t, (1, LANES // HEAD_DIM)) for t in (ca, sp, sm))


def _rope_kernel(q_ref, k_ref, v_ref, ca_ref, sp_ref, sm_ref, qo_ref, ko_ref, qb_ref, kb_ref, vb_ref, km_ref):
    ca, sp, sm = ca_ref[...], sp_ref[...], sm_ref[...]
    half = ROT_DIM // 2
    for c in range(MIX_HALF // LANES):
        sl = slice(c * LANES, (c + 1) * LANES)
        for src, dst in ((q_ref, qo_ref), (k_ref, ko_ref)):
            x = src[:, sl]
            dst[:, sl] = x * ca + pltpu.roll(x, half, 1) * sp + pltpu.roll(x, LANES - half, 1) * sm
    q = qo_ref[...]
    k = ko_ref[...]
    qb_ref[...] = (q * (HEAD_DIM ** -0.5)).astype(BF16)
    kb_ref[...] = k.astype(BF16)
    vb_ref[...] = v_ref[...].astype(BF16)
    km_ref[0] = jnp.mean(k, axis=0, keepdims=True)


def _rope(proj, tables):
    t_rows = proj.shape[0]
    ts = MOBA_BLOCK

    def col(c):
        return pl.BlockSpec((ts, MIX_HALF), lambda i, c=c: (i, c))

    tab = pl.BlockSpec((ts, LANES), lambda i: (i, 0))
    row = pl.BlockSpec((ts, MIX_HALF), lambda i: (i, 0))
    f32o = jax.ShapeDtypeStruct((t_rows, MIX_HALF), F32)
    bfo = jax.ShapeDtypeStruct((t_rows, MIX_HALF), BF16)
    outs = pl.pallas_call(
        _rope_kernel,
        grid=(t_rows // ts,),
        in_specs=[col(1), col(2), col(3), tab, tab, tab],
        out_specs=[row, row, row, row, row, pl.BlockSpec((1, 1, MIX_HALF), lambda i: (i, 0, 0))],
        out_shape=[f32o, f32o, bfo, bfo, bfo, jax.ShapeDtypeStruct((t_rows // ts, 1, MIX_HALF), F32)],
        compiler_params=_cparams(("parallel",)),
    )(proj, proj, proj, *tables)
    return list(outs[:5]) + [outs[5].reshape(t_rows // ts, MIX_HALF)]


def _top_rows_mask(gate, n_valid_rows, k_top):
    n = gate.shape[0]
    row = lax.broadcasted_iota(jnp.int32, gate.shape, 0)
    live = row < n_valid_rows
    sel = jnp.zeros(gate.shape, jnp.bool_)
    for _ in range(k_top):
        g = jnp.where(live, gate, NEG_INF)
        mx = jnp.max(g, axis=0, keepdims=True)
        first = jnp.min(jnp.where(live & (g == mx), row, n), axis=0, keepdims=True)
        pick = row == first
        sel = sel | pick
        live = live & jnp.logical_not(pick)
    return sel


def _top_lanes_mask(gate, n_valid, k_top):
    n = gate.shape[1]
    lane = lax.broadcasted_iota(jnp.int32, gate.shape, 1)
    live = lane < n_valid
    sel = jnp.zeros(gate.shape, jnp.bool_)
    for _ in range(k_top):
        g = jnp.where(live, gate, NEG_INF)
        mx = jnp.max(g, axis=1, keepdims=True)
        first = jnp.min(jnp.where(live & (g == mx), lane, n), axis=1, keepdims=True)
        pick = lane == first
        sel = sel | pick
        live = live & jnp.logical_not(pick)
    return sel


HEAD_PAIRS = MIX_HALF // LANES


def _moba_prompt_kernel(q_ref, qb_ref, kb_ref, vb_ref, km_ref, o_ref,
                        qbd_ref, sel_ref, m_ref, l_ref, acc_ref, *, n_blk):
    qi = pl.program_id(1)
    tq = MOBA_BLOCK
    lane = lax.broadcasted_iota(jnp.int32, (tq, LANES), 1)
    krow = lax.broadcasted_iota(jnp.int32, (tq, 2 * tq), 0)
    qcol = lax.broadcasted_iota(jnp.int32, (tq, 2 * tq), 1) % tq
    causal = krow <= qcol
    nt_dims = (((1,), (1,)), ((), ()))
    tn_dims = (((0,), (0,)), ((), ()))
    row0 = pl.multiple_of(qi * tq, tq)

    def attend(pr, r0, mask_of, first):
        ps = slice(pr * LANES, (pr + 1) * LANES)
        kblk = kb_ref[pl.ds(r0, tq), ps]
        vblk = vb_ref[pl.ds(r0, tq), ps]
        for ck in range(2 * tq // LANES):
            cs = slice(ck * LANES, (ck + 1) * LANES)
            hd = ck * LANES // tq
            qs = slice(ck * LANES - hd * tq, (ck + 1) * LANES - hd * tq)
            s = lax.dot_general(kblk, qbd_ref[pr, cs, :], nt_dims, preferred_element_type=F32)
            s = jnp.where(mask_of(cs), s, NEG_INF)
            m_blk = jnp.max(s, axis=0, keepdims=True)
            m_old = m_ref[pr, :, cs]
            m_new = m_blk if first else jnp.maximum(m_old, m_blk)
            p = jnp.exp(s - m_new)
            pv = lax.dot_general(vblk, p.astype(BF16), tn_dims, preferred_element_type=F32)
            pv = pv[hd * HEAD_DIM:(hd + 1) * HEAD_DIM]
            if first:
                l_ref[pr, :, cs] = jnp.sum(p, axis=0, keepdims=True)
                acc_ref[pr, hd, :, qs] = pv
            else:
                alpha = jnp.exp(m_old - m_new)
                l_ref[pr, :, cs] = alpha * l_ref[pr, :, cs] + jnp.sum(p, axis=0, keepdims=True)
                acc_ref[pr, hd, :, qs] = alpha * acc_ref[pr, hd, :, qs] + pv
            m_ref[pr, :, cs] = m_new

    for pr in range(HEAD_PAIRS):
        ps = slice(pr * LANES, (pr + 1) * LANES)
        qf = q_ref[:, ps] * (HEAD_DIM ** -0.5)
        qbd_f = jnp.concatenate([jnp.where(lane < HEAD_DIM, qf, 0.0), jnp.where(lane >= HEAD_DIM, qf, 0.0)], axis=0)
        qb = qb_ref[:, ps]
        zero = jnp.zeros_like(qb)
        qbd_ref[pr] = jnp.concatenate([jnp.where(lane < HEAD_DIM, qb, zero), jnp.where(lane >= HEAD_DIM, qb, zero)],
                                      axis=0)
        gate = lax.dot_general(km_ref[:, ps], qbd_f, nt_dims, precision=lax.Precision.HIGHEST,
                               preferred_element_type=F32)
        sel_ref[pr] = _top_rows_mask(gate, qi, MOBA_TOPK).astype(F32)
        attend(pr, row0, lambda cs: causal[:, cs], True)

    def blk_body(j, _):
        r0 = pl.multiple_of(j * tq, tq)
        for pr in range(HEAD_PAIRS):
            picked = sel_ref[pr, pl.ds(j, 1), :] > 0.5
            attend(pr, r0, lambda cs, picked=picked: picked[:, cs], False)
        return 0

    lax.fori_loop(0, qi, blk_body, 0)
    for pr in range(HEAD_PAIRS):
        l = l_ref[pr]
        out_t = jnp.concatenate([acc_ref[pr, 0] / l[:, :tq], acc_ref[pr, 1] / l[:, tq:]], axis=0)
        o_ref[:, pr * LANES:(pr + 1) * LANES] = out_t.T


def _moba_prompt(q_rot, qb, kb, vb, kmean, *, n_b, n_s):
    n_blk = n_s // MOBA_BLOCK
    tq = MOBA_BLOCK
    qspec = pl.BlockSpec((tq, MIX_HALF), lambda b, i: (b * n_blk + i, 0))
    kvspec = pl.BlockSpec((n_s, MIX_HALF), lambda b, i: (b, 0))
    return pl.pallas_call(
        functools.partial(_moba_prompt_kernel, n_blk=n_blk),
        grid=(n_b, n_blk),
        in_specs=[qspec, qspec, kvspec, kvspec, pl.BlockSpec((n_blk, MIX_HALF), lambda b, i: (b, 0))],
        out_specs=qspec,
        out_shape=jax.ShapeDtypeStruct((n_b * n_s, MIX_HALF), F32),
        scratch_shapes=[pltpu.VMEM((HEAD_PAIRS, 2 * tq, LANES), BF16),
                        pltpu.VMEM((HEAD_PAIRS, n_blk, 2 * tq), F32),
                        pltpu.VMEM((HEAD_PAIRS, 1, 2 * tq), F32),
                        pltpu.VMEM((HEAD_PAIRS, 1, 2 * tq), F32),
                        pltpu.VMEM((HEAD_PAIRS, 2, HEAD_DIM, tq), F32)],
        compiler_params=_cparams(("arbitrary", "arbitrary")),
    )(q_rot, qb, kb, vb, kmean)


PAGES_PER_STEP = 8
BLOCK_PAGES = MOBA_BLOCK // PAGE_SIZE


def _moba_sample_kernel(pt_ref, *refs, n_blk, n_q):
    del pt_ref
    kp = refs[:PAGES_PER_STEP]
    vp = refs[PAGES_PER_STEP:2 * PAGES_PER_STEP]
    qbt_ref, qbtf_ref, kn_ref, vn_ref, o_ref, oacc_ref, m_ref, l_ref, km_ref = refs[2 * PAGES_PER_STEP:]
    c = pl.program_id(1)
    ncol = ATT_HEADS * n_q
    blocks_per_step = PAGES_PER_STEP // BLOCK_PAGES
    nt_dims = (((1,), (1,)), ((), ()))
    qbt = qbt_ref[0]
    lane_c = lax.broadcasted_iota(jnp.int32, (ncol, LANES), 1)
    lane_k = lax.broadcasted_iota(jnp.int32, (MIX_HALF, LANES), 1)

    @pl.when(c == 0)
    def _():
        m_ref[...] = jnp.zeros(m_ref.shape, F32)
        l_ref[...] = jnp.zeros(l_ref.shape, F32)
        km_ref[...] = jnp.zeros(km_ref.shape, F32)

    for blk in range(blocks_per_step):
        n = c * blocks_per_step + blk
        kt = jnp.concatenate([kp[blk * BLOCK_PAGES + j][0, 0] for j in range(BLOCK_PAGES)], axis=1)
        vt = jnp.concatenate([vp[blk * BLOCK_PAGES + j][0, 0] for j in range(BLOCK_PAGES)], axis=1)
        kmean = jnp.sum(kt, axis=1, keepdims=True) * (1.0 / MOBA_BLOCK)
        km_ref[...] = jnp.where(lane_k == n, kmean, km_ref[...])
        s = jnp.dot(qbt, kt.astype(BF16), preferred_element_type=F32)
        m = jnp.max(s, axis=1, keepdims=True)
        p = jnp.exp(s - m)
        m_ref[...] = jnp.where(lane_c == n, m, m_ref[...])
        l_ref[...] = jnp.where(lane_c == n, jnp.sum(p, axis=1, keepdims=True), l_ref[...])
        oacc_ref[n] = lax.dot_general(p.astype(BF16), vt.astype(BF16), nt_dims,
                                      preferred_element_type=F32)

    @pl.when(c == pl.num_programs(1) - 1)
    def _():
        gate = jnp.dot(qbtf_ref[0], km_ref[...], precision=lax.Precision.HIGHEST,
                       preferred_element_type=F32)
        sel = _top_lanes_mask(gate, n_blk, MOBA_TOPK)
        s_own = lax.dot_general(qbt, kn_ref[...].astype(BF16), nt_dims, preferred_element_type=F32)
        qidx = lax.broadcasted_iota(jnp.int32, (ncol, n_q), 0) % n_q
        kidx = lax.broadcasted_iota(jnp.int32, (ncol, n_q), 1)
        s_own = jnp.where(kidx <= qidx, s_own, NEG_INF)
        m_all = m_ref[...]
        m_fin = jnp.maximum(jnp.max(jnp.where(sel, m_all, NEG_INF), axis=1, keepdims=True),
                            jnp.max(s_own, axis=1, keepdims=True))
        w = jnp.where(sel, jnp.exp(m_all - m_fin), 0.0)
        p_own = jnp.exp(s_own - m_fin)
        l_fin = jnp.sum(w * l_ref[...], axis=1, keepdims=True) + jnp.sum(p_own, axis=1, keepdims=True)
        w = w / l_fin
        p_own = p_own / l_fin
        acc = jnp.dot(p_own, vn_ref[...], preferred_element_type=F32)
        for n in range(n_blk):
            acc = acc + w[:, n:n + 1] * oacc_ref[n]
        head = lax.broadcasted_iota(jnp.int32, (n_q, MIX_HALF), 1) // HEAD_DIM
        out = jnp.zeros((n_q, MIX_HALF), F32)
        for h in range(ATT_HEADS):
            out = out + jnp.where(head == h, acc[h * n_q:(h + 1) * n_q], 0.0)
        o_ref[...] = out


def _moba_sample(page_table, cache_k, cache_v, layer_i, q_rot, k_rot, proj, *, n_b, n_q, row0):
    n_pages = page_table.shape[1]
    n_blk = n_pages // BLOCK_PAGES
    ncol = ATT_HEADS * n_q
    assert n_blk <= LANES and n_pages % PAGES_PER_STEP == 0 and row0 % n_q == 0
    blk0 = row0 // n_q
    q4 = (q_rot[row0:row0 + n_b * n_q] * (HEAD_DIM ** -0.5)).reshape(n_b, n_q, ATT_HEADS, HEAD_DIM)
    qbtf = jnp.einsum('bihd,hg->bhigd', q4, jnp.eye(ATT_HEADS, dtype=F32)).reshape(n_b, ncol, MIX_HALF)

    def page_spec(j):
        return pl.BlockSpec((1, 1, MIX_HALF, PAGE_SIZE),
                            lambda b, c, pt, j=j: (layer_i, pt[b, c * PAGES_PER_STEP + j], 0, 0))

    per_b3 = lambda shape: pl.BlockSpec(shape, lambda b, c, pt: (b, 0, 0))
    grid_spec = pltpu.PrefetchScalarGridSpec(
        num_scalar_prefetch=1,
        grid=(n_b, n_pages // PAGES_PER_STEP),
        in_specs=[page_spec(j) for j in range(PAGES_PER_STEP)] * 2
        + [per_b3((1, ncol, MIX_HALF)), per_b3((1, ncol, MIX_HALF)),
           pl.BlockSpec((n_q, MIX_HALF), lambda b, c, pt: (blk0 + b, 0)),
           pl.BlockSpec((n_q, MIX_HALF), lambda b, c, pt: (blk0 + b, 3))],
        out_specs=pl.BlockSpec((n_q, MIX_HALF), lambda b, c, pt: (b, 0)),
        scratch_shapes=[pltpu.VMEM((n_blk, ncol, MIX_HALF), F32),
                        pltpu.VMEM((ncol, LANES), F32), pltpu.VMEM((ncol, LANES), F32),
                        pltpu.VMEM((MIX_HALF, LANES), F32)])
    return pl.pallas_call(
        functools.partial(_moba_sample_kernel, n_blk=n_blk, n_q=n_q),
        grid_spec=grid_spec,
        out_shape=jax.ShapeDtypeStruct((n_b * n_q, MIX_HALF), F32),
        compiler_params=_cparams(("arbitrary", "arbitrary")),
    )(page_table, *([cache_k] * PAGES_PER_STEP), *([cache_v] * PAGES_PER_STEP), qbtf.astype(BF16), qbtf,
      k_rot, proj)


TOK_TILE = 256
SEG_ALIGN = SUBLANES
ASG_TILE = -(-(TOK_TILE * TOPK + N_EXPERTS * (SEG_ALIGN - 1)) // LANES) * LANES
EXP_TILE = 512


def _layer_norm_rows(h, g, b):
    mu = jnp.mean(h, axis=-1, keepdims=True)
    hc = h - mu
    var = jnp.mean(hc * hc, axis=-1, keepdims=True)
    return hc * lax.rsqrt(var + LN_EPS) * g + b


def _route_kernel(x_ref, ap_ref, as_ref, bp_ref, bs_ref, wo_ref, g_ref, bt_ref, wrh_ref, wrl_ref, br_ref,
                  x1_ref, xs_ref, dg_ref, cnt_ref, *, alpha, n_prompt_tiles):
    is_sample = pl.program_id(0) >= n_prompt_tiles
    a = jnp.where(is_sample, as_ref[...], ap_ref[...])
    b = jnp.where(is_sample, bs_ref[...], bp_ref[...])
    h = (alpha * x_ref[...]
         + jnp.dot(a.astype(BF16), wo_ref[0:MIX_HALF, :], preferred_element_type=F32)
         + jnp.dot(b.astype(BF16), wo_ref[MIX_HALF:, :], preferred_element_type=F32))
    x1 = _layer_norm_rows(h, g_ref[...], bt_ref[...])
    x1_ref[...] = x1
    x1h = x1.astype(BF16)
    x1l = (x1 - x1h.astype(F32)).astype(BF16)
    nt_dims = (((1,), (1,)), ((), ()))
    logits = (lax.dot_general(wrh_ref[...], x1h, nt_dims, preferred_element_type=F32)
              + lax.dot_general(wrh_ref[...], x1l, nt_dims, preferred_element_type=F32)
              + lax.dot_general(wrl_ref[...], x1h, nt_dims, preferred_element_type=F32)) + br_ref[...]
    row = lax.broadcasted_iota(jnp.int32, logits.shape, 0)
    g = logits
    picks, vals = [], []
    for _ in range(TOPK):
        mx = jnp.max(g, axis=0, keepdims=True)
        first = jnp.min(jnp.where(g == mx, row, N_EXPERTS), axis=0, keepdims=True)
        pick = row == first
        picks.append(pick)
        vals.append(mx)
        g = jnp.where(pick, NEG_INF, g)
    es = [jnp.exp(v - vals[0]) for v in vals]
    den = es[0] + es[1] + es[2] + es[3]
    onehot = [p.astype(F32) for p in picks]
    member = onehot[0] + onehot[1] + onehot[2] + onehot[3]
    t_r = lax.broadcasted_iota(jnp.int32, (TOK_TILE, TOK_TILE), 0)
    t_c = lax.broadcasted_iota(jnp.int32, (TOK_TILE, TOK_TILE), 1)
    before = (t_r < t_c).astype(BF16)
    rank = jnp.dot(member.astype(BF16), before, preferred_element_type=F32)
    cnt = jnp.sum(member, axis=1, keepdims=True)
    cnt = jnp.ceil(cnt * (1.0 / SEG_ALIGN)) * SEG_ALIGN
    e_r = lax.broadcasted_iota(jnp.int32, (N_EXPERTS, N_EXPERTS), 0)
    e_c = lax.broadcasted_iota(jnp.int32, (N_EXPERTS, N_EXPERTS), 1)
    lower = (e_c < e_r).astype(F32)
    off = jnp.dot(lower, jnp.broadcast_to(cnt, (N_EXPERTS, TOK_TILE)), precision=lax.Precision.HIGHEST,
                  preferred_element_type=F32)
    slot = off + rank
    dests = [jnp.sum(oh * slot, axis=0, keepdims=True) for oh in onehot]
    r_iota = lax.broadcasted_iota(jnp.int32, (ASG_TILE, TOK_TILE), 0)
    perm = jnp.zeros((ASG_TILE, TOK_TILE), F32)
    for d in dests:
        perm = jnp.where(r_iota == d.astype(jnp.int32), 1.0, perm)
    xs_ref[...] = jnp.dot(perm.astype(BF16), x1h, preferred_element_type=F32)
    dg_ref[0] = jnp.concatenate(dests + [e / den for e in es], axis=0)
    cnt_ref[0] = jnp.broadcast_to(cnt, (N_EXPERTS, LANES))


def _route(x, mix_a, mix_b, w_out_bf, ln_g, ln_b, wr_t, br, *, alpha):
    t = x.shape[0]
    nt = t // TOK_TILE
    npt = mix_a[0].shape[0] // TOK_TILE
    assert mix_a[0].shape[0] % TOK_TILE == 0 and mix_a[1].shape[0] == (nt - npt) * TOK_TILE
    full2 = lambda a: pl.BlockSpec(a.shape, lambda i: (0, 0))
    prompt_rows = pl.BlockSpec((TOK_TILE, MIX_HALF), lambda i: (jnp.minimum(i, npt - 1), 0))
    sample_rows = pl.BlockSpec((TOK_TILE, MIX_HALF), lambda i: (jnp.maximum(i - npt, 0), 0))
    wr_hi = wr_t.astype(BF16)
    wr_lo = (wr_t - wr_hi.astype(F32)).astype(BF16)
    return pl.pallas_call(
        functools.partial(_route_kernel, alpha=alpha, n_prompt_tiles=npt),
        grid=(nt,),
        in_specs=[pl.BlockSpec((TOK_TILE, D_MODEL), lambda i: (i, 0)),
                  prompt_rows, sample_rows, prompt_rows, sample_rows,
                  full2(w_out_bf), full2(ln_g), full2(ln_b), full2(wr_hi), full2(wr_lo), full2(br)],
        out_specs=[pl.BlockSpec((TOK_TILE, D_MODEL), lambda i: (i, 0)),
                   pl.BlockSpec((ASG_TILE, D_MODEL), lambda i: (i, 0)),
                   pl.BlockSpec((1, 2 * TOPK, TOK_TILE), lambda i: (i, 0, 0)),
                   pl.BlockSpec((1, N_EXPERTS, LANES), lambda i: (i, 0, 0))],
        out_shape=[jax.ShapeDtypeStruct((t, D_MODEL), F32),
                   jax.ShapeDtypeStruct((nt * ASG_TILE, D_MODEL), F32),
                   jax.ShapeDtypeStruct((nt, 2 * TOPK, TOK_TILE), F32),
                   jax.ShapeDtypeStruct((nt, N_EXPERTS, LANES), F32)],
        compiler_params=_cparams(("parallel",)),
    )(x, mix_a[0], mix_a[1], mix_b[0], mix_b[1], w_out_bf, ln_g, ln_b, wr_hi, wr_lo, br)


SEG_PIECES = tuple(1 << b for b in range(TOK_TILE.bit_length() - 1, SEG_ALIGN.bit_length() - 2, -1))
TILE_PIECES = tuple(1 << b for b in range(EXP_TILE.bit_length() - 1, SEG_ALIGN.bit_length() - 2, -1))


def _expert_kernel(te_ref, nu_ref, lo_ref, hi_ref, valid_ref, src_ref, dst_ref, len_ref, tot_ref,
                   xs_ref, wg_ref, bg_ref, wu_ref, bu_ref, wd_ref, bd_ref, ys_ref,
                   xbuf, ybuf, zbuf, wgb_ref, wub_ref, wdb_ref, gsem, ssem, zsem, *, n_tok_tiles):
    m = pl.program_id(0)
    n_used = nu_ref[0]

    def zero_tail(i, start):
        n = ASG_TILE - tot_ref[i]
        for size in SEG_PIECES:
            @pl.when((n & size) != 0)
            def _():
                row = pl.multiple_of(i * ASG_TILE + tot_ref[i] + (n & ~(2 * size - 1)), SEG_ALIGN)
                cp = pltpu.make_async_copy(zbuf.at[pl.ds(0, size)], ys_ref.at[pl.ds(row, size)], zsem)
                cp.start() if start else cp.wait()

    @pl.when(m == 0)
    def _():
        zbuf[...] = jnp.zeros(zbuf.shape, F32)

    @pl.when((m >= 1) & (m - 1 < n_tok_tiles))
    def _():
        zero_tail(m - 1, False)

    @pl.when(m < n_tok_tiles)
    def _():
        zero_tail(m, True)

    def for_pieces(t, fn):
        row0 = t * EXP_TILE

        def seg_body(s, _):
            start = jnp.maximum(dst_ref[s], row0)
            n = jnp.minimum(dst_ref[s] + len_ref[s], row0 + EXP_TILE) - start
            base_src = src_ref[s] + (start - dst_ref[s])
            base_dst = start - row0
            for size in SEG_PIECES:
                @pl.when((n & size) != 0)
                def _():
                    done = n & ~(2 * size - 1)
                    fn(pl.multiple_of(base_src + done, SEG_ALIGN), pl.multiple_of(base_dst + done, SEG_ALIGN), size)
            return 0

        lax.fori_loop(lo_ref[t], hi_ref[t], seg_body, 0)

    def wait_rows(n, copy_of):
        for size in TILE_PIECES:
            @pl.when((n & size) != 0)
            def _():
                copy_of(size).wait()

    def gather(t, start):
        slot = t % 2
        if start:
            for_pieces(t, lambda row, r, size: pltpu.make_async_copy(
                xs_ref.at[pl.ds(row, size)], xbuf.at[slot, pl.ds(r, size)], gsem.at[slot]).start())
        else:
            wait_rows(valid_ref[t], lambda size: pltpu.make_async_copy(
                xs_ref.at[pl.ds(0, size)], xbuf.at[slot, pl.ds(0, size)], gsem.at[slot]))

    def scatter(t, start):
        slot = t % 2
        if start:
            for_pieces(t, lambda row, r, size: pltpu.make_async_copy(
                ybuf.at[slot, pl.ds(r, size)], ys_ref.at[pl.ds(row, size)], ssem.at[slot]).start())
        else:
            wait_rows(valid_ref[t], lambda size: pltpu.make_async_copy(
                ybuf.at[slot, pl.ds(0, size)], ys_ref.at[pl.ds(0, size)], ssem.at[slot]))

    @pl.when((m == 0) & (n_used > 0))
    def _():
        gather(0, True)

    @pl.when(m + 1 < n_used)
    def _():
        gather(m + 1, True)

    @pl.when((m >= 2) & (m - 2 < n_used))
    def _():
        scatter(m - 2, False)

    @pl.when(m < n_used)
    def _():
        gather(m, False)
        prev = te_ref[jnp.maximum(m - 1, 0)]

        @pl.when((m == 0) | (te_ref[m] != prev))
        def _():
            wgb_ref[...] = wg_ref[0, 0].astype(BF16)
            wub_ref[...] = wu_ref[0, 0].astype(BF16)
            wdb_ref[...] = wd_ref[0, 0].astype(BF16)

        slot = m % 2

        def swiglu(rows):
            xb = xbuf[slot, 0:rows].astype(BF16)
            gt = jnp.minimum(jnp.dot(xb, wgb_ref[...], preferred_element_type=F32) + bg_ref[0, 0], SWIGLU_LIMIT)
            up = jnp.clip(jnp.dot(xb, wub_ref[...], preferred_element_type=F32) + bu_ref[0, 0],
                          -SWIGLU_LIMIT, SWIGLU_LIMIT)
            act = gt * _sigmoid(SWIGLU_ALPHA * gt) * (up + 1.0)
            ybuf[slot, 0:rows] = jnp.dot(act.astype(BF16), wdb_ref[...], preferred_element_type=F32) + bd_ref[0, 0]

        half_full = valid_ref[m] <= EXP_TILE // 2

        @pl.when(half_full)
        def _():
            swiglu(EXP_TILE // 2)

        @pl.when(jnp.logical_not(half_full))
        def _():
            swiglu(EXP_TILE)

        scatter(m, True)


def _experts(tile_expert, n_used, seg_lo, seg_hi, tile_valid, seg_src, seg_dst, seg_len, tile_total, xs, layer,
             w_gate, b_gate, w_up, b_up, w_down, b_down):
    n_tiles = tile_expert.shape[0]
    n_tok_tiles = tile_total.shape[0]
    assert n_tiles > n_tok_tiles
    wspec = pl.BlockSpec((1, 1, D_MODEL, D_MODEL), lambda m, te, *_: (layer, te[m], 0, 0))
    bspec = pl.BlockSpec((1, 1, 1, D_MODEL), lambda m, te, *_: (layer, te[m], 0, 0))
    grid_spec = pltpu.PrefetchScalarGridSpec(
        num_scalar_prefetch=9, grid=(n_tiles,),
        in_specs=[pl.BlockSpec(memory_space=pl.ANY), wspec, bspec, wspec, bspec, wspec, bspec],
        out_specs=pl.BlockSpec(memory_space=pl.ANY),
        scratch_shapes=[pltpu.VMEM((2, EXP_TILE, D_MODEL), F32), pltpu.VMEM((2, EXP_TILE, D_MODEL), F32),
                        pltpu.VMEM((TOK_TILE, D_MODEL), F32)]
        + [pltpu.VMEM((D_MODEL, D_MODEL), BF16)] * 3
        + [pltpu.SemaphoreType.DMA((2,)), pltpu.SemaphoreType.DMA((2,)), pltpu.SemaphoreType.DMA(())])
    depth = w_gate.shape[0]
    bshape = (depth, N_EXPERTS, 1, D_MODEL)
    return pl.pallas_call(
        functools.partial(_expert_kernel, n_tok_tiles=n_tok_tiles),
        grid_spec=grid_spec,
        out_shape=jax.ShapeDtypeStruct(xs.shape, F32),
        compiler_params=_cparams(("arbitrary",)),
    )(tile_expert, n_used, seg_lo, seg_hi, tile_valid, seg_src, seg_dst, seg_len, tile_total, xs,
      w_gate, b_gate.reshape(bshape), w_up, b_up.reshape(bshape), w_down, b_down.reshape(bshape))


def _combine_kernel(x1_ref, ys_ref, dg_ref, g_ref, bt_ref, o_ref, *, alpha):
    dg = dg_ref[0]
    r_iota = lax.broadcasted_iota(jnp.int32, (ASG_TILE, TOK_TILE), 0)
    comb = jnp.zeros((ASG_TILE, TOK_TILE), F32)
    for k in range(TOPK):
        comb = jnp.where(r_iota == dg[k:k + 1, :].astype(jnp.int32), dg[TOPK + k:TOPK + k + 1, :], comb)
    ffn = lax.dot_general(comb.astype(BF16), ys_ref[...].astype(BF16), (((0,), (0,)), ((), ())),
                          preferred_element_type=F32)
    o_ref[...] = _layer_norm_rows(alpha * x1_ref[...] + ffn, g_ref[...], bt_ref[...])


def _combine(x1, ys, dg, ln_g, ln_b, *, alpha):
    t = x1.shape[0]
    nt = t // TOK_TILE
    return pl.pallas_call(
        functools.partial(_combine_kernel, alpha=alpha),
        grid=(nt,),
        in_specs=[pl.BlockSpec((TOK_TILE, D_MODEL), lambda i: (i, 0)),
                  pl.BlockSpec((ASG_TILE, D_MODEL), lambda i: (i, 0)),
                  pl.BlockSpec((1, 2 * TOPK, TOK_TILE), lambda i: (i, 0, 0)),
                  pl.BlockSpec((1, D_MODEL), lambda i: (0, 0)),
                  pl.BlockSpec((1, D_MODEL), lambda i: (0, 0))],
        out_specs=pl.BlockSpec((TOK_TILE, D_MODEL), lambda i: (i, 0)),
        out_shape=jax.ShapeDtypeStruct((t, D_MODEL), F32),
        compiler_params=_cparams(("parallel",)),
    )(x1, ys, dg, ln_g, ln_b)


def _channel_mix(x, mix_a, mix_b, p, layer, alpha):
    t = x.shape[0]
    nt = t // TOK_TILE
    x1, xs, dg, cnt = _route(
        x, mix_a, mix_b, p['w_out'][layer].astype(BF16), p['ln_g'][layer, 0].reshape(1, D_MODEL),
        p['ln_b'][layer, 0].reshape(1, D_MODEL), p['router_w'][layer].T.astype(F32),
        p['router_b'][layer].reshape(N_EXPERTS, 1).astype(F32), alpha=alpha)
    cnt = cnt[:, :, 0].astype(jnp.int32)
    local_off = jnp.cumsum(cnt, axis=1) - cnt
    tile_rows = (jnp.arange(nt, dtype=jnp.int32) * ASG_TILE)[:, None] + local_off
    total = jnp.sum(cnt, axis=0)
    padded = (total + EXP_TILE - 1) // EXP_TILE * EXP_TILE
    pend = jnp.cumsum(padded)
    expert_rows = (pend - padded)[None, :] + jnp.cumsum(cnt, axis=0) - cnt
    n_tiles = -(-(nt * (TOK_TILE * TOPK + N_EXPERTS * (SEG_ALIGN - 1))) // EXP_TILE) + N_EXPERTS + 2
    tile_start = (jnp.arange(n_tiles, dtype=jnp.int32) * EXP_TILE)[:, None]
    count_below = lambda a, bound: jnp.sum((a[None, :] < bound).astype(jnp.int32), axis=1)
    tile_expert = jnp.minimum(count_below(pend, tile_start + 1), N_EXPERTS - 1)
    n_used = (pend[-1:] // EXP_TILE).astype(jnp.int32)
    seg_src = tile_rows.T.reshape(-1)
    seg_dst = expert_rows.T.reshape(-1).astype(jnp.int32)
    seg_len = cnt.T.reshape(-1)
    seg_lo = count_below(seg_dst + seg_len, tile_start + 1)
    seg_hi = count_below(seg_dst, tile_start + EXP_TILE)
    expert_end = (pend - padded + total)[tile_expert]
    tile_valid = jnp.clip(expert_end - tile_start[:, 0], 0, EXP_TILE).astype(jnp.int32)
    ys = _experts(tile_expert, n_used, seg_lo, seg_hi, tile_valid, seg_src, seg_dst, seg_len, jnp.sum(cnt, axis=1),
                  xs, layer,
                  p['moe_w_gate'], p['moe_b_gate'], p['moe_w_up'], p['moe_b_up'], p['moe_w_down'], p['moe_b_down'])
    return _combine(x1, ys, dg, p['ln_g'][layer, 1].reshape(1, D_MODEL), p['ln_b'][layer, 1].reshape(1, D_MODEL),
                    alpha=alpha)


def kernel(x_prompt, x_sample, state_s5, state_pool, cache_k, cache_v, page_table, w_in_ab, s5_lambda_re, s5_lambda_im, s5_b_re, s5_b_im, s5_c_re, s5_c_im, s5_d, s5_log_dt, s5_w_glu, s5_b_glu, gm_norm_g, gm_w_s, gm_b_s, w_in_cd, pool_w, pool_scale, w_out, ln_g, ln_b, router_w, router_b, moe_w_gate, moe_b_gate, moe_w_up, moe_b_up, moe_w_down, moe_b_down):
    p = dict(w_in_ab=w_in_ab, s5_lambda_re=s5_lambda_re, s5_lambda_im=s5_lambda_im,
             s5_b_re=s5_b_re, s5_b_im=s5_b_im, s5_c_re=s5_c_re, s5_c_im=s5_c_im, s5_d=s5_d,
             s5_log_dt=s5_log_dt, s5_w_glu=s5_w_glu, s5_b_glu=s5_b_glu, gm_norm_g=gm_norm_g,
             gm_w_s=gm_w_s, gm_b_s=gm_b_s, w_in_cd=w_in_cd, pool_w=pool_w, pool_scale=pool_scale,
             w_out=w_out, ln_g=ln_g, ln_b=ln_b, router_w=router_w, router_b=router_b,
             moe_w_gate=moe_w_gate, moe_b_gate=moe_b_gate, moe_w_up=moe_w_up, moe_b_up=moe_b_up,
             moe_w_down=moe_w_down, moe_b_down=moe_b_down)
    n_bp, n_sp, _ = x_prompt.shape
    n_bs, n_ss, _ = x_sample.shape
    t_p, t_s = n_bp * n_sp, n_bs * n_ss
    depth = w_out.shape[0]
    alpha = (2 * depth) ** 0.25
    past_len = page_table.shape[1] * PAGE_SIZE
    x = jnp.concatenate([x_prompt.reshape(t_p, D_MODEL), x_sample.reshape(t_s, D_MODEL)], axis=0)
    zero_s5 = jnp.zeros((n_bp, S5_GROUPS, S5_STATE, 2), F32)
    pos = jnp.concatenate([jnp.tile(jnp.arange(n_sp), n_bp), jnp.tile(past_len + jnp.arange(n_ss), n_bs)])
    rope_tables = _rope_tables(pos)
    cache_k2 = cache_k.transpose(0, 1, 3, 4, 2).reshape(cache_k.shape[0], cache_k.shape[1], MIX_HALF, PAGE_SIZE)
    cache_v2 = cache_v.transpose(0, 1, 3, 4, 2).reshape(cache_v.shape[0], cache_v.shape[1], MIX_HALF, PAGE_SIZE)
    s5_p, s5_s, gmv_s, pool_p, pool_s, k_p, v_p, k_s, v_s = [], [], [], [], [], [], [], [], []
    for layer in range(depth):
        i = layer // 2
        if layer % 2 == 0:
            proj = _proj(x, w_in_ab[i].astype(BF16), PROJ_TILE)
            a_p, b_p, _, st_p = _even_layer_mix(proj, zero_s5, p, i, n_b=n_bp, n_s=n_sp, sample=False)
            a_s, b_s, vn, st_s = _even_layer_mix(proj, state_s5[i], p, i, n_b=n_bs, n_s=n_ss, sample=True, row0=t_p)
            s5_p.append(st_p)
            s5_s.append(st_s)
            gmv_s.append(vn.reshape(n_bs, n_ss, MIX_HALF))
        else:
            proj = _proj(x, w_in_cd[i].astype(BF16), PROJ_TILE)
            pw = pool_w[i].astype(BF16)
            ps = pool_scale[i].reshape(1, MIX_HALF).astype(F32)
            a_p = _pool_mixer(proj, jnp.zeros((n_bp, POOL_HALO, MIX_HALF), F32), pw, ps, n_b=n_bp, n_s=n_sp, base=0)
            halo = jnp.concatenate([jnp.zeros((n_bs, POOL_HALO - POOL_BUF, MIX_HALF), F32),
                                    state_pool[i].astype(F32)], axis=1)
            a_s = _pool_mixer(proj, halo, pw, ps, n_b=n_bs, n_s=n_ss, base=POOL_BUF, row0=t_p)
            q_rot, k_rot, qb, kb, vb, kmean = _rope(proj, rope_tables)
            b_p = _moba_prompt(q_rot, qb, kb, vb, kmean, n_b=n_bp, n_s=n_sp)
            b_s = _moba_sample(page_table, cache_k2, cache_v2, i, q_rot, k_rot, proj, n_b=n_bs, n_q=n_ss, row0=t_p)
            c_p = proj[:t_p, :MIX_HALF].reshape(n_bp, n_sp, MIX_HALF)
            c_s = proj[t_p:, :MIX_HALF].reshape(n_bs, n_ss, MIX_HALF)
            pool_p.append(c_p[:, -POOL_BUF:])
            pool_s.append(jnp.concatenate([state_pool[i].astype(F32), c_s], axis=1)[:, -POOL_BUF:])
            k_p.append(k_rot[:t_p].reshape(n_bp, n_sp, ATT_HEADS, HEAD_DIM))
            k_s.append(k_rot[t_p:].reshape(n_bs, n_ss, ATT_HEADS, HEAD_DIM))
            v_p.append(proj[:t_p, 3 * MIX_HALF:].reshape(n_bp, n_sp, ATT_HEADS, HEAD_DIM))
            v_s.append(proj[t_p:, 3 * MIX_HALF:].reshape(n_bs, n_ss, ATT_HEADS, HEAD_DIM))
        x = _channel_mix(x, (a_p, a_s), (b_p, b_s), p, layer, alpha)
    return (x[:t_p].reshape(n_bp, n_sp, D_MODEL), x[t_p:].reshape(n_bs, n_ss, D_MODEL),
            jnp.stack(s5_p), jnp.stack(s5_s), jnp.stack(gmv_s), jnp.stack(pool_p), jnp.stack(pool_s),
            jnp.stack(k_p), jnp.stack(v_p), jnp.stack(k_s), jnp.stack(v_s))
```

```python
import functools
import math

import jax
import jax.numpy as jnp
from jax import lax
from jax.experimental import pallas as pl
from jax.experimental.pallas import tpu as pltpu

F32 = jnp.float32
BF16 = jnp.bfloat16

D_MODEL = 1024
MIX_HALF = D_MODEL // 2
S5_GROUP_CH = 16
S5_GROUPS = MIX_HALF // S5_GROUP_CH
S5_STATE = 64
GM_CHUNK = 128
GM_GROUPS = 4
GM_CH = MIX_HALF // GM_GROUPS
POOL_WINDOWS = (2, 4, 8, 16)
POOL_CH = MIX_HALF // len(POOL_WINDOWS)
POOL_BUF = max(POOL_WINDOWS) - 1
ATT_HEADS = 8
HEAD_DIM = MIX_HALF // ATT_HEADS
ROT_DIM = HEAD_DIM // 4
ROPE_THETA = 500000.0
MOBA_BLOCK = 256
MOBA_TOPK = 3
N_EXPERTS = 32
TOPK = 4
SWIGLU_LIMIT = 7.0
SWIGLU_ALPHA = 1.702
LN_EPS = 1e-5
PAGE_SIZE = 128

LANES = 128
SUBLANES = 8
VMEM_LIMIT = 56 * 1024 * 1024

S5_OCT = 4
S5_PAIRS = S5_GROUPS * S5_STATE // LANES
NEG_INF = float("-inf")


def _cparams(sem):
    return pltpu.CompilerParams(dimension_semantics=sem, vmem_limit_bytes=VMEM_LIMIT)


def _gelu(x):
    return 0.5 * x * (1.0 + jnp.tanh(math.sqrt(2.0 / math.pi) * (x + 0.044715 * (x * x * x))))


def _sigmoid(x):
    return 1.0 / (1.0 + jnp.exp(-x))


PROJ_TILE = 640


def _proj_kernel(x_ref, w_ref, o_ref):
    o_ref[...] = jnp.dot(x_ref[...].astype(BF16), w_ref[...], preferred_element_type=F32)


def _proj(x, w_bf16, tm):
    t, k = x.shape
    n = w_bf16.shape[1]
    return pl.pallas_call(
        _proj_kernel,
        grid=(t // tm,),
        in_specs=[pl.BlockSpec((tm, k), lambda i: (i, 0)),
                  pl.BlockSpec((k, n), lambda i: (0, 0))],
        out_specs=pl.BlockSpec((tm, n), lambda i: (i, 0)),
        out_shape=jax.ShapeDtypeStruct((t, n), F32),
        compiler_params=_cparams(("parallel",)),
    )(x, w_bf16)


def _s5_params(lam_re, lam_im, b_re, b_im, c_re, c_im, log_dt):
    dt = jnp.exp(log_dt.astype(F32))[:, None]
    lam = lax.complex(lam_re.astype(F32), lam_im.astype(F32))
    lam_bar = jnp.exp(lam * dt)
    b_bar = ((lam_bar - 1.0) / lam)[..., None] * lax.complex(b_re.astype(F32), b_im.astype(F32))
    eye = jnp.eye(SUBLANES, dtype=F32)
    bb = b_bar.reshape(S5_OCT, 8, S5_STATE, S5_GROUP_CH)

    def bdiag_b(t):
        return jnp.einsum('qgph,gk->qghkp', t, eye).reshape(S5_OCT, 128, 512)

    bw = jnp.concatenate([bdiag_b(bb.real), bdiag_b(bb.imag)], axis=-1).astype(BF16)
    cc_re = c_re.astype(F32).reshape(S5_OCT, 8, S5_GROUP_CH, S5_STATE)
    cc_im = c_im.astype(F32).reshape(S5_OCT, 8, S5_GROUP_CH, S5_STATE)

    def bdiag_c(t):
        return jnp.einsum('qghp,gk->qgpkh', t, eye).reshape(S5_OCT, 512, 128)

    cw = jnp.concatenate([bdiag_c(cc_re), -bdiag_c(cc_im)], axis=1).astype(BF16)
    rows = jnp.arange(SUBLANES)
    planes = []
    for d in (1, 2, 4):
        pw = jnp.exp(lam * dt * float(d)).reshape(S5_PAIRS, 1, LANES)
        m = (rows >= d).astype(F32)[None, :, None]
        planes += [pw.real * m, pw.imag * m]
    pw = jnp.exp((lam * dt).reshape(S5_PAIRS, 1, LANES) * (rows + 1).astype(F32)[None, :, None])
    planes += [pw.real, pw.imag]
    coef = jnp.stack(planes, axis=1).astype(F32)
    return bw, cw, coef


def _even_kernel(u_ref, gu_ref, gv_ref, x0_ref, bw_ref, coef_ref, cw_ref, d_ref, wglu_ref, bglu_ref,
                 ng_ref, m_ref, bias_ref, *rest, ts, chunk, per_block_init, with_vn):
    if with_vn:
        a_ref, b_ref, vn_ref, st_out_ref, st_ref, carry_ref = rest
    else:
        a_ref, b_ref, st_out_ref, st_ref, carry_ref = rest
        vn_ref = None
    n_rb = ts // SUBLANES

    if not per_block_init:
        @pl.when(pl.program_id(1) == 0)
        def _():
            carry_ref[...] = x0_ref[0]

    u = u_ref[...]
    ub = u.astype(BF16)
    for q in range(S5_OCT):
        bu = jnp.dot(ub[:, q * 128:(q + 1) * 128], bw_ref[q], preferred_element_type=F32)
        for c in range(4):
            st_ref[q * 4 + c] = bu[:, c * 128:(c + 1) * 128]
            st_ref[S5_PAIRS + q * 4 + c] = bu[:, 512 + c * 128:512 + (c + 1) * 128]

    def pair_body(j, _):
        cf = coef_ref[j]
        a1r, a1i, a2r, a2i, a4r, a4i, pr, pi = [cf[k] for k in range(8)]

        def rb_body(r, carry):
            cr, ci = carry
            row = pl.multiple_of(r * SUBLANES, SUBLANES)
            xr = st_ref[j, pl.ds(row, SUBLANES), :]
            xi = st_ref[S5_PAIRS + j, pl.ds(row, SUBLANES), :]
            for d, ar, ai in ((1, a1r, a1i), (2, a2r, a2i), (4, a4r, a4i)):
                sr = pltpu.roll(xr, d, 0)
                si = pltpu.roll(xi, d, 0)
                xr, xi = xr + ar * sr - ai * si, xi + ar * si + ai * sr
            if per_block_init:
                cr = x0_ref[j, r]
                ci = x0_ref[S5_PAIRS + j, r]
            xr, xi = xr + pr * cr - pi * ci, xi + pr * ci + pi * cr
            st_ref[j, pl.ds(row, SUBLANES), :] = xr
            st_ref[S5_PAIRS + j, pl.ds(row, SUBLANES), :] = xi
            ncr = jnp.broadcast_to(xr[SUBLANES - 1:SUBLANES, :], (SUBLANES, LANES))
            nci = jnp.broadcast_to(xi[SUBLANES - 1:SUBLANES, :], (SUBLANES, LANES))
            if per_block_init:
                st_out_ref[j, r] = ncr
                st_out_ref[S5_PAIRS + j, r] = nci
            return ncr, nci

        cr, ci = lax.fori_loop(0, n_rb, rb_body, (carry_ref[j], carry_ref[S5_PAIRS + j]),
                               unroll=min(4, n_rb))
        carry_ref[j] = cr
        carry_ref[S5_PAIRS + j] = ci
        return 0

    lax.fori_loop(0, S5_PAIRS, pair_body, 0)

    if not per_block_init:
        @pl.when(pl.program_id(1) == pl.num_programs(1) - 1)
        def _():
            st_out_ref[0] = carry_ref[...]

    ys = []
    for q in range(S5_OCT):
        xq = jnp.concatenate([st_ref[q * 4 + c] for c in range(4)]
                             + [st_ref[S5_PAIRS + q * 4 + c] for c in range(4)], axis=-1)
        ys.append(jnp.dot(xq.astype(BF16), cw_ref[q], preferred_element_type=F32))
    y = jnp.concatenate(ys, axis=-1) + d_ref[...] * u
    g = _gelu(y)
    z = jnp.dot(g.astype(BF16), wglu_ref[...], preferred_element_type=F32) + bglu_ref[...]
    a_ref[...] = g * _sigmoid(z)

    gu = _gelu(gu_ref[...])
    gv = _gelu(gv_ref[...])
    for gi in range(GM_GROUPS):
        sl = slice(gi * GM_CH, (gi + 1) * GM_CH)
        v = gv[:, sl]
        mu = jnp.mean(v, axis=-1, keepdims=True)
        vc = v - mu
        var = jnp.mean(vc * vc, axis=-1, keepdims=True)
        vn = vc * lax.rsqrt(var + LN_EPS) * ng_ref[:, sl]
        if with_vn:
            vn_ref[:, sl] = vn
        vnb = vn.astype(BF16)
        for c in range(ts // chunk):
            rs = slice(c * chunk, (c + 1) * chunk)
            s = jnp.dot(m_ref[gi], vnb[rs], preferred_element_type=F32) + bias_ref[:, sl]
            b_ref[rs, sl] = gu[rs, sl] * s


def _even_mixer(proj, x0, s5p, d_skip, w_glu, b_glu, norm_g, m_mix, bias, *, n_b, n_s, per_block_init, row0=0):
    bw, cw, coef = s5p
    if per_block_init:
        ts, grid, chunk = n_b * n_s, (1, 1), n_b * n_s
        assert n_s == SUBLANES
        n_rb = ts // SUBLANES
        x0_spec = pl.BlockSpec((2 * S5_PAIRS, n_rb, SUBLANES, LANES), lambda b, t: (0, 0, 0, 0))
        st_shape = (2 * S5_PAIRS, n_rb, SUBLANES, LANES)
        st_spec = x0_spec
    else:
        ts = min(512, n_s)
        grid, chunk = (n_b, n_s // ts), GM_CHUNK
        x0_spec = pl.BlockSpec((1, 2 * S5_PAIRS, SUBLANES, LANES), lambda b, t: (b, 0, 0, 0))
        st_shape = (n_b, 2 * S5_PAIRS, SUBLANES, LANES)
        st_spec = x0_spec
    nt = grid[1]
    with_vn = per_block_init
    blk0 = row0 // ts
    assert row0 % ts == 0

    def rows(col):
        return pl.BlockSpec((ts, MIX_HALF), lambda b, t, col=col: (blk0 + b * nt + t, col))

    def full(a):
        return pl.BlockSpec(a.shape, lambda b, t, nd=a.ndim: (0,) * nd)

    row_out = pl.BlockSpec((ts, MIX_HALF), lambda b, t: (b * nt + t, 0))
    rows_shape = jax.ShapeDtypeStruct((n_b * n_s, MIX_HALF), F32)
    n_row_outs = 3 if with_vn else 2
    out_shape = [rows_shape] * n_row_outs + [jax.ShapeDtypeStruct(st_shape, F32)]
    out_specs = [row_out] * n_row_outs + [st_spec]
    weights = (bw, coef, cw, d_skip, w_glu, b_glu, norm_g, m_mix, bias)
    outs = pl.pallas_call(
        functools.partial(_even_kernel, ts=ts, chunk=chunk, per_block_init=per_block_init, with_vn=with_vn),
        grid=grid,
        in_specs=[rows(0), rows(1), rows(2), x0_spec] + [full(w) for w in weights],
        out_specs=out_specs,
        out_shape=out_shape,
        scratch_shapes=[pltpu.VMEM((2 * S5_PAIRS, ts, LANES), F32),
                        pltpu.VMEM((2 * S5_PAIRS, SUBLANES, LANES), F32)],
        compiler_params=_cparams(("arbitrary", "arbitrary")),
    )(proj, proj, proj, x0, *weights)
    if with_vn:
        return outs[0], outs[1], outs[2], outs[3]
    return outs[0], outs[1], None, outs[2]


def _state_to_lanes(x0, n_b):
    re = x0[..., 0].astype(F32).reshape(n_b, S5_PAIRS, LANES)
    im = x0[..., 1].astype(F32).reshape(n_b, S5_PAIRS, LANES)
    return jnp.concatenate([re, im], axis=1)


def _lanes_to_state(st, n_b):
    re = st[:, :S5_PAIRS].reshape(n_b, S5_GROUPS, S5_STATE)
    im = st[:, S5_PAIRS:].reshape(n_b, S5_GROUPS, S5_STATE)
    return jnp.stack([re, im], axis=-1)


def _even_layer_mix(proj, x0, p, i, *, n_b, n_s, sample, row0=0):
    s5p = _s5_params(p['s5_lambda_re'][i], p['s5_lambda_im'][i], p['s5_b_re'][i], p['s5_b_im'][i],
                     p['s5_c_re'][i], p['s5_c_im'][i], p['s5_log_dt'][i])
    st0 = _state_to_lanes(x0, n_b)
    if sample:
        chunk = n_s
        x0k = jnp.broadcast_to(st0.transpose(1, 0, 2)[:, :, None, :], (2 * S5_PAIRS, n_b, SUBLANES, LANES))
        w = jnp.tril(p['gm_w_s'][i][:, :chunk, :chunk])
        m_mix = jnp.einsum('bc,gij->gbicj', jnp.eye(n_b, dtype=F32), w).reshape(GM_GROUPS, n_b * chunk, n_b * chunk)
        bias_rows = jnp.tile(p['gm_b_s'][i][:, :chunk].T, (n_b, 1))
    else:
        x0k = jnp.broadcast_to(st0[:, :, None, :], (n_b, 2 * S5_PAIRS, SUBLANES, LANES))
        m_mix = jnp.tril(p['gm_w_s'][i][:, :GM_CHUNK, :GM_CHUNK])
        bias_rows = p['gm_b_s'][i][:, :GM_CHUNK].T
    bias = jnp.repeat(bias_rows.astype(F32), GM_CH, axis=1)
    a, b, vn, st = _even_mixer(
        proj, x0k, s5p, p['s5_d'][i].reshape(1, MIX_HALF).astype(F32), p['s5_w_glu'][i].astype(BF16),
        p['s5_b_glu'][i].reshape(1, MIX_HALF).astype(F32), p['gm_norm_g'][i].reshape(1, MIX_HALF).astype(F32),
        m_mix.astype(BF16), bias, n_b=n_b, n_s=n_s, per_block_init=sample, row0=row0)
    if sample:
        st = st[:, :, 0, :].transpose(1, 0, 2)
    else:
        st = st[:, :, 0, :]
    return a, b, vn, _lanes_to_state(st, n_b)


POOL_HALO = 16


def _pool_kernel(c_ref, halo_ref, w_ref, scale_ref, o_ref, hist_ref, *, ts, base):
    t = pl.program_id(1)

    @pl.when(t == 0)
    def _():
        hist_ref[0:POOL_HALO, :] = halo_ref[0]

    @pl.when(t > 0)
    def _():
        hist_ref[0:POOL_HALO, :] = hist_ref[ts:ts + POOL_HALO, :]

    hist_ref[POOL_HALO:POOL_HALO + ts, :] = c_ref[...]
    pos = base + t * ts + lax.broadcasted_iota(jnp.int32, (ts, 1), 0)
    for g, win in enumerate(POOL_WINDOWS):
        sl = slice(g * POOL_CH, (g + 1) * POOL_CH)
        x = hist_ref[POOL_HALO:POOL_HALO + ts, sl]
        acc = x
        for d in range(1, win):
            acc = acc + hist_ref[POOL_HALO - d:POOL_HALO - d + ts, sl]
        cnt = jnp.minimum(pos + 1, win).astype(F32)
        pooled = acc / cnt - x
        y = jnp.dot(pooled.astype(BF16), w_ref[g], preferred_element_type=F32)
        o_ref[:, sl] = y * scale_ref[:, sl]


def _pool_mixer(proj, halo, w, scale, *, n_b, n_s, base, row0=0):
    ts = min(512, n_s)
    nt = n_s // ts
    blk0 = row0 // ts
    assert row0 % ts == 0
    return pl.pallas_call(
        functools.partial(_pool_kernel, ts=ts, base=base),
        grid=(n_b, nt),
        in_specs=[pl.BlockSpec((ts, MIX_HALF), lambda b, t: (blk0 + b * nt + t, 0)),
                  pl.BlockSpec((1, POOL_HALO, MIX_HALF), lambda b, t: (b, 0, 0)),
                  pl.BlockSpec(w.shape, lambda b, t: (0, 0, 0)),
                  pl.BlockSpec((1, MIX_HALF), lambda b, t: (0, 0))],
        out_specs=pl.BlockSpec((ts, MIX_HALF), lambda b, t: (b * nt + t, 0)),
        out_shape=jax.ShapeDtypeStruct((n_b * n_s, MIX_HALF), F32),
        scratch_shapes=[pltpu.VMEM((POOL_HALO + ts, MIX_HALF), F32)],
        compiler_params=_cparams(("arbitrary", "arbitrary")),
    )(proj, halo, w, scale)


def _rope_tables(pos):
    half = ROT_DIM // 2
    inv = ROPE_THETA ** (-jnp.arange(half, dtype=F32) * 2.0 / ROT_DIM)
    ang = pos.astype(F32)[:, None] * inv[None, :]
    cos, sin = jnp.cos(ang), jnp.sin(ang)
    n = pos.shape[0]
    one = jnp.ones((n, HEAD_DIM - ROT_DIM), F32)
    zero = jnp.zeros((n, HEAD_DIM - ROT_DIM), F32)
    z8 = jnp.zeros((n, half), F32)
    ca = jnp.concatenate([cos, cos, one], axis=1)
    sp = jnp.concatenate([z8, sin, zero], axis=1)
    sm = jnp.concatenate([-sin, z8, zero], axis=1)
    return tuple(jnp.tile(t, (1, LANES // HEAD_DIM)) for t in (ca, sp, sm))


def _rope_kernel(q_ref, k_ref, v_ref, ca_ref, sp_ref, sm_ref, qo_ref, ko_ref, qb_ref, kb_ref, vb_ref, km_ref):
    ca, sp, sm = ca_ref[...], sp_ref[...], sm_ref[...]
    half = ROT_DIM // 2
    for c in range(MIX_HALF // LANES):
        sl = slice(c * LANES, (c + 1) * LANES)
        for src, dst in ((q_ref, qo_ref), (k_ref, ko_ref)):
            x = src[:, sl]
            dst[:, sl] = x * ca + pltpu.roll(x, half, 1) * sp + pltpu.roll(x, LANES - half, 1) * sm
    q = qo_ref[...]
    k = ko_ref[...]
    qb_ref[...] = (q * (HEAD_DIM ** -0.5)).astype(BF16)
    kb_ref[...] = k.astype(BF16)
    vb_ref[...] = v_ref[...].astype(BF16)
    km_ref[0] = jnp.mean(k, axis=0, keepdims=True)


def _rope(proj, tables):
    t_rows = proj.shape[0]
    ts = MOBA_BLOCK

    def col(c):
        return pl.BlockSpec((ts, MIX_HALF), lambda i, c=c: (i, c))

    tab = pl.BlockSpec((ts, LANES), lambda i: (i, 0))
    row = pl.BlockSpec((ts, MIX_HALF), lambda i: (i, 0))
    f32o = jax.ShapeDtypeStruct((t_rows, MIX_HALF), F32)
    bfo = jax.ShapeDtypeStruct((t_rows, MIX_HALF), BF16)
    outs = pl.pallas_call(
        _rope_kernel,
        grid=(t_rows // ts,),
        in_specs=[col(1), col(2), col(3), tab, tab, tab],
        out_specs=[row, row, row, row, row, pl.BlockSpec((1, 1, MIX_HALF), lambda i: (i, 0, 0))],
        out_shape=[f32o, f32o, bfo, bfo, bfo, jax.ShapeDtypeStruct((t_rows // ts, 1, MIX_HALF), F32)],
        compiler_params=_cparams(("parallel",)),
    )(proj, proj, proj, *tables)
    return list(outs[:5]) + [outs[5].reshape(t_rows // ts, MIX_HALF)]


def _top_rows_mask(gate, n_valid_rows, k_top):
    n = gate.shape[0]
    row = lax.broadcasted_iota(jnp.int32, gate.shape, 0)
    live = row < n_valid_rows
    sel = jnp.zeros(gate.shape, jnp.bool_)
    for _ in range(k_top):
        g = jnp.where(live, gate, NEG_INF)
        mx = jnp.max(g, axis=0, keepdims=True)
        first = jnp.min(jnp.where(live & (g == mx), row, n), axis=0, keepdims=True)
        pick = row == first
        sel = sel | pick
        live = live & jnp.logical_not(pick)
    return sel


def _top_lanes_mask(gate, n_valid, k_top):
    n = gate.shape[1]
    lane = lax.broadcasted_iota(jnp.int32, gate.shape, 1)
    live = lane < n_valid
    sel = jnp.zeros(gate.shape, jnp.bool_)
    for _ in range(k_top):
        g = jnp.where(live, gate, NEG_INF)
        mx = jnp.max(g, axis=1, keepdims=True)
        first = jnp.min(jnp.where(live & (g == mx), lane, n), axis=1, keepdims=True)
        pick = lane == first
        sel = sel | pick
        live = live & jnp.logical_not(pick)
    return sel


HEAD_PAIRS = MIX_HALF // LANES


def _moba_prompt_kernel(q_ref, qb_ref, kb_ref, vb_ref, km_ref, o_ref,
                        qbd_ref, sel_ref, m_ref, l_ref, acc_ref, s_ref, p_ref, a_ref, *, n_blk):
    qi = pl.program_id(1)
    tq = MOBA_BLOCK
    lane = lax.broadcasted_iota(jnp.int32, (tq, LANES), 1)
    krow = lax.broadcasted_iota(jnp.int32, (tq, 2 * tq), 0)
    qcol = lax.broadcasted_iota(jnp.int32, (tq, 2 * tq), 1) % tq
    causal = krow <= qcol
    nt_dims = (((1,), (1,)), ((), ()))
    tn_dims = (((0,), (0,)), ((), ()))
    row0 = pl.multiple_of(qi * tq, tq)

    def key_block(r0, mask_of, first):
        for pr in range(HEAD_PAIRS):
            kblk = kb_ref[pl.ds(r0, tq), pr * LANES:(pr + 1) * LANES]
            s_ref[pr] = lax.dot_general(kblk, qbd_ref[pr], nt_dims, preferred_element_type=F32)
        for pr in range(HEAD_PAIRS):
            mask = mask_of(pr)
            for ck in range(2 * tq // LANES):
                cs = slice(ck * LANES, (ck + 1) * LANES)
                s = jnp.where(mask[:, cs], s_ref[pr, :, cs], NEG_INF)
                m_blk = jnp.max(s, axis=0, keepdims=True)
                if first:
                    m_new = m_blk
                    p = jnp.exp(s - m_new)
                    l_ref[pr, :, cs] = jnp.sum(p, axis=0, keepdims=True)
                else:
                    m_old = m_ref[pr, :, cs]
                    m_new = jnp.maximum(m_old, m_blk)
                    p = jnp.exp(s - m_new)
                    alpha = jnp.exp(m_old - m_new)
                    a_ref[pr, :, cs] = alpha
                    l_ref[pr, :, cs] = alpha * l_ref[pr, :, cs] + jnp.sum(p, axis=0, keepdims=True)
                m_ref[pr, :, cs] = m_new
                p_ref[pr, :, cs] = p.astype(BF16)
        for pr in range(HEAD_PAIRS):
            vblk = vb_ref[pl.ds(r0, tq), pr * LANES:(pr + 1) * LANES]
            pv = lax.dot_general(vblk, p_ref[pr], tn_dims, preferred_element_type=F32)
            pv_a, pv_b = pv[:HEAD_DIM, :tq], pv[HEAD_DIM:, tq:]
            if first:
                acc_ref[pr, 0] = pv_a
                acc_ref[pr, 1] = pv_b
            else:
                alpha = a_ref[pr]
                acc_ref[pr, 0] = alpha[:, :tq] * acc_ref[pr, 0] + pv_a
                acc_ref[pr, 1] = alpha[:, tq:] * acc_ref[pr, 1] + pv_b

    for pr in range(HEAD_PAIRS):
        ps = slice(pr * LANES, (pr + 1) * LANES)
        qf = q_ref[:, ps] * (HEAD_DIM ** -0.5)
        qbd_f = jnp.concatenate([jnp.where(lane < HEAD_DIM, qf, 0.0), jnp.where(lane >= HEAD_DIM, qf, 0.0)], axis=0)
        qb = qb_ref[:, ps]
        zero = jnp.zeros_like(qb)
        qbd_ref[pr] = jnp.concatenate([jnp.where(lane < HEAD_DIM, qb, zero), jnp.where(lane >= HEAD_DIM, qb, zero)],
                                      axis=0)
        gate = lax.dot_general(km_ref[:, ps], qbd_f, nt_dims, precision=lax.Precision.HIGHEST,
                               preferred_element_type=F32)
        sel_ref[pr] = _top_rows_mask(gate, qi, MOBA_TOPK).astype(F32)

    key_block(row0, lambda pr: causal, True)

    def blk_body(j, _):
        key_block(pl.multiple_of(j * tq, tq), lambda pr: sel_ref[pr, pl.ds(j, 1), :] > 0.5, False)
        return 0

    lax.fori_loop(0, qi, blk_body, 0)
    for pr in range(HEAD_PAIRS):
        l = l_ref[pr]
        out_t = jnp.concatenate([acc_ref[pr, 0] / l[:, :tq], acc_ref[pr, 1] / l[:, tq:]], axis=0)
        o_ref[:, pr * LANES:(pr + 1) * LANES] = out_t.T


def _moba_prompt(q_rot, qb, kb, vb, kmean, *, n_b, n_s):
    n_blk = n_s // MOBA_BLOCK
    tq = MOBA_BLOCK
    qspec = pl.BlockSpec((tq, MIX_HALF), lambda b, i: (b * n_blk + i, 0))
    kvspec = pl.BlockSpec((n_s, MIX_HALF), lambda b, i: (b, 0))
    return pl.pallas_call(
        functools.partial(_moba_prompt_kernel, n_blk=n_blk),
        grid=(n_b, n_blk),
        in_specs=[qspec, qspec, kvspec, kvspec, pl.BlockSpec((n_blk, MIX_HALF), lambda b, i: (b, 0))],
        out_specs=qspec,
        out_shape=jax.ShapeDtypeStruct((n_b * n_s, MIX_HALF), F32),
        scratch_shapes=[pltpu.VMEM((HEAD_PAIRS, 2 * tq, LANES), BF16),
                        pltpu.VMEM((HEAD_PAIRS, n_blk, 2 * tq), F32),
                        pltpu.VMEM((HEAD_PAIRS, 1, 2 * tq), F32),
                        pltpu.VMEM((HEAD_PAIRS, 1, 2 * tq), F32),
                        pltpu.VMEM((HEAD_PAIRS, 2, HEAD_DIM, tq), F32),
                        pltpu.VMEM((HEAD_PAIRS, tq, 2 * tq), F32),
                        pltpu.VMEM((HEAD_PAIRS, tq, 2 * tq), BF16),
                        pltpu.VMEM((HEAD_PAIRS, 1, 2 * tq), F32)],
        compiler_params=_cparams(("arbitrary", "arbitrary")),
    )(q_rot, qb, kb, vb, kmean)


PAGES_PER_STEP = 8
BLOCK_PAGES = MOBA_BLOCK // PAGE_SIZE


def _moba_sample_kernel(pt_ref, *refs, n_blk, n_q):
    del pt_ref
    kp = refs[:PAGES_PER_STEP]
    vp = refs[PAGES_PER_STEP:2 * PAGES_PER_STEP]
    qbt_ref, qbtf_ref, kn_ref, vn_ref, o_ref, oacc_ref, m_ref, l_ref, km_ref = refs[2 * PAGES_PER_STEP:]
    c = pl.program_id(1)
    ncol = ATT_HEADS * n_q
    blocks_per_step = PAGES_PER_STEP // BLOCK_PAGES
    nt_dims = (((1,), (1,)), ((), ()))
    qbt = qbt_ref[0]
    lane_c = lax.broadcasted_iota(jnp.int32, (ncol, LANES), 1)
    lane_k = lax.broadcasted_iota(jnp.int32, (MIX_HALF, LANES), 1)

    @pl.when(c == 0)
    def _():
        m_ref[...] = jnp.zeros(m_ref.shape, F32)
        l_ref[...] = jnp.zeros(l_ref.shape, F32)
        km_ref[...] = jnp.zeros(km_ref.shape, F32)

    for blk in range(blocks_per_step):
        n = c * blocks_per_step + blk
        kt = jnp.concatenate([kp[blk * BLOCK_PAGES + j][0, 0] for j in range(BLOCK_PAGES)], axis=1)
        vt = jnp.concatenate([vp[blk * BLOCK_PAGES + j][0, 0] for j in range(BLOCK_PAGES)], axis=1)
        kmean = jnp.sum(kt, axis=1, keepdims=True) * (1.0 / MOBA_BLOCK)
        km_ref[...] = jnp.where(lane_k == n, kmean, km_ref[...])
        s = jnp.dot(qbt, kt.astype(BF16), preferred_element_type=F32)
        m = jnp.max(s, axis=1, keepdims=True)
        p = jnp.exp(s - m)
        m_ref[...] = jnp.where(lane_c == n, m, m_ref[...])
        l_ref[...] = jnp.where(lane_c == n, jnp.sum(p, axis=1, keepdims=True), l_ref[...])
        oacc_ref[n] = lax.dot_general(p.astype(BF16), vt.astype(BF16), nt_dims,
                                      preferred_element_type=F32)

    @pl.when(c == pl.num_programs(1) - 1)
    def _():
        gate = jnp.dot(qbtf_ref[0], km_ref[...], precision=lax.Precision.HIGHEST,
                       preferred_element_type=F32)
        sel = _top_lanes_mask(gate, n_blk, MOBA_TOPK)
        s_own = lax.dot_general(qbt, kn_ref[...].astype(BF16), nt_dims, preferred_element_type=F32)
        qidx = lax.broadcasted_iota(jnp.int32, (ncol, n_q), 0) % n_q
        kidx = lax.broadcasted_iota(jnp.int32, (ncol, n_q), 1)
        s_own = jnp.where(kidx <= qidx, s_own, NEG_INF)
        m_all = m_ref[...]
        m_fin = jnp.maximum(jnp.max(jnp.where(sel, m_all, NEG_INF), axis=1, keepdims=True),
                            jnp.max(s_own, axis=1, keepdims=True))
        w = jnp.where(sel, jnp.exp(m_all - m_fin), 0.0)
        p_own = jnp.exp(s_own - m_fin)
        l_fin = jnp.sum(w * l_ref[...], axis=1, keepdims=True) + jnp.sum(p_own, axis=1, keepdims=True)
        w = w / l_fin
        p_own = p_own / l_fin
        acc = jnp.dot(p_own, vn_ref[...], preferred_element_type=F32)
        for n in range(n_blk):
            acc = acc + w[:, n:n + 1] * oacc_ref[n]
        head = lax.broadcasted_iota(jnp.int32, (n_q, MIX_HALF), 1) // HEAD_DIM
        out = jnp.zeros((n_q, MIX_HALF), F32)
        for h in range(ATT_HEADS):
            out = out + jnp.where(head == h, acc[h * n_q:(h + 1) * n_q], 0.0)
        o_ref[...] = out


def _moba_sample(page_table, cache_k, cache_v, layer_i, q_rot, k_rot, proj, *, n_b, n_q, row0):
    n_pages = page_table.shape[1]
    n_blk = n_pages // BLOCK_PAGES
    ncol = ATT_HEADS * n_q
    assert n_blk <= LANES and n_pages % PAGES_PER_STEP == 0 and row0 % n_q == 0
    blk0 = row0 // n_q
    q4 = (q_rot[row0:row0 + n_b * n_q] * (HEAD_DIM ** -0.5)).reshape(n_b, n_q, ATT_HEADS, HEAD_DIM)
    qbtf = jnp.einsum('bihd,hg->bhigd', q4, jnp.eye(ATT_HEADS, dtype=F32)).reshape(n_b, ncol, MIX_HALF)

    def page_spec(j):
        return pl.BlockSpec((1, 1, MIX_HALF, PAGE_SIZE),
                            lambda b, c, pt, j=j: (layer_i, pt[b, c * PAGES_PER_STEP + j], 0, 0))

    per_b3 = lambda shape: pl.BlockSpec(shape, lambda b, c, pt: (b, 0, 0))
    grid_spec = pltpu.PrefetchScalarGridSpec(
        num_scalar_prefetch=1,
        grid=(n_b, n_pages // PAGES_PER_STEP),
        in_specs=[page_spec(j) for j in range(PAGES_PER_STEP)] * 2
        + [per_b3((1, ncol, MIX_HALF)), per_b3((1, ncol, MIX_HALF)),
           pl.BlockSpec((n_q, MIX_HALF), lambda b, c, pt: (blk0 + b, 0)),
           pl.BlockSpec((n_q, MIX_HALF), lambda b, c, pt: (blk0 + b, 3))],
        out_specs=pl.BlockSpec((n_q, MIX_HALF), lambda b, c, pt: (b, 0)),
        scratch_shapes=[pltpu.VMEM((n_blk, ncol, MIX_HALF), F32),
                        pltpu.VMEM((ncol, LANES), F32), pltpu.VMEM((ncol, LANES), F32),
                        pltpu.VMEM((MIX_HALF, LANES), F32)])
    return pl.pallas_call(
        functools.partial(_moba_sample_kernel, n_blk=n_blk, n_q=n_q),
        grid_spec=grid_spec,
        out_shape=jax.ShapeDtypeStruct((n_b * n_q, MIX_HALF), F32),
        compiler_params=_cparams(("arbitrary", "arbitrary")),
    )(page_table, *([cache_k] * PAGES_PER_STEP), *([cache_v] * PAGES_PER_STEP), qbtf.astype(BF16), qbtf,
      k_rot, proj)


TOK_TILE = 256
SEG_ALIGN = SUBLANES
ASG_TILE = -(-(TOK_TILE * TOPK + N_EXPERTS * (SEG_ALIGN - 1)) // LANES) * LANES
EXP_TILE = 512


def _layer_norm_rows(h, g, b):
    mu = jnp.mean(h, axis=-1, keepdims=True)
    hc = h - mu
    var = jnp.mean(hc * hc, axis=-1, keepdims=True)
    return hc * lax.rsqrt(var + LN_EPS) * g + b


def _route_kernel(x_ref, ap_ref, as_ref, bp_ref, bs_ref, wo_ref, g_ref, bt_ref, wrh_ref, wrl_ref, br_ref,
                  x1_ref, xs_ref, dg_ref, cnt_ref, *, alpha, n_prompt_tiles):
    is_sample = pl.program_id(0) >= n_prompt_tiles
    a = jnp.where(is_sample, as_ref[...], ap_ref[...])
    b = jnp.where(is_sample, bs_ref[...], bp_ref[...])
    h = (alpha * x_ref[...]
         + jnp.dot(a.astype(BF16), wo_ref[0:MIX_HALF, :], preferred_element_type=F32)
         + jnp.dot(b.astype(BF16), wo_ref[MIX_HALF:, :], preferred_element_type=F32))
    x1 = _layer_norm_rows(h, g_ref[...], bt_ref[...])
    x1_ref[...] = x1
    x1h = x1.astype(BF16)
    x1l = (x1 - x1h.astype(F32)).astype(BF16)
    nt_dims = (((1,), (1,)), ((), ()))
    logits = (lax.dot_general(wrh_ref[...], x1h, nt_dims, preferred_element_type=F32)
              + lax.dot_general(wrh_ref[...], x1l, nt_dims, preferred_element_type=F32)
              + lax.dot_general(wrl_ref[...], x1h, nt_dims, preferred_element_type=F32)) + br_ref[...]
    row = lax.broadcasted_iota(jnp.int32, logits.shape, 0)
    g = logits
    picks, vals = [], []
    for _ in range(TOPK):
        mx = jnp.max(g, axis=0, keepdims=True)
        first = jnp.min(jnp.where(g == mx, row, N_EXPERTS), axis=0, keepdims=True)
        pick = row == first
        picks.append(pick)
        vals.append(mx)
        g = jnp.where(pick, NEG_INF, g)
    es = [jnp.exp(v - vals[0]) for v in vals]
    den = es[0] + es[1] + es[2] + es[3]
    onehot = [p.astype(F32) for p in picks]
    member = onehot[0] + onehot[1] + onehot[2] + onehot[3]
    t_r = lax.broadcasted_iota(jnp.int32, (TOK_TILE, TOK_TILE), 0)
    t_c = lax.broadcasted_iota(jnp.int32, (TOK_TILE, TOK_TILE), 1)
    before = (t_r < t_c).astype(BF16)
    rank = jnp.dot(member.astype(BF16), before, preferred_element_type=F32)
    cnt = jnp.sum(member, axis=1, keepdims=True)
    cnt = jnp.ceil(cnt * (1.0 / SEG_ALIGN)) * SEG_ALIGN
    e_r = lax.broadcasted_iota(jnp.int32, (N_EXPERTS, N_EXPERTS), 0)
    e_c = lax.broadcasted_iota(jnp.int32, (N_EXPERTS, N_EXPERTS), 1)
    lower = (e_c < e_r).astype(F32)
    off = jnp.dot(lower, jnp.broadcast_to(cnt, (N_EXPERTS, TOK_TILE)), precision=lax.Precision.HIGHEST,
                  preferred_element_type=F32)
    slot = off + rank
    dests = [jnp.sum(oh * slot, axis=0, keepdims=True) for oh in onehot]
    r_iota = lax.broadcasted_iota(jnp.int32, (ASG_TILE, TOK_TILE), 0)
    perm = jnp.zeros((ASG_TILE, TOK_TILE), F32)
    for d in dests:
        perm = jnp.where(r_iota == d.astype(jnp.int32), 1.0, perm)
    xs_ref[...] = jnp.dot(perm.astype(BF16), x1h, preferred_element_type=F32)
    dg_ref[0] = jnp.concatenate(dests + [e / den for e in es], axis=0)
    cnt_ref[0] = jnp.broadcast_to(cnt, (N_EXPERTS, LANES))


def _route(x, mix_a, mix_b, w_out_bf, ln_g, ln_b, wr_t, br, *, alpha):
    t = x.shape[0]
    nt = t // TOK_TILE
    npt = mix_a[0].shape[0] // TOK_TILE
    assert mix_a[0].shape[0] % TOK_TILE == 0 and mix_a[1].shape[0] == (nt - npt) * TOK_TILE
    full2 = lambda a: pl.BlockSpec(a.shape, lambda i: (0, 0))
    prompt_rows = pl.BlockSpec((TOK_TILE, MIX_HALF), lambda i: (jnp.minimum(i, npt - 1), 0))
    sample_rows = pl.BlockSpec((TOK_TILE, MIX_HALF), lambda i: (jnp.maximum(i - npt, 0), 0))
    wr_hi = wr_t.astype(BF16)
    wr_lo = (wr_t - wr_hi.astype(F32)).astype(BF16)
    return pl.pallas_call(
        functools.partial(_route_kernel, alpha=alpha, n_prompt_tiles=npt),
        grid=(nt,),
        in_specs=[pl.BlockSpec((TOK_TILE, D_MODEL), lambda i: (i, 0)),
                  prompt_rows, sample_rows, prompt_rows, sample_rows,
                  full2(w_out_bf), full2(ln_g), full2(ln_b), full2(wr_hi), full2(wr_lo), full2(br)],
        out_specs=[pl.BlockSpec((TOK_TILE, D_MODEL), lambda i: (i, 0)),
                   pl.BlockSpec((ASG_TILE, D_MODEL), lambda i: (i, 0)),
                   pl.BlockSpec((1, 2 * TOPK, TOK_TILE), lambda i: (i, 0, 0)),
                   pl.BlockSpec((1, N_EXPERTS, LANES), lambda i: (i, 0, 0))],
        out_shape=[jax.ShapeDtypeStruct((t, D_MODEL), F32),
                   jax.ShapeDtypeStruct((nt * ASG_TILE, D_MODEL), F32),
                   jax.ShapeDtypeStruct((nt, 2 * TOPK, TOK_TILE), F32),
                   jax.ShapeDtypeStruct((nt, N_EXPERTS, LANES), F32)],
        compiler_params=_cparams(("parallel",)),
    )(x, mix_a[0], mix_a[1], mix_b[0], mix_b[1], w_out_bf, ln_g, ln_b, wr_hi, wr_lo, br)


SEG_PIECES = tuple(1 << b for b in range(TOK_TILE.bit_length() - 1, SEG_ALIGN.bit_length() - 2, -1))
TILE_PIECES = tuple(1 << b for b in range(EXP_TILE.bit_length() - 1, SEG_ALIGN.bit_length() - 2, -1))


def _expert_kernel(te_ref, nu_ref, lo_ref, hi_ref, valid_ref, src_ref, dst_ref, len_ref, tot_ref,
                   xs_ref, wg_ref, bg_ref, wu_ref, bu_ref, wd_ref, bd_ref, ys_ref,
                   xbuf, ybuf, zbuf, wgb_ref, wub_ref, wdb_ref, gsem, ssem, zsem, *, n_tok_tiles):
    m = pl.program_id(0)
    n_used = nu_ref[0]

    def zero_tail(i, start):
        n = ASG_TILE - tot_ref[i]
        for size in SEG_PIECES:
            @pl.when((n & size) != 0)
            def _():
                row = pl.multiple_of(i * ASG_TILE + tot_ref[i] + (n & ~(2 * size - 1)), SEG_ALIGN)
                cp = pltpu.make_async_copy(zbuf.at[pl.ds(0, size)], ys_ref.at[pl.ds(row, size)], zsem)
                cp.start() if start else cp.wait()

    @pl.when(m == 0)
    def _():
        zbuf[...] = jnp.zeros(zbuf.shape, F32)

    @pl.when((m >= 1) & (m - 1 < n_tok_tiles))
    def _():
        zero_tail(m - 1, False)

    @pl.when(m < n_tok_tiles)
    def _():
        zero_tail(m, True)

    def for_pieces(t, fn):
        row0 = t * EXP_TILE

        def seg_body(s, _):
            start = jnp.maximum(dst_ref[s], row0)
            n = jnp.minimum(dst_ref[s] + len_ref[s], row0 + EXP_TILE) - start
            base_src = src_ref[s] + (start - dst_ref[s])
            base_dst = start - row0
            for size in SEG_PIECES:
                @pl.when((n & size) != 0)
                def _():
                    done = n & ~(2 * size - 1)
                    fn(pl.multiple_of(base_src + done, SEG_ALIGN), pl.multiple_of(base_dst + done, SEG_ALIGN), size)
            return 0

        lax.fori_loop(lo_ref[t], hi_ref[t], seg_body, 0)

    def wait_rows(n, copy_of):
        for size in TILE_PIECES:
            @pl.when((n & size) != 0)
            def _():
                copy_of(size).wait()

    def gather(t, start):
        slot = t % 2
        if start:
            for_pieces(t, lambda row, r, size: pltpu.make_async_copy(
                xs_ref.at[pl.ds(row, size)], xbuf.at[slot, pl.ds(r, size)], gsem.at[slot]).start())
        else:
            wait_rows(valid_ref[t], lambda size: pltpu.make_async_copy(
                xs_ref.at[pl.ds(0, size)], xbuf.at[slot, pl.ds(0, size)], gsem.at[slot]))

    def scatter(t, start):
        slot = t % 2
        if start:
            for_pieces(t, lambda row, r, size: pltpu.make_async_copy(
                ybuf.at[slot, pl.ds(r, size)], ys_ref.at[pl.ds(row, size)], ssem.at[slot]).start())
        else:
            wait_rows(valid_ref[t], lambda size: pltpu.make_async_copy(
                ybuf.at[slot, pl.ds(0, size)], ys_ref.at[pl.ds(0, size)], ssem.at[slot]))

    @pl.when((m == 0) & (n_used > 0))
    def _():
        gather(0, True)

    @pl.when(m + 1 < n_used)
    def _():
        gather(m + 1, True)

    @pl.when((m >= 2) & (m - 2 < n_used))
    def _():
        scatter(m - 2, False)

    @pl.when(m < n_used)
    def _():
        gather(m, False)
        prev = te_ref[jnp.maximum(m - 1, 0)]

        @pl.when((m == 0) | (te_ref[m] != prev))
        def _():
            wgb_ref[...] = wg_ref[0, 0].astype(BF16)
            wub_ref[...] = wu_ref[0, 0].astype(BF16)
            wdb_ref[...] = wd_ref[0, 0].astype(BF16)

        slot = m % 2
        xb = xbuf[slot].astype(BF16)
        gt = jnp.minimum(jnp.dot(xb, wgb_ref[...], preferred_element_type=F32) + bg_ref[0, 0], SWIGLU_LIMIT)
        up = jnp.clip(jnp.dot(xb, wub_ref[...], preferred_element_type=F32) + bu_ref[0, 0],
                      -SWIGLU_LIMIT, SWIGLU_LIMIT)
        act = gt * _sigmoid(SWIGLU_ALPHA * gt) * (up + 1.0)
        ybuf[slot] = jnp.dot(act.astype(BF16), wdb_ref[...], preferred_element_type=F32) + bd_ref[0, 0]
        scatter(m, True)


def _experts(tile_expert, n_used, seg_lo, seg_hi, tile_valid, seg_src, seg_dst, seg_len, tile_total, xs, layer,
             w_gate, b_gate, w_up, b_up, w_down, b_down):
    n_tiles = tile_expert.shape[0]
    n_tok_tiles = tile_total.shape[0]
    assert n_tiles > n_tok_tiles
    wspec = pl.BlockSpec((1, 1, D_MODEL, D_MODEL), lambda m, te, *_: (layer, te[m], 0, 0))
    bspec = pl.BlockSpec((1, 1, 1, D_MODEL), lambda m, te, *_: (layer, te[m], 0, 0))
    grid_spec = pltpu.PrefetchScalarGridSpec(
        num_scalar_prefetch=9, grid=(n_tiles,),
        in_specs=[pl.BlockSpec(memory_space=pl.ANY), wspec, bspec, wspec, bspec, wspec, bspec],
        out_specs=pl.BlockSpec(memory_space=pl.ANY),
        scratch_shapes=[pltpu.VMEM((2, EXP_TILE, D_MODEL), F32), pltpu.VMEM((2, EXP_TILE, D_MODEL), F32),
                        pltpu.VMEM((TOK_TILE, D_MODEL), F32)]
        + [pltpu.VMEM((D_MODEL, D_MODEL), BF16)] * 3
        + [pltpu.SemaphoreType.DMA((2,)), pltpu.SemaphoreType.DMA((2,)), pltpu.SemaphoreType.DMA(())])
    depth = w_gate.shape[0]
    bshape = (depth, N_EXPERTS, 1, D_MODEL)
    return pl.pallas_call(
        functools.partial(_expert_kernel, n_tok_tiles=n_tok_tiles),
        grid_spec=grid_spec,
        out_shape=jax.ShapeDtypeStruct(xs.shape, F32),
        compiler_params=_cparams(("arbitrary",)),
    )(tile_expert, n_used, seg_lo, seg_hi, tile_valid, seg_src, seg_dst, seg_len, tile_total, xs,
      w_gate, b_gate.reshape(bshape), w_up, b_up.reshape(bshape), w_down, b_down.reshape(bshape))


def _combine_kernel(x1_ref, ys_ref, dg_ref, g_ref, bt_ref, o_ref, *, alpha):
    dg = dg_ref[0]
    r_iota = lax.broadcasted_iota(jnp.int32, (ASG_TILE, TOK_TILE), 0)
    comb = jnp.zeros((ASG_TILE, TOK_TILE), F32)
    for k in range(TOPK):
        comb = jnp.where(r_iota == dg[k:k + 1, :].astype(jnp.int32), dg[TOPK + k:TOPK + k + 1, :], comb)
    ffn = lax.dot_general(comb.astype(BF16), ys_ref[...].astype(BF16), (((0,), (0,)), ((), ())),
                          preferred_element_type=F32)
    o_ref[...] = _layer_norm_rows(alpha * x1_ref[...] + ffn, g_ref[...], bt_ref[...])


def _combine(x1, ys, dg, ln_g, ln_b, *, alpha):
    t = x1.shape[0]
    nt = t // TOK_TILE
    return pl.pallas_call(
        functools.partial(_combine_kernel, alpha=alpha),
        grid=(nt,),
        in_specs=[pl.BlockSpec((TOK_TILE, D_MODEL), lambda i: (i, 0)),
                  pl.BlockSpec((ASG_TILE, D_MODEL), lambda i: (i, 0)),
                  pl.BlockSpec((1, 2 * TOPK, TOK_TILE), lambda i: (i, 0, 0)),
                  pl.BlockSpec((1, D_MODEL), lambda i: (0, 0)),
                  pl.BlockSpec((1, D_MODEL), lambda i: (0, 0))],
        out_specs=pl.BlockSpec((TOK_TILE, D_MODEL), lambda i: (i, 0)),
        out_shape=jax.ShapeDtypeStruct((t, D_MODEL), F32),
        compiler_params=_cparams(("parallel",)),
    )(x1, ys, dg, ln_g, ln_b)


def _channel_mix(x, mix_a, mix_b, p, layer, alpha):
    t = x.shape[0]
    nt = t // TOK_TILE
    x1, xs, dg, cnt = _route(
        x, mix_a, mix_b, p['w_out'][layer].astype(BF16), p['ln_g'][layer, 0].reshape(1, D_MODEL),
        p['ln_b'][layer, 0].reshape(1, D_MODEL), p['router_w'][layer].T.astype(F32),
        p['router_b'][layer].reshape(N_EXPERTS, 1).astype(F32), alpha=alpha)
    cnt = cnt[:, :, 0].astype(jnp.int32)
    local_off = jnp.cumsum(cnt, axis=1) - cnt
    tile_rows = (jnp.arange(nt, dtype=jnp.int32) * ASG_TILE)[:, None] + local_off
    total = jnp.sum(cnt, axis=0)
    padded = (total + EXP_TILE - 1) // EXP_TILE * EXP_TILE
    pend = jnp.cumsum(padded)
    expert_rows = (pend - padded)[None, :] + jnp.cumsum(cnt, axis=0) - cnt
    n_tiles = -(-(nt * (TOK_TILE * TOPK + N_EXPERTS * (SEG_ALIGN - 1))) // EXP_TILE) + N_EXPERTS + 2
    tile_start = (jnp.arange(n_tiles, dtype=jnp.int32) * EXP_TILE)[:, None]
    count_below = lambda a, bound: jnp.sum((a[None, :] < bound).astype(jnp.int32), axis=1)
    tile_expert = jnp.minimum(count_below(pend, tile_start + 1), N_EXPERTS - 1)
    n_used = (pend[-1:] // EXP_TILE).astype(jnp.int32)
    seg_src = tile_rows.T.reshape(-1)
    seg_dst = expert_rows.T.reshape(-1).astype(jnp.int32)
    seg_len = cnt.T.reshape(-1)
    seg_lo = count_below(seg_dst + seg_len, tile_start + 1)
    seg_hi = count_below(seg_dst, tile_start + EXP_TILE)
    expert_end = (pend - padded + total)[tile_expert]
    tile_valid = jnp.clip(expert_end - tile_start[:, 0], 0, EXP_TILE).astype(jnp.int32)
    ys = _experts(tile_expert, n_used, seg_lo, seg_hi, tile_valid, seg_src, seg_dst, seg_len, jnp.sum(cnt, axis=1),
                  xs, layer,
                  p['moe_w_gate'], p['moe_b_gate'], p['moe_w_up'], p['moe_b_up'], p['moe_w_down'], p['moe_b_down'])
    return _combine(x1, ys, dg, p['ln_g'][layer, 1].reshape(1, D_MODEL), p['ln_b'][layer, 1].reshape(1, D_MODEL),
                    alpha=alpha)


def kernel(x_prompt, x_sample, state_s5, state_pool, cache_k, cache_v, page_table, w_in_ab, s5_lambda_re, s5_lambda_im, s5_b_re, s5_b_im, s5_c_re, s5_c_im, s5_d, s5_log_dt, s5_w_glu, s5_b_glu, gm_norm_g, gm_w_s, gm_b_s, w_in_cd, pool_w, pool_scale, w_out, ln_g, ln_b, router_w, router_b, moe_w_gate, moe_b_gate, moe_w_up, moe_b_up, moe_w_down, moe_b_down):
    p = dict(w_in_ab=w_in_ab, s5_lambda_re=s5_lambda_re, s5_lambda_im=s5_lambda_im,
             s5_b_re=s5_b_re, s5_b_im=s5_b_im, s5_c_re=s5_c_re, s5_c_im=s5_c_im, s5_d=s5_d,
             s5_log_dt=s5_log_dt, s5_w_glu=s5_w_glu, s5_b_glu=s5_b_glu, gm_norm_g=gm_norm_g,
             gm_w_s=gm_w_s, gm_b_s=gm_b_s, w_in_cd=w_in_cd, pool_w=pool_w, pool_scale=pool_scale,
             w_out=w_out, ln_g=ln_g, ln_b=ln_b, router_w=router_w, router_b=router_b,
             moe_w_gate=moe_w_gate, moe_b_gate=moe_b_gate, moe_w_up=moe_w_up, moe_b_up=moe_b_up,
             moe_w_down=moe_w_down, moe_b_down=moe_b_down)
    n_bp, n_sp, _ = x_prompt.shape
    n_bs, n_ss, _ = x_sample.shape
    t_p, t_s = n_bp * n_sp, n_bs * n_ss
    depth = w_out.shape[0]
    alpha = (2 * depth) ** 0.25
    past_len = page_table.shape[1] * PAGE_SIZE
    x = jnp.concatenate([x_prompt.reshape(t_p, D_MODEL), x_sample.reshape(t_s, D_MODEL)], axis=0)
    zero_s5 = jnp.zeros((n_bp, S5_GROUPS, S5_STATE, 2), F32)
    pos = jnp.concatenate([jnp.tile(jnp.arange(n_sp), n_bp), jnp.tile(past_len + jnp.arange(n_ss), n_bs)])
    rope_tables = _rope_tables(pos)
    cache_k2 = cache_k.transpose(0, 1, 3, 4, 2).reshape(cache_k.shape[0], cache_k.shape[1], MIX_HALF, PAGE_SIZE)
    cache_v2 = cache_v.transpose(0, 1, 3, 4, 2).reshape(cache_v.shape[0], cache_v.shape[1], MIX_HALF, PAGE_SIZE)
    s5_p, s5_s, gmv_s, pool_p, pool_s, k_p, v_p, k_s, v_s = [], [], [], [], [], [], [], [], []
    for layer in range(depth):
        i = layer // 2
        if layer % 2 == 0:
            proj = _proj(x, w_in_ab[i].astype(BF16), PROJ_TILE)
            a_p, b_p, _, st_p = _even_layer_mix(proj, zero_s5, p, i, n_b=n_bp, n_s=n_sp, sample=False)
            a_s, b_s, vn, st_s = _even_layer_mix(proj, state_s5[i], p, i, n_b=n_bs, n_s=n_ss, sample=True, row0=t_p)
            s5_p.append(st_p)
            s5_s.append(st_s)
            gmv_s.append(vn.reshape(n_bs, n_ss, MIX_HALF))
        else:
            proj = _proj(x, w_in_cd[i].astype(BF16), PROJ_TILE)
            pw = pool_w[i].astype(BF16)
            ps = pool_scale[i].reshape(1, MIX_HALF).astype(F32)
            a_p = _pool_mixer(proj, jnp.zeros((n_bp, POOL_HALO, MIX_HALF), F32), pw, ps, n_b=n_bp, n_s=n_sp, base=0)
            halo = jnp.concatenate([jnp.zeros((n_bs, POOL_HALO - POOL_BUF, MIX_HALF), F32),
                                    state_pool[i].astype(F32)], axis=1)
            a_s = _pool_mixer(proj, halo, pw, ps, n_b=n_bs, n_s=n_ss, base=POOL_BUF, row0=t_p)
            q_rot, k_rot, qb, kb, vb, kmean = _rope(proj, rope_tables)
            b_p = _moba_prompt(q_rot, qb, kb, vb, kmean, n_b=n_bp, n_s=n_sp)
            b_s = _moba_sample(page_table, cache_k2, cache_v2, i, q_rot, k_rot, proj, n_b=n_bs, n_q=n_ss, row0=t_p)
            c_p = proj[:t_p, :MIX_HALF].reshape(n_bp, n_sp, MIX_HALF)
            c_s = proj[t_p:, :MIX_HALF].reshape(n_bs, n_ss, MIX_HALF)
            pool_p.append(c_p[:, -POOL_BUF:])
            pool_s.append(jnp.concatenate([state_pool[i].astype(F32), c_s], axis=1)[:, -POOL_BUF:])
            k_p.append(k_rot[:t_p].reshape(n_bp, n_sp, ATT_HEADS, HEAD_DIM))
            k_s.append(k_rot[t_p:].reshape(n_bs, n_ss, ATT_HEADS, HEAD_DIM))
            v_p.append(proj[:t_p, 3 * MIX_HALF:].reshape(n_bp, n_sp, ATT_HEADS, HEAD_DIM))
            v_s.append(proj[t_p:, 3 * MIX_HALF:].reshape(n_bs, n_ss, ATT_HEADS, HEAD_DIM))
        x = _channel_mix(x, (a_p, a_s), (b_p, b_s), p, layer, alpha)
    return (x[:t_p].reshape(n_bp, n_sp, D_MODEL), x[t_p:].reshape(n_bs, n_ss, D_MODEL),
            jnp.stack(s5_p), jnp.stack(s5_s), jnp.stack(gmv_s), jnp.stack(pool_p), jnp.stack(pool_s),
            jnp.stack(k_p), jnp.stack(v_p), jnp.stack(k_s), jnp.stack(v_s))
```

```python
import functools
import math

import jax
import jax.numpy as jnp
from jax import lax
from jax.experimental import pallas as pl
from jax.experimental.pallas import tpu as pltpu

F32 = jnp.float32
BF16 = jnp.bfloat16

D_MODEL = 1024
MIX_HALF = D_MODEL // 2
S5_GROUP_CH = 16
S5_GROUPS = MIX_HALF // S5_GROUP_CH
S5_STATE = 64
GM_CHUNK = 128
GM_GROUPS = 4
GM_CH = MIX_HALF // GM_GROUPS
POOL_WINDOWS = (2, 4, 8, 16)
POOL_CH = MIX_HALF // len(POOL_WINDOWS)
POOL_BUF = max(POOL_WINDOWS) - 1
ATT_HEADS = 8
HEAD_DIM = MIX_HALF // ATT_HEADS
ROT_DIM = HEAD_DIM // 4
ROPE_THETA = 500000.0
MOBA_BLOCK = 256
MOBA_TOPK = 3
N_EXPERTS = 32
TOPK = 4
SWIGLU_LIMIT = 7.0
SWIGLU_ALPHA = 1.702
LN_EPS = 1e-5
PAGE_SIZE = 128

LANES = 128
SUBLANES = 8
VMEM_LIMIT = 56 * 1024 * 1024

S5_OCT = 4
S5_PAIRS = S5_GROUPS * S5_STATE // LANES
NEG_INF = float("-inf")


def _cparams(sem):
    return pltpu.CompilerParams(dimension_semantics=sem, vmem_limit_bytes=VMEM_LIMIT)


def _gelu(x):
    return 0.5 * x * (1.0 + jnp.tanh(math.sqrt(2.0 / math.pi) * (x + 0.044715 * (x * x * x))))


def _sigmoid(x):
    return 1.0 / (1.0 + jnp.exp(-x))


PROJ_TILE = 640


def _proj_kernel(x_ref, w_ref, o_ref):
    o_ref[...] = jnp.dot(x_ref[...].astype(BF16), w_ref[...], preferred_element_type=F32)


def _proj(x, w_bf16, tm):
    t, k = x.shape
    n = w_bf16.shape[1]
    return pl.pallas_call(
        _proj_kernel,
        grid=(t // tm,),
        in_specs=[pl.BlockSpec((tm, k), lambda i: (i, 0)),
                  pl.BlockSpec((k, n), lambda i: (0, 0))],
        out_specs=pl.BlockSpec((tm, n), lambda i: (i, 0)),
        out_shape=jax.ShapeDtypeStruct((t, n), F32),
        compiler_params=_cparams(("parallel",)),
    )(x, w_bf16)


def _s5_params(lam_re, lam_im, b_re, b_im, c_re, c_im, log_dt):
    dt = jnp.exp(log_dt.astype(F32))[:, None]
    lam = lax.complex(lam_re.astype(F32), lam_im.astype(F32))
    lam_bar = jnp.exp(lam * dt)
    b_bar = ((lam_bar - 1.0) / lam)[..., None] * lax.complex(b_re.astype(F32), b_im.astype(F32))
    eye = jnp.eye(SUBLANES, dtype=F32)
    bb = b_bar.reshape(S5_OCT, 8, S5_STATE, S5_GROUP_CH)

    def bdiag_b(t):
        return jnp.einsum('qgph,gk->qghkp', t, eye).reshape(S5_OCT, 128, 512)

    bw = jnp.concatenate([bdiag_b(bb.real), bdiag_b(bb.imag)], axis=-1).astype(BF16)
    cc_re = c_re.astype(F32).reshape(S5_OCT, 8, S5_GROUP_CH, S5_STATE)
    cc_im = c_im.astype(F32).reshape(S5_OCT, 8, S5_GROUP_CH, S5_STATE)

    def bdiag_c(t):
        return jnp.einsum('qghp,gk->qgpkh', t, eye).reshape(S5_OCT, 512, 128)

    cw = jnp.concatenate([bdiag_c(cc_re), -bdiag_c(cc_im)], axis=1).astype(BF16)
    rows = jnp.arange(SUBLANES)
    planes = []
    for d in (1, 2, 4):
        pw = jnp.exp(lam * dt * float(d)).reshape(S5_PAIRS, 1, LANES)
        m = (rows >= d).astype(F32)[None, :, None]
        planes += [pw.real * m, pw.imag * m]
    pw = jnp.exp((lam * dt).reshape(S5_PAIRS, 1, LANES) * (rows + 1).astype(F32)[None, :, None])
    planes += [pw.real, pw.imag]
    coef = jnp.stack(planes, axis=1).astype(F32)
    return bw, cw, coef


def _even_kernel(u_ref, gu_ref, gv_ref, x0_ref, bw_ref, coef_ref, cw_ref, d_ref, wglu_ref, bglu_ref,
                 ng_ref, m_ref, bias_ref, *rest, ts, chunk, per_block_init, with_vn):
    if with_vn:
        a_ref, b_ref, vn_ref, st_out_ref, st_ref, carry_ref = rest
    else:
        a_ref, b_ref, st_out_ref, st_ref, carry_ref = rest
        vn_ref = None
    n_rb = ts // SUBLANES

    if not per_block_init:
        @pl.when(pl.program_id(1) == 0)
        def _():
            carry_ref[...] = x0_ref[0]

    u = u_ref[...]
    ub = u.astype(BF16)
    for q in range(S5_OCT):
        bu = jnp.dot(ub[:, q * 128:(q + 1) * 128], bw_ref[q], preferred_element_type=F32)
        for c in range(4):
            st_ref[q * 4 + c] = bu[:, c * 128:(c + 1) * 128]
            st_ref[S5_PAIRS + q * 4 + c] = bu[:, 512 + c * 128:512 + (c + 1) * 128]

    def pair_body(j, _):
        cf = coef_ref[j]
        a1r, a1i, a2r, a2i, a4r, a4i, pr, pi = [cf[k] for k in range(8)]

        def rb_body(r, carry):
            cr, ci = carry
            row = pl.multiple_of(r * SUBLANES, SUBLANES)
            xr = st_ref[j, pl.ds(row, SUBLANES), :]
            xi = st_ref[S5_PAIRS + j, pl.ds(row, SUBLANES), :]
            for d, ar, ai in ((1, a1r, a1i), (2, a2r, a2i), (4, a4r, a4i)):
                sr = pltpu.roll(xr, d, 0)
                si = pltpu.roll(xi, d, 0)
                xr, xi = xr + ar * sr - ai * si, xi + ar * si + ai * sr
            if per_block_init:
                cr = x0_ref[j, r]
                ci = x0_ref[S5_PAIRS + j, r]
            xr, xi = xr + pr * cr - pi * ci, xi + pr * ci + pi * cr
            st_ref[j, pl.ds(row, SUBLANES), :] = xr
            st_ref[S5_PAIRS + j, pl.ds(row, SUBLANES), :] = xi
            ncr = jnp.broadcast_to(xr[SUBLANES - 1:SUBLANES, :], (SUBLANES, LANES))
            nci = jnp.broadcast_to(xi[SUBLANES - 1:SUBLANES, :], (SUBLANES, LANES))
            if per_block_init:
                st_out_ref[j, r] = ncr
                st_out_ref[S5_PAIRS + j, r] = nci
            return ncr, nci

        cr, ci = lax.fori_loop(0, n_rb, rb_body, (carry_ref[j], carry_ref[S5_PAIRS + j]),
                               unroll=min(4, n_rb))
        carry_ref[j] = cr
        carry_ref[S5_PAIRS + j] = ci
        return 0

    lax.fori_loop(0, S5_PAIRS, pair_body, 0)

    if not per_block_init:
        @pl.when(pl.program_id(1) == pl.num_programs(1) - 1)
        def _():
            st_out_ref[0] = carry_ref[...]

    ys = []
    for q in range(S5_OCT):
        xq = jnp.concatenate([st_ref[q * 4 + c] for c in range(4)]
                             + [st_ref[S5_PAIRS + q * 4 + c] for c in range(4)], axis=-1)
        ys.append(jnp.dot(xq.astype(BF16), cw_ref[q], preferred_element_type=F32))
    y = jnp.concatenate(ys, axis=-1) + d_ref[...] * u
    g = _gelu(y)
    z = jnp.dot(g.astype(BF16), wglu_ref[...], preferred_element_type=F32) + bglu_ref[...]
    a_ref[...] = g * _sigmoid(z)

    gu = _gelu(gu_ref[...])
    gv = _gelu(gv_ref[...])
    for gi in range(GM_GROUPS):
        sl = slice(gi * GM_CH, (gi + 1) * GM_CH)
        v = gv[:, sl]
        mu = jnp.mean(v, axis=-1, keepdims=True)
        vc = v - mu
        var = jnp.mean(vc * vc, axis=-1, keepdims=True)
        vn = vc * lax.rsqrt(var + LN_EPS) * ng_ref[:, sl]
        if with_vn:
            vn_ref[:, sl] = vn
        vnb = vn.astype(BF16)
        for c in range(ts // chunk):
            rs = slice(c * chunk, (c + 1) * chunk)
            s = jnp.dot(m_ref[gi], vnb[rs], preferred_element_type=F32) + bias_ref[:, sl]
            b_ref[rs, sl] = gu[rs, sl] * s


def _even_mixer(proj, x0, s5p, d_skip, w_glu, b_glu, norm_g, m_mix, bias, *, n_b, n_s, per_block_init, row0=0):
    bw, cw, coef = s5p
    if per_block_init:
        ts, grid, chunk = n_b * n_s, (1, 1), n_b * n_s
        assert n_s == SUBLANES
        n_rb = ts // SUBLANES
        x0_spec = pl.BlockSpec((2 * S5_PAIRS, n_rb, SUBLANES, LANES), lambda b, t: (0, 0, 0, 0))
        st_shape = (2 * S5_PAIRS, n_rb, SUBLANES, LANES)
        st_spec = x0_spec
    else:
        ts = min(512, n_s)
        grid, chunk = (n_b, n_s // ts), GM_CHUNK
        x0_spec = pl.BlockSpec((1, 2 * S5_PAIRS, SUBLANES, LANES), lambda b, t: (b, 0, 0, 0))
        st_shape = (n_b, 2 * S5_PAIRS, SUBLANES, LANES)
        st_spec = x0_spec
    nt = grid[1]
    with_vn = per_block_init
    blk0 = row0 // ts
    assert row0 % ts == 0

    def rows(col):
        return pl.BlockSpec((ts, MIX_HALF), lambda b, t, col=col: (blk0 + b * nt + t, col))

    def full(a):
        return pl.BlockSpec(a.shape, lambda b, t, nd=a.ndim: (0,) * nd)

    row_out = pl.BlockSpec((ts, MIX_HALF), lambda b, t: (b * nt + t, 0))
    rows_shape = jax.ShapeDtypeStruct((n_b * n_s, MIX_HALF), F32)
    n_row_outs = 3 if with_vn else 2
    out_shape = [rows_shape] * n_row_outs + [jax.ShapeDtypeStruct(st_shape, F32)]
    out_specs = [row_out] * n_row_outs + [st_spec]
    weights = (bw, coef, cw, d_skip, w_glu, b_glu, norm_g, m_mix, bias)
    outs = pl.pallas_call(
        functools.partial(_even_kernel, ts=ts, chunk=chunk, per_block_init=per_block_init, with_vn=with_vn),
        grid=grid,
        in_specs=[rows(0), rows(1), rows(2), x0_spec] + [full(w) for w in weights],
        out_specs=out_specs,
        out_shape=out_shape,
        scratch_shapes=[pltpu.VMEM((2 * S5_PAIRS, ts, LANES), F32),
                        pltpu.VMEM((2 * S5_PAIRS, SUBLANES, LANES), F32)],
        compiler_params=_cparams(("arbitrary", "arbitrary")),
    )(proj, proj, proj, x0, *weights)
    if with_vn:
        return outs[0], outs[1], outs[2], outs[3]
    return outs[0], outs[1], None, outs[2]


def _state_to_lanes(x0, n_b):
    re = x0[..., 0].astype(F32).reshape(n_b, S5_PAIRS, LANES)
    im = x0[..., 1].astype(F32).reshape(n_b, S5_PAIRS, LANES)
    return jnp.concatenate([re, im], axis=1)


def _lanes_to_state(st, n_b):
    re = st[:, :S5_PAIRS].reshape(n_b, S5_GROUPS, S5_STATE)
    im = st[:, S5_PAIRS:].reshape(n_b, S5_GROUPS, S5_STATE)
    return jnp.stack([re, im], axis=-1)


def _even_layer_mix(proj, x0, p, i, *, n_b, n_s, sample, row0=0):
    s5p = _s5_params(p['s5_lambda_re'][i], p['s5_lambda_im'][i], p['s5_b_re'][i], p['s5_b_im'][i],
                     p['s5_c_re'][i], p['s5_c_im'][i], p['s5_log_dt'][i])
    st0 = _state_to_lanes(x0, n_b)
    if sample:
        chunk = n_s
        x0k = jnp.broadcast_to(st0.transpose(1, 0, 2)[:, :, None, :], (2 * S5_PAIRS, n_b, SUBLANES, LANES))
        w = jnp.tril(p['gm_w_s'][i][:, :chunk, :chunk])
        m_mix = jnp.einsum('bc,gij->gbicj', jnp.eye(n_b, dtype=F32), w).reshape(GM_GROUPS, n_b * chunk, n_b * chunk)
        bias_rows = jnp.tile(p['gm_b_s'][i][:, :chunk].T, (n_b, 1))
    else:
        x0k = jnp.broadcast_to(st0[:, :, None, :], (n_b, 2 * S5_PAIRS, SUBLANES, LANES))
        m_mix = jnp.tril(p['gm_w_s'][i][:, :GM_CHUNK, :GM_CHUNK])
        bias_rows = p['gm_b_s'][i][:, :GM_CHUNK].T
    bias = jnp.repeat(bias_rows.astype(F32), GM_CH, axis=1)
    a, b, vn, st = _even_mixer(
        proj, x0k, s5p, p['s5_d'][i].reshape(1, MIX_HALF).astype(F32), p['s5_w_glu'][i].astype(BF16),
        p['s5_b_glu'][i].reshape(1, MIX_HALF).astype(F32), p['gm_norm_g'][i].reshape(1, MIX_HALF).astype(F32),
        m_mix.astype(BF16), bias, n_b=n_b, n_s=n_s, per_block_init=sample, row0=row0)
    if sample:
        st = st[:, :, 0, :].transpose(1, 0, 2)
    else:
        st = st[:, :, 0, :]
    return a, b, vn, _lanes_to_state(st, n_b)


POOL_HALO = 16


def _pool_kernel(c_ref, halo_ref, w_ref, scale_ref, o_ref, hist_ref, *, ts, base):
    t = pl.program_id(1)

    @pl.when(t == 0)
    def _():
        hist_ref[0:POOL_HALO, :] = halo_ref[0]

    @pl.when(t > 0)
    def _():
        hist_ref[0:POOL_HALO, :] = hist_ref[ts:ts + POOL_HALO, :]

    hist_ref[POOL_HALO:POOL_HALO + ts, :] = c_ref[...]
    pos = base + t * ts + lax.broadcasted_iota(jnp.int32, (ts, 1), 0)
    for g, win in enumerate(POOL_WINDOWS):
        sl = slice(g * POOL_CH, (g + 1) * POOL_CH)
        x = hist_ref[POOL_HALO:POOL_HALO + ts, sl]
        acc = x
        for d in range(1, win):
            acc = acc + hist_ref[POOL_HALO - d:POOL_HALO - d + ts, sl]
        cnt = jnp.minimum(pos + 1, win).astype(F32)
        pooled = acc / cnt - x
        y = jnp.dot(pooled.astype(BF16), w_ref[g], preferred_element_type=F32)
        o_ref[:, sl] = y * scale_ref[:, sl]


def _pool_mixer(proj, halo, w, scale, *, n_b, n_s, base, row0=0):
    ts = min(512, n_s)
    nt = n_s // ts
    blk0 = row0 // ts
    assert row0 % ts == 0
    return pl.pallas_call(
        functools.partial(_pool_kernel, ts=ts, base=base),
        grid=(n_b, nt),
        in_specs=[pl.BlockSpec((ts, MIX_HALF), lambda b, t: (blk0 + b * nt + t, 0)),
                  pl.BlockSpec((1, POOL_HALO, MIX_HALF), lambda b, t: (b, 0, 0)),
                  pl.BlockSpec(w.shape, lambda b, t: (0, 0, 0)),
                  pl.BlockSpec((1, MIX_HALF), lambda b, t: (0, 0))],
        out_specs=pl.BlockSpec((ts, MIX_HALF), lambda b, t: (b * nt + t, 0)),
        out_shape=jax.ShapeDtypeStruct((n_b * n_s, MIX_HALF), F32),
        scratch_shapes=[pltpu.VMEM((POOL_HALO + ts, MIX_HALF), F32)],
        compiler_params=_cparams(("arbitrary", "arbitrary")),
    )(proj, halo, w, scale)


def _rope_tables(pos):
    half = ROT_DIM // 2
    inv = ROPE_THETA ** (-jnp.arange(half, dtype=F32) * 2.0 / ROT_DIM)
    ang = pos.astype(F32)[:, None] * inv[None, :]
    cos, sin = jnp.cos(ang), jnp.sin(ang)
    n = pos.shape[0]
    one = jnp.ones((n, HEAD_DIM - ROT_DIM), F32)
    zero = jnp.zeros((n, HEAD_DIM - ROT_DIM), F32)
    z8 = jnp.zeros((n, half), F32)
    ca = jnp.concatenate([cos, cos, one], axis=1)
    sp = jnp.concatenate([z8, sin, zero], axis=1)
    sm = jnp.concatenate([-sin, z8, zero], axis=1)
    return tuple(jnp.tile(t, (1, LANES // HEAD_DIM)) for t in (ca, sp, sm))


def _rope_kernel(q_ref, k_ref, v_ref, ca_ref, sp_ref, sm_ref, qo_ref, ko_ref, qb_ref, kb_ref, vb_ref, km_ref):
    ca, sp, sm = ca_ref[...], sp_ref[...], sm_ref[...]
    half = ROT_DIM // 2
    for c in range(MIX_HALF // LANES):
        sl = slice(c * LANES, (c + 1) * LANES)
        for src, dst in ((q_ref, qo_ref), (k_ref, ko_ref)):
            x = src[:, sl]
            dst[:, sl] = x * ca + pltpu.roll(x, half, 1) * sp + pltpu.roll(x, LANES - half, 1) * sm
    q = qo_ref[...]
    k = ko_ref[...]
    qb_ref[...] = (q * (HEAD_DIM ** -0.5)).astype(BF16)
    kb_ref[...] = k.astype(BF16)
    vb_ref[...] = v_ref[...].astype(BF16)
    km_ref[0] = jnp.mean(k, axis=0, keepdims=True)


def _rope(proj, tables):
    t_rows = proj.shape[0]
    ts = MOBA_BLOCK

    def col(c):
        return pl.BlockSpec((ts, MIX_HALF), lambda i, c=c: (i, c))

    tab = pl.BlockSpec((ts, LANES), lambda i: (i, 0))
    row = pl.BlockSpec((ts, MIX_HALF), lambda i: (i, 0))
    f32o = jax.ShapeDtypeStruct((t_rows, MIX_HALF), F32)
    bfo = jax.ShapeDtypeStruct((t_rows, MIX_HALF), BF16)
    outs = pl.pallas_call(
        _rope_kernel,
        grid=(t_rows // ts,),
        in_specs=[col(1), col(2), col(3), tab, tab, tab],
        out_specs=[row, row, row, row, row, pl.BlockSpec((1, 1, MIX_HALF), lambda i: (i, 0, 0))],
        out_shape=[f32o, f32o, bfo, bfo, bfo, jax.ShapeDtypeStruct((t_rows // ts, 1, MIX_HALF), F32)],
        compiler_params=_cparams(("parallel",)),
    )(proj, proj, proj, *tables)
    return list(outs[:5]) + [outs[5].reshape(t_rows // ts, MIX_HALF)]


def _top_rows_mask(gate, n_valid_rows, k_top):
    n = gate.shape[0]
    row = lax.broadcasted_iota(jnp.int32, gate.shape, 0)
    live = row < n_valid_rows
    sel = jnp.zeros(gate.shape, jnp.bool_)
    for _ in range(k_top):
        g = jnp.where(live, gate, NEG_INF)
        mx = jnp.max(g, axis=0, keepdims=True)
        first = jnp.min(jnp.where(live & (g == mx), row, n), axis=0, keepdims=True)
        pick = row == first
        sel = sel | pick
        live = live & jnp.logical_not(pick)
    return sel


def _top_lanes_mask(gate, n_valid, k_top):
    n = gate.shape[1]
    lane = lax.broadcasted_iota(jnp.int32, gate.shape, 1)
    live = lane < n_valid
    sel = jnp.zeros(gate.shape, jnp.bool_)
    for _ in range(k_top):
        g = jnp.where(live, gate, NEG_INF)
        mx = jnp.max(g, axis=1, keepdims=True)
        first = jnp.min(jnp.where(live & (g == mx), lane, n), axis=1, keepdims=True)
        pick = lane == first
        sel = sel | pick
        live = live & jnp.logical_not(pick)
    return sel


HEAD_PAIRS = MIX_HALF // LANES


def _moba_prompt_kernel(q_ref, qb_ref, kb_ref, vb_ref, km_ref, o_ref,
                        qbd_ref, sel_ref, m_ref, l_ref, acc_ref, *, n_blk):
    qi = pl.program_id(1)
    tq = MOBA_BLOCK
    lane = lax.broadcasted_iota(jnp.int32, (tq, LANES), 1)
    krow = lax.broadcasted_iota(jnp.int32, (tq, 2 * tq), 0)
    qcol = lax.broadcasted_iota(jnp.int32, (tq, 2 * tq), 1) % tq
    causal = krow <= qcol
    nt_dims = (((1,), (1,)), ((), ()))
    tn_dims = (((0,), (0,)), ((), ()))
    row0 = pl.multiple_of(qi * tq, tq)

    def attend(pr, r0, mask_of, first):
        ps = slice(pr * LANES, (pr + 1) * LANES)
        kblk = kb_ref[pl.ds(r0, tq), ps]
        vblk = vb_ref[pl.ds(r0, tq), ps]
        for ck in range(2 * tq // LANES):
            cs = slice(ck * LANES, (ck + 1) * LANES)
            hd = ck * LANES // tq
            qs = slice(ck * LANES - hd * tq, (ck + 1) * LANES - hd * tq)
            s = lax.dot_general(kblk, qbd_ref[pr, cs, :], nt_dims, preferred_element_type=F32)
            s = jnp.where(mask_of(cs), s, NEG_INF)
            m_blk = jnp.max(s, axis=0, keepdims=True)
            m_old = m_ref[pr, :, cs]
            m_new = m_blk if first else jnp.maximum(m_old, m_blk)
            p = jnp.exp(s - m_new)
            pv = lax.dot_general(vblk, p.astype(BF16), tn_dims, preferred_element_type=F32)
            pv = pv[hd * HEAD_DIM:(hd + 1) * HEAD_DIM]
            if first:
                l_ref[pr, :, cs] = jnp.sum(p, axis=0, keepdims=True)
                acc_ref[pr, hd, :, qs] = pv
            else:
                alpha = jnp.exp(m_old - m_new)
                l_ref[pr, :, cs] = alpha * l_ref[pr, :, cs] + jnp.sum(p, axis=0, keepdims=True)
                acc_ref[pr, hd, :, qs] = alpha * acc_ref[pr, hd, :, qs] + pv
            m_ref[pr, :, cs] = m_new

    for pr in range(HEAD_PAIRS):
        ps = slice(pr * LANES, (pr + 1) * LANES)
        qf = q_ref[:, ps] * (HEAD_DIM ** -0.5)
        qbd_f = jnp.concatenate([jnp.where(lane < HEAD_DIM, qf, 0.0), jnp.where(lane >= HEAD_DIM, qf, 0.0)], axis=0)
        qb = qb_ref[:, ps]
        zero = jnp.zeros_like(qb)
        qbd_ref[pr] = jnp.concatenate([jnp.where(lane < HEAD_DIM, qb, zero), jnp.where(lane >= HEAD_DIM, qb, zero)],
                                      axis=0)
        gate = lax.dot_general(km_ref[:, ps], qbd_f, nt_dims, precision=lax.Precision.HIGHEST,
                               preferred_element_type=F32)
        sel_ref[pr] = _top_rows_mask(gate, qi, MOBA_TOPK).astype(F32)
        attend(pr, row0, lambda cs: causal[:, cs], True)

    def blk_body(j, _):
        r0 = pl.multiple_of(j * tq, tq)
        for pr in range(HEAD_PAIRS):
            picked = sel_ref[pr, pl.ds(j, 1), :] > 0.5
            attend(pr, r0, lambda cs, picked=picked: picked[:, cs], False)
        return 0

    lax.fori_loop(0, qi, blk_body, 0)
    for pr in range(HEAD_PAIRS):
        l = l_ref[pr]
        out_t = jnp.concatenate([acc_ref[pr, 0] / l[:, :tq], acc_ref[pr, 1] / l[:, tq:]], axis=0)
        o_ref[:, pr * LANES:(pr + 1) * LANES] = out_t.T


def _moba_prompt(q_rot, qb, kb, vb, kmean, *, n_b, n_s):
    n_blk = n_s // MOBA_BLOCK
    tq = MOBA_BLOCK
    qspec = pl.BlockSpec((tq, MIX_HALF), lambda b, i: (b * n_blk + i, 0))
    kvspec = pl.BlockSpec((n_s, MIX_HALF), lambda b, i: (b, 0))
    return pl.pallas_call(
        functools.partial(_moba_prompt_kernel, n_blk=n_blk),
        grid=(n_b, n_blk),
        in_specs=[qspec, qspec, kvspec, kvspec, pl.BlockSpec((n_blk, MIX_HALF), lambda b, i: (b, 0))],
        out_specs=qspec,
        out_shape=jax.ShapeDtypeStruct((n_b * n_s, MIX_HALF), F32),
        scratch_shapes=[pltpu.VMEM((HEAD_PAIRS, 2 * tq, LANES), BF16),
                        pltpu.VMEM((HEAD_PAIRS, n_blk, 2 * tq), F32),
                        pltpu.VMEM((HEAD_PAIRS, 1, 2 * tq), F32),
                        pltpu.VMEM((HEAD_PAIRS, 1, 2 * tq), F32),
                        pltpu.VMEM((HEAD_PAIRS, 2, HEAD_DIM, tq), F32)],
        compiler_params=_cparams(("arbitrary", "arbitrary")),
    )(q_rot, qb, kb, vb, kmean)


PAGES_PER_STEP = 8
BLOCK_PAGES = MOBA_BLOCK // PAGE_SIZE


def _moba_sample_kernel(pt_ref, *refs, n_blk, n_q):
    del pt_ref
    kp = refs[:PAGES_PER_STEP]
    vp = refs[PAGES_PER_STEP:2 * PAGES_PER_STEP]
    qbt_ref, qbtf_ref, kn_ref, vn_ref, o_ref, oacc_ref, m_ref, l_ref, km_ref = refs[2 * PAGES_PER_STEP:]
    c = pl.program_id(1)
    ncol = ATT_HEADS * n_q
    blocks_per_step = PAGES_PER_STEP // BLOCK_PAGES
    nt_dims = (((1,), (1,)), ((), ()))
    qbt = qbt_ref[0]
    lane_c = lax.broadcasted_iota(jnp.int32, (ncol, LANES), 1)
    lane_k = lax.broadcasted_iota(jnp.int32, (MIX_HALF, LANES), 1)

    @pl.when(c == 0)
    def _():
        m_ref[...] = jnp.zeros(m_ref.shape, F32)
        l_ref[...] = jnp.zeros(l_ref.shape, F32)
        km_ref[...] = jnp.zeros(km_ref.shape, F32)

    kt_all = jnp.concatenate([kp[j][0, 0] for j in range(PAGES_PER_STEP)], axis=1)
    s_all = jnp.dot(qbt, kt_all.astype(BF16), preferred_element_type=F32)
    for blk in range(blocks_per_step):
        n = c * blocks_per_step + blk
        ks = slice(blk * MOBA_BLOCK, (blk + 1) * MOBA_BLOCK)
        vt = jnp.concatenate([vp[blk * BLOCK_PAGES + j][0, 0] for j in range(BLOCK_PAGES)], axis=1)
        kmean = jnp.sum(kt_all[:, ks], axis=1, keepdims=True) * (1.0 / MOBA_BLOCK)
        km_ref[...] = jnp.where(lane_k == n, kmean, km_ref[...])
        s = s_all[:, ks]
        m = jnp.max(s, axis=1, keepdims=True)
        p = jnp.exp(s - m)
        m_ref[...] = jnp.where(lane_c == n, m, m_ref[...])
        l_ref[...] = jnp.where(lane_c == n, jnp.sum(p, axis=1, keepdims=True), l_ref[...])
        oacc_ref[n] = lax.dot_general(p.astype(BF16), vt.astype(BF16), nt_dims,
                                      preferred_element_type=F32)

    @pl.when(c == pl.num_programs(1) - 1)
    def _():
        gate = jnp.dot(qbtf_ref[0], km_ref[...], precision=lax.Precision.HIGHEST,
                       preferred_element_type=F32)
        sel = _top_lanes_mask(gate, n_blk, MOBA_TOPK)
        s_own = lax.dot_general(qbt, kn_ref[...].astype(BF16), nt_dims, preferred_element_type=F32)
        qidx = lax.broadcasted_iota(jnp.int32, (ncol, n_q), 0) % n_q
        kidx = lax.broadcasted_iota(jnp.int32, (ncol, n_q), 1)
        s_own = jnp.where(kidx <= qidx, s_own, NEG_INF)
        m_all = m_ref[...]
        m_fin = jnp.maximum(jnp.max(jnp.where(sel, m_all, NEG_INF), axis=1, keepdims=True),
                            jnp.max(s_own, axis=1, keepdims=True))
        w = jnp.where(sel, jnp.exp(m_all - m_fin), 0.0)
        p_own = jnp.exp(s_own - m_fin)
        l_fin = jnp.sum(w * l_ref[...], axis=1, keepdims=True) + jnp.sum(p_own, axis=1, keepdims=True)
        w = w / l_fin
        p_own = p_own / l_fin
        acc = jnp.dot(p_own, vn_ref[...], preferred_element_type=F32)
        for n in range(n_blk):
            acc = acc + w[:, n:n + 1] * oacc_ref[n]
        head = lax.broadcasted_iota(jnp.int32, (n_q, MIX_HALF), 1) // HEAD_DIM
        out = jnp.zeros((n_q, MIX_HALF), F32)
        for h in range(ATT_HEADS):
            out = out + jnp.where(head == h, acc[h * n_q:(h + 1) * n_q], 0.0)
        o_ref[...] = out


def _moba_sample(page_table, cache_k, cache_v, layer_i, q_rot, k_rot, proj, *, n_b, n_q, row0):
    n_pages = page_table.shape[1]
    n_blk = n_pages // BLOCK_PAGES
    ncol = ATT_HEADS * n_q
    assert n_blk <= LANES and n_pages % PAGES_PER_STEP == 0 and row0 % n_q == 0
    blk0 = row0 // n_q
    q4 = (q_rot[row0:row0 + n_b * n_q] * (HEAD_DIM ** -0.5)).reshape(n_b, n_q, ATT_HEADS, HEAD_DIM)
    qbtf = jnp.einsum('bihd,hg->bhigd', q4, jnp.eye(ATT_HEADS, dtype=F32)).reshape(n_b, ncol, MIX_HALF)

    def page_spec(j):
        return pl.BlockSpec((1, 1, MIX_HALF, PAGE_SIZE),
                            lambda b, c, pt, j=j: (layer_i, pt[b, c * PAGES_PER_STEP + j], 0, 0))

    per_b3 = lambda shape: pl.BlockSpec(shape, lambda b, c, pt: (b, 0, 0))
    grid_spec = pltpu.PrefetchScalarGridSpec(
        num_scalar_prefetch=1,
        grid=(n_b, n_pages // PAGES_PER_STEP),
        in_specs=[page_spec(j) for j in range(PAGES_PER_STEP)] * 2
        + [per_b3((1, ncol, MIX_HALF)), per_b3((1, ncol, MIX_HALF)),
           pl.BlockSpec((n_q, MIX_HALF), lambda b, c, pt: (blk0 + b, 0)),
           pl.BlockSpec((n_q, MIX_HALF), lambda b, c, pt: (blk0 + b, 3))],
        out_specs=pl.BlockSpec((n_q, MIX_HALF), lambda b, c, pt: (b, 0)),
        scratch_shapes=[pltpu.VMEM((n_blk, ncol, MIX_HALF), F32),
                        pltpu.VMEM((ncol, LANES), F32), pltpu.VMEM((ncol, LANES), F32),
                        pltpu.VMEM((MIX_HALF, LANES), F32)])
    return pl.pallas_call(
        functools.partial(_moba_sample_kernel, n_blk=n_blk, n_q=n_q),
        grid_spec=grid_spec,
        out_shape=jax.ShapeDtypeStruct((n_b * n_q, MIX_HALF), F32),
        compiler_params=_cparams(("arbitrary", "arbitrary")),
    )(page_table, *([cache_k] * PAGES_PER_STEP), *([cache_v] * PAGES_PER_STEP), qbtf.astype(BF16), qbtf,
      k_rot, proj)


TOK_TILE = 256
SEG_ALIGN = SUBLANES
ASG_TILE = -(-(TOK_TILE * TOPK + N_EXPERTS * (SEG_ALIGN - 1)) // LANES) * LANES
EXP_TILE = 512


def _layer_norm_rows(h, g, b):
    mu = jnp.mean(h, axis=-1, keepdims=True)
    hc = h - mu
    var = jnp.mean(hc * hc, axis=-1, keepdims=True)
    return hc * lax.rsqrt(var + LN_EPS) * g + b


def _route_kernel(x_ref, ap_ref, as_ref, bp_ref, bs_ref, wo_ref, g_ref, bt_ref, wrh_ref, wrl_ref, br_ref,
                  x1_ref, xs_ref, dg_ref, cnt_ref, *, alpha, n_prompt_tiles):
    is_sample = pl.program_id(0) >= n_prompt_tiles
    a = jnp.where(is_sample, as_ref[...], ap_ref[...])
    b = jnp.where(is_sample, bs_ref[...], bp_ref[...])
    h = (alpha * x_ref[...]
         + jnp.dot(a.astype(BF16), wo_ref[0:MIX_HALF, :], preferred_element_type=F32)
         + jnp.dot(b.astype(BF16), wo_ref[MIX_HALF:, :], preferred_element_type=F32))
    x1 = _layer_norm_rows(h, g_ref[...], bt_ref[...])
    x1_ref[...] = x1
    x1h = x1.astype(BF16)
    x1l = (x1 - x1h.astype(F32)).astype(BF16)
    nt_dims = (((1,), (1,)), ((), ()))
    logits = (lax.dot_general(wrh_ref[...], x1h, nt_dims, preferred_element_type=F32)
              + lax.dot_general(wrh_ref[...], x1l, nt_dims, preferred_element_type=F32)
              + lax.dot_general(wrl_ref[...], x1h, nt_dims, preferred_element_type=F32)) + br_ref[...]
    row = lax.broadcasted_iota(jnp.int32, logits.shape, 0)
    g = logits
    picks, vals = [], []
    for _ in range(TOPK):
        mx = jnp.max(g, axis=0, keepdims=True)
        first = jnp.min(jnp.where(g == mx, row, N_EXPERTS), axis=0, keepdims=True)
        pick = row == first
        picks.append(pick)
        vals.append(mx)
        g = jnp.where(pick, NEG_INF, g)
    es = [jnp.exp(v - vals[0]) for v in vals]
    den = es[0] + es[1] + es[2] + es[3]
    onehot = [p.astype(F32) for p in picks]
    member = onehot[0] + onehot[1] + onehot[2] + onehot[3]
    t_r = lax.broadcasted_iota(jnp.int32, (TOK_TILE, TOK_TILE), 0)
    t_c = lax.broadcasted_iota(jnp.int32, (TOK_TILE, TOK_TILE), 1)
    before = (t_r < t_c).astype(BF16)
    rank = jnp.dot(member.astype(BF16), before, preferred_element_type=F32)
    cnt = jnp.sum(member, axis=1, keepdims=True)
    cnt = jnp.ceil(cnt * (1.0 / SEG_ALIGN)) * SEG_ALIGN
    e_r = lax.broadcasted_iota(jnp.int32, (N_EXPERTS, N_EXPERTS), 0)
    e_c = lax.broadcasted_iota(jnp.int32, (N_EXPERTS, N_EXPERTS), 1)
    lower = (e_c < e_r).astype(F32)
    off = jnp.dot(lower, jnp.broadcast_to(cnt, (N_EXPERTS, TOK_TILE)), precision=lax.Precision.HIGHEST,
                  preferred_element_type=F32)
    slot = off + rank
    dests = [jnp.sum(oh * slot, axis=0, keepdims=True) for oh in onehot]
    r_iota = lax.broadcasted_iota(jnp.int32, (ASG_TILE, TOK_TILE), 0)
    perm = jnp.zeros((ASG_TILE, TOK_TILE), F32)
    for d in dests:
        perm = jnp.where(r_iota == d.astype(jnp.int32), 1.0, perm)
    xs_ref[...] = jnp.dot(perm.astype(BF16), x1h, preferred_element_type=F32)
    dg_ref[0] = jnp.concatenate(dests + [e / den for e in es], axis=0)
    cnt_ref[0] = jnp.broadcast_to(cnt, (N_EXPERTS, LANES))


def _route(x, mix_a, mix_b, w_out_bf, ln_g, ln_b, wr_t, br, *, alpha):
    t = x.shape[0]
    nt = t // TOK_TILE
    npt = mix_a[0].shape[0] // TOK_TILE
    assert mix_a[0].shape[0] % TOK_TILE == 0 and mix_a[1].shape[0] == (nt - npt) * TOK_TILE
    full2 = lambda a: pl.BlockSpec(a.shape, lambda i: (0, 0))
    prompt_rows = pl.BlockSpec((TOK_TILE, MIX_HALF), lambda i: (jnp.minimum(i, npt - 1), 0))
    sample_rows = pl.BlockSpec((TOK_TILE, MIX_HALF), lambda i: (jnp.maximum(i - npt, 0), 0))
    wr_hi = wr_t.astype(BF16)
    wr_lo = (wr_t - wr_hi.astype(F32)).astype(BF16)
    return pl.pallas_call(
        functools.partial(_route_kernel, alpha=alpha, n_prompt_tiles=npt),
        grid=(nt,),
        in_specs=[pl.BlockSpec((TOK_TILE, D_MODEL), lambda i: (i, 0)),
                  prompt_rows, sample_rows, prompt_rows, sample_rows,
                  full2(w_out_bf), full2(ln_g), full2(ln_b), full2(wr_hi), full2(wr_lo), full2(br)],
        out_specs=[pl.BlockSpec((TOK_TILE, D_MODEL), lambda i: (i, 0)),
                   pl.BlockSpec((ASG_TILE, D_MODEL), lambda i: (i, 0)),
                   pl.BlockSpec((1, 2 * TOPK, TOK_TILE), lambda i: (i, 0, 0)),
                   pl.BlockSpec((1, N_EXPERTS, LANES), lambda i: (i, 0, 0))],
        out_shape=[jax.ShapeDtypeStruct((t, D_MODEL), F32),
                   jax.ShapeDtypeStruct((nt * ASG_TILE, D_MODEL), F32),
                   jax.ShapeDtypeStruct((nt, 2 * TOPK, TOK_TILE), F32),
                   jax.ShapeDtypeStruct((nt, N_EXPERTS, LANES), F32)],
        compiler_params=_cparams(("parallel",)),
    )(x, mix_a[0], mix_a[1], mix_b[0], mix_b[1], w_out_bf, ln_g, ln_b, wr_hi, wr_lo, br)


SEG_PIECES = tuple(1 << b for b in range(TOK_TILE.bit_length() - 1, SEG_ALIGN.bit_length() - 2, -1))
TILE_PIECES = tuple(1 << b for b in range(EXP_TILE.bit_length() - 1, SEG_ALIGN.bit_length() - 2, -1))
SEG_SMALL = 64


def _expert_kernel(te_ref, nu_ref, lo_ref, hi_ref, valid_ref, src_ref, dst_ref, len_ref, tot_ref,
                   xs_ref, wg_ref, bg_ref, wu_ref, bu_ref, wd_ref, bd_ref, ys_ref,
                   xbuf, ybuf, zbuf, wgb_ref, wub_ref, wdb_ref, gsem, ssem, zsem, *, n_tok_tiles):
    m = pl.program_id(0)
    n_used = nu_ref[0]

    def zero_tail(i, start):
        n = ASG_TILE - tot_ref[i]
        for size in SEG_PIECES:
            @pl.when((n & size) != 0)
            def _():
                row = pl.multiple_of(i * ASG_TILE + tot_ref[i] + (n & ~(2 * size - 1)), SEG_ALIGN)
                cp = pltpu.make_async_copy(zbuf.at[pl.ds(0, size)], ys_ref.at[pl.ds(row, size)], zsem)
                cp.start() if start else cp.wait()

    @pl.when(m == 0)
    def _():
        zbuf[...] = jnp.zeros(zbuf.shape, F32)

    @pl.when((m >= 1) & (m - 1 < n_tok_tiles))
    def _():
        zero_tail(m - 1, False)

    @pl.when(m < n_tok_tiles)
    def _():
        zero_tail(m, True)

    def for_pieces(t, fn):
        row0 = t * EXP_TILE

        def seg_body(s, _):
            start = jnp.maximum(dst_ref[s], row0)
            n = jnp.minimum(dst_ref[s] + len_ref[s], row0 + EXP_TILE) - start
            base_src = src_ref[s] + (start - dst_ref[s])
            base_dst = start - row0
            def pieces(sizes):
                for size in sizes:
                    @pl.when((n & size) != 0)
                    def _():
                        done = n & ~(2 * size - 1)
                        fn(pl.multiple_of(base_src + done, SEG_ALIGN), pl.multiple_of(base_dst + done, SEG_ALIGN),
                           size)

            @pl.when(n >= SEG_SMALL)
            def _():
                pieces([size for size in SEG_PIECES if size >= SEG_SMALL])

            pieces([size for size in SEG_PIECES if size < SEG_SMALL])
            return 0

        lax.fori_loop(lo_ref[t], hi_ref[t], seg_body, 0)

    def wait_rows(n, copy_of):
        for size in TILE_PIECES:
            @pl.when((n & size) != 0)
            def _():
                copy_of(size).wait()

    def gather(t, start):
        slot = t % 2
        if start:
            for_pieces(t, lambda row, r, size: pltpu.make_async_copy(
                xs_ref.at[pl.ds(row, size)], xbuf.at[slot, pl.ds(r, size)], gsem.at[slot]).start())
        else:
            wait_rows(valid_ref[t], lambda size: pltpu.make_async_copy(
                xs_ref.at[pl.ds(0, size)], xbuf.at[slot, pl.ds(0, size)], gsem.at[slot]))

    def scatter(t, start):
        slot = t % 2
        if start:
            for_pieces(t, lambda row, r, size: pltpu.make_async_copy(
                ybuf.at[slot, pl.ds(r, size)], ys_ref.at[pl.ds(row, size)], ssem.at[slot]).start())
        else:
            wait_rows(valid_ref[t], lambda size: pltpu.make_async_copy(
                ybuf.at[slot, pl.ds(0, size)], ys_ref.at[pl.ds(0, size)], ssem.at[slot]))

    @pl.when((m == 0) & (n_used > 0))
    def _():
        gather(0, True)

    @pl.when(m + 1 < n_used)
    def _():
        gather(m + 1, True)

    @pl.when((m >= 2) & (m - 2 < n_used))
    def _():
        scatter(m - 2, False)

    @pl.when(m < n_used)
    def _():
        gather(m, False)
        prev = te_ref[jnp.maximum(m - 1, 0)]

        @pl.when((m == 0) | (te_ref[m] != prev))
        def _():
            wgb_ref[...] = wg_ref[0, 0].astype(BF16)
            wub_ref[...] = wu_ref[0, 0].astype(BF16)
            wdb_ref[...] = wd_ref[0, 0].astype(BF16)

        slot = m % 2
        xb = xbuf[slot].astype(BF16)
        gt = jnp.minimum(jnp.dot(xb, wgb_ref[...], preferred_element_type=F32) + bg_ref[0, 0], SWIGLU_LIMIT)
        up = jnp.clip(jnp.dot(xb, wub_ref[...], preferred_element_type=F32) + bu_ref[0, 0],
                      -SWIGLU_LIMIT, SWIGLU_LIMIT)
        act = gt * _sigmoid(SWIGLU_ALPHA * gt) * (up + 1.0)
        ybuf[slot] = jnp.dot(act.astype(BF16), wdb_ref[...], preferred_element_type=F32) + bd_ref[0, 0]
        scatter(m, True)


def _experts(tile_expert, n_used, seg_lo, seg_hi, tile_valid, seg_src, seg_dst, seg_len, tile_total, xs, layer,
             w_gate, b_gate, w_up, b_up, w_down, b_down):
    n_tiles = tile_expert.shape[0]
    n_tok_tiles = tile_total.shape[0]
    assert n_tiles > n_tok_tiles
    wspec = pl.BlockSpec((1, 1, D_MODEL, D_MODEL), lambda m, te, *_: (layer, te[m], 0, 0))
    bspec = pl.BlockSpec((1, 1, 1, D_MODEL), lambda m, te, *_: (layer, te[m], 0, 0))
    grid_spec = pltpu.PrefetchScalarGridSpec(
        num_scalar_prefetch=9, grid=(n_tiles,),
        in_specs=[pl.BlockSpec(memory_space=pl.ANY), wspec, bspec, wspec, bspec, wspec, bspec],
        out_specs=pl.BlockSpec(memory_space=pl.ANY),
        scratch_shapes=[pltpu.VMEM((2, EXP_TILE, D_MODEL), F32), pltpu.VMEM((2, EXP_TILE, D_MODEL), F32),
                        pltpu.VMEM((TOK_TILE, D_MODEL), F32)]
        + [pltpu.VMEM((D_MODEL, D_MODEL), BF16)] * 3
        + [pltpu.SemaphoreType.DMA((2,)), pltpu.SemaphoreType.DMA((2,)), pltpu.SemaphoreType.DMA(())])
    depth = w_gate.shape[0]
    bshape = (depth, N_EXPERTS, 1, D_MODEL)
    return pl.pallas_call(
        functools.partial(_expert_kernel, n_tok_tiles=n_tok_tiles),
        grid_spec=grid_spec,
        out_shape=jax.ShapeDtypeStruct(xs.shape, F32),
        compiler_params=_cparams(("arbitrary",)),
    )(tile_expert, n_used, seg_lo, seg_hi, tile_valid, seg_src, seg_dst, seg_len, tile_total, xs,
      w_gate, b_gate.reshape(bshape), w_up, b_up.reshape(bshape), w_down, b_down.reshape(bshape))


def _combine_kernel(x1_ref, ys_ref, dg_ref, g_ref, bt_ref, o_ref, *, alpha):
    dg = dg_ref[0]
    r_iota = lax.broadcasted_iota(jnp.int32, (ASG_TILE, TOK_TILE), 0)
    comb = jnp.zeros((ASG_TILE, TOK_TILE), F32)
    for k in range(TOPK):
        comb = jnp.where(r_iota == dg[k:k + 1, :].astype(jnp.int32), dg[TOPK + k:TOPK + k + 1, :], comb)
    ffn = lax.dot_general(comb.astype(BF16), ys_ref[...].astype(BF16), (((0,), (0,)), ((), ())),
                          preferred_element_type=F32)
    o_ref[...] = _layer_norm_rows(alpha * x1_ref[...] + ffn, g_ref[...], bt_ref[...])


def _combine(x1, ys, dg, ln_g, ln_b, *, alpha):
    t = x1.shape[0]
    nt = t // TOK_TILE
    return pl.pallas_call(
        functools.partial(_combine_kernel, alpha=alpha),
        grid=(nt,),
        in_specs=[pl.BlockSpec((TOK_TILE, D_MODEL), lambda i: (i, 0)),
                  pl.BlockSpec((ASG_TILE, D_MODEL), lambda i: (i, 0)),
                  pl.BlockSpec((1, 2 * TOPK, TOK_TILE), lambda i: (i, 0, 0)),
                  pl.BlockSpec((1, D_MODEL), lambda i: (0, 0)),
                  pl.BlockSpec((1, D_MODEL), lambda i: (0, 0))],
        out_specs=pl.BlockSpec((TOK_TILE, D_MODEL), lambda i: (i, 0)),
        out_shape=jax.ShapeDtypeStruct((t, D_MODEL), F32),
        compiler_params=_cparams(("parallel",)),
    )(x1, ys, dg, ln_g, ln_b)


def _channel_mix(x, mix_a, mix_b, p, layer, alpha):
    t = x.shape[0]
    nt = t // TOK_TILE
    x1, xs, dg, cnt = _route(
        x, mix_a, mix_b, p['w_out'][layer].astype(BF16), p['ln_g'][layer, 0].reshape(1, D_MODEL),
        p['ln_b'][layer, 0].reshape(1, D_MODEL), p['router_w'][layer].T.astype(F32),
        p['router_b'][layer].reshape(N_EXPERTS, 1).astype(F32), alpha=alpha)
    cnt = cnt[:, :, 0].astype(jnp.int32)
    local_off = jnp.cumsum(cnt, axis=1) - cnt
    tile_rows = (jnp.arange(nt, dtype=jnp.int32) * ASG_TILE)[:, None] + local_off
    total = jnp.sum(cnt, axis=0)
    padded = (total + EXP_TILE - 1) // EXP_TILE * EXP_TILE
    pend = jnp.cumsum(padded)
    expert_rows = (pend - padded)[None, :] + jnp.cumsum(cnt, axis=0) - cnt
    n_tiles = -(-(nt * (TOK_TILE * TOPK + N_EXPERTS * (SEG_ALIGN - 1))) // EXP_TILE) + N_EXPERTS + 2
    tile_start = (jnp.arange(n_tiles, dtype=jnp.int32) * EXP_TILE)[:, None]
    count_below = lambda a, bound: jnp.sum((a[None, :] < bound).astype(jnp.int32), axis=1)
    tile_expert = jnp.minimum(count_below(pend, tile_start + 1), N_EXPERTS - 1)
    n_used = (pend[-1:] // EXP_TILE).astype(jnp.int32)
    seg_src = tile_rows.T.reshape(-1)
    seg_dst = expert_rows.T.reshape(-1).astype(jnp.int32)
    seg_len = cnt.T.reshape(-1)
    seg_lo = count_below(seg_dst + seg_len, tile_start + 1)
    seg_hi = count_below(seg_dst, tile_start + EXP_TILE)
    expert_end = (pend - padded + total)[tile_expert]
    tile_valid = jnp.clip(expert_end - tile_start[:, 0], 0, EXP_TILE).astype(jnp.int32)
    ys = _experts(tile_expert, n_used, seg_lo, seg_hi, tile_valid, seg_src, seg_dst, seg_len, jnp.sum(cnt, axis=1),
                  xs, layer,
                  p['moe_w_gate'], p['moe_b_gate'], p['moe_w_up'], p['moe_b_up'], p['moe_w_down'], p['moe_b_down'])
    return _combine(x1, ys, dg, p['ln_g'][layer, 1].reshape(1, D_MODEL), p['ln_b'][layer, 1].reshape(1, D_MODEL),
                    alpha=alpha)


def kernel(x_prompt, x_sample, state_s5, state_pool, cache_k, cache_v, page_table, w_in_ab, s5_lambda_re, s5_lambda_im, s5_b_re, s5_b_im, s5_c_re, s5_c_im, s5_d, s5_log_dt, s5_w_glu, s5_b_glu, gm_norm_g, gm_w_s, gm_b_s, w_in_cd, pool_w, pool_scale, w_out, ln_g, ln_b, router_w, router_b, moe_w_gate, moe_b_gate, moe_w_up, moe_b_up, moe_w_down, moe_b_down):
    p = dict(w_in_ab=w_in_ab, s5_lambda_re=s5_lambda_re, s5_lambda_im=s5_lambda_im,
             s5_b_re=s5_b_re, s5_b_im=s5_b_im, s5_c_re=s5_c_re, s5_c_im=s5_c_im, s5_d=s5_d,
             s5_log_dt=s5_log_dt, s5_w_glu=s5_w_glu, s5_b_glu=s5_b_glu, gm_norm_g=gm_norm_g,
             gm_w_s=gm_w_s, gm_b_s=gm_b_s, w_in_cd=w_in_cd, pool_w=pool_w, pool_scale=pool_scale,
             w_out=w_out, ln_g=ln_g, ln_b=ln_b, router_w=router_w, router_b=router_b,
             moe_w_gate=moe_w_gate, moe_b_gate=moe_b_gate, moe_w_up=moe_w_up, moe_b_up=moe_b_up,
             moe_w_down=moe_w_down, moe_b_down=moe_b_down)
    n_bp, n_sp, _ = x_prompt.shape
    n_bs, n_ss, _ = x_sample.shape
    t_p, t_s = n_bp * n_sp, n_bs * n_ss
    depth = w_out.shape[0]
    alpha = (2 * depth) ** 0.25
    past_len = page_table.shape[1] * PAGE_SIZE
    x = jnp.concatenate([x_prompt.reshape(t_p, D_MODEL), x_sample.reshape(t_s, D_MODEL)], axis=0)
    zero_s5 = jnp.zeros((n_bp, S5_GROUPS, S5_STATE, 2), F32)
    pos = jnp.concatenate([jnp.tile(jnp.arange(n_sp), n_bp), jnp.tile(past_len + jnp.arange(n_ss), n_bs)])
    rope_tables = _rope_tables(pos)
    cache_k2 = cache_k.transpose(0, 1, 3, 4, 2).reshape(cache_k.shape[0], cache_k.shape[1], MIX_HALF, PAGE_SIZE)
    cache_v2 = cache_v.transpose(0, 1, 3, 4, 2).reshape(cache_v.shape[0], cache_v.shape[1], MIX_HALF, PAGE_SIZE)
    s5_p, s5_s, gmv_s, pool_p, pool_s, k_p, v_p, k_s, v_s = [], [], [], [], [], [], [], [], []
    for layer in range(depth):
        i = layer // 2
        if layer % 2 == 0:
            proj = _proj(x, w_in_ab[i].astype(BF16), PROJ_TILE)
            a_p, b_p, _, st_p = _even_layer_mix(proj, zero_s5, p, i, n_b=n_bp, n_s=n_sp, sample=False)
            a_s, b_s, vn, st_s = _even_layer_mix(proj, state_s5[i], p, i, n_b=n_bs, n_s=n_ss, sample=True, row0=t_p)
            s5_p.append(st_p)
            s5_s.append(st_s)
            gmv_s.append(vn.reshape(n_bs, n_ss, MIX_HALF))
        else:
            proj = _proj(x, w_in_cd[i].astype(BF16), PROJ_TILE)
            pw = pool_w[i].astype(BF16)
            ps = pool_scale[i].reshape(1, MIX_HALF).astype(F32)
            a_p = _pool_mixer(proj, jnp.zeros((n_bp, POOL_HALO, MIX_HALF), F32), pw, ps, n_b=n_bp, n_s=n_sp, base=0)
            halo = jnp.concatenate([jnp.zeros((n_bs, POOL_HALO - POOL_BUF, MIX_HALF), F32),
                                    state_pool[i].astype(F32)], axis=1)
            a_s = _pool_mixer(proj, halo, pw, ps, n_b=n_bs, n_s=n_ss, base=POOL_BUF, row0=t_p)
            q_rot, k_rot, qb, kb, vb, kmean = _rope(proj, rope_tables)
            b_p = _moba_prompt(q_rot, qb, kb, vb, kmean, n_b=n_bp, n_s=n_sp)
            b_s = _moba_sample(page_table, cache_k2, cache_v2, i, q_rot, k_rot, proj, n_b=n_bs, n_q=n_ss, row0=t_p)
            c_p = proj[:t_p, :MIX_HALF].reshape(n_bp, n_sp, MIX_HALF)
            c_s = proj[t_p:, :MIX_HALF].reshape(n_bs, n_ss, MIX_HALF)
            pool_p.append(c_p[:, -POOL_BUF:])
            pool_s.append(jnp.concatenate([state_pool[i].astype(F32), c_s], axis=1)[:, -POOL_BUF:])
            k_p.append(k_rot[:t_p].reshape(n_bp, n_sp, ATT_HEADS, HEAD_DIM))
            k_s.append(k_rot[t_p:].reshape(n_bs, n_ss, ATT_HEADS, HEAD_DIM))
            v_p.append(proj[:t_p, 3 * MIX_HALF:].reshape(n_bp, n_sp, ATT_HEADS, HEAD_DIM))
            v_s.append(proj[t_p:, 3 * MIX_HALF:].reshape(n_bs, n_ss, ATT_HEADS, HEAD_DIM))
        x = _channel_mix(x, (a_p, a_s), (b_p, b_s), p, layer, alpha)
    return (x[:t_p].reshape(n_bp, n_sp, D_MODEL), x[t_p:].reshape(n_bs, n_ss, D_MODEL),
            jnp.stack(s5_p), jnp.stack(s5_s), jnp.stack(gmv_s), jnp.stack(pool_p), jnp.stack(pool_s),
            jnp.stack(k_p), jnp.stack(v_p), jnp.stack(k_s), jnp.stack(v_s))
```

```python
import functools
import math

import jax
import jax.numpy as jnp
from jax import lax
from jax.experimental import pallas as pl
from jax.experimental.pallas import tpu as pltpu

F32 = jnp.float32
BF16 = jnp.bfloat16

D_MODEL = 1024
MIX_HALF = D_MODEL // 2
S5_GROUP_CH = 16
S5_GROUPS = MIX_HALF // S5_GROUP_CH
S5_STATE = 64
GM_CHUNK = 128
GM_GROUPS = 4
GM_CH = MIX_HALF // GM_GROUPS
POOL_WINDOWS = (2, 4, 8, 16)
POOL_CH = MIX_HALF // len(POOL_WINDOWS)
POOL_BUF = max(POOL_WINDOWS) - 1
ATT_HEADS = 8
HEAD_DIM = MIX_HALF // ATT_HEADS
ROT_DIM = HEAD_DIM // 4
ROPE_THETA = 500000.0
MOBA_BLOCK = 256
MOBA_TOPK = 3
N_EXPERTS = 32
TOPK = 4
SWIGLU_LIMIT = 7.0
SWIGLU_ALPHA = 1.702
LN_EPS = 1e-5
PAGE_SIZE = 128

LANES = 128
SUBLANES = 8
VMEM_LIMIT = 56 * 1024 * 1024

S5_OCT = 4
S5_PAIRS = S5_GROUPS * S5_STATE // LANES
NEG_INF = float("-inf")


def _cparams(sem):
    return pltpu.CompilerParams(dimension_semantics=sem, vmem_limit_bytes=VMEM_LIMIT)


def _gelu(x):
    return 0.5 * x * (1.0 + jnp.tanh(math.sqrt(2.0 / math.pi) * (x + 0.044715 * (x * x * x))))


def _sigmoid(x):
    return 1.0 / (1.0 + jnp.exp(-x))


PROJ_TILE = 640


def _proj_kernel(x_ref, w_ref, o_ref):
    o_ref[...] = jnp.dot(x_ref[...].astype(BF16), w_ref[...], preferred_element_type=F32)


def _proj(x, w_bf16, tm):
    t, k = x.shape
    n = w_bf16.shape[1]
    return pl.pallas_call(
        _proj_kernel,
        grid=(t // tm,),
        in_specs=[pl.BlockSpec((tm, k), lambda i: (i, 0)),
                  pl.BlockSpec((k, n), lambda i: (0, 0))],
        out_specs=pl.BlockSpec((tm, n), lambda i: (i, 0)),
        out_shape=jax.ShapeDtypeStruct((t, n), F32),
        compiler_params=_cparams(("parallel",)),
    )(x, w_bf16)


def _s5_params(lam_re, lam_im, b_re, b_im, c_re, c_im, log_dt):
    dt = jnp.exp(log_dt.astype(F32))[:, None]
    lam = lax.complex(lam_re.astype(F32), lam_im.astype(F32))
    lam_bar = jnp.exp(lam * dt)
    b_bar = ((lam_bar - 1.0) / lam)[..., None] * lax.complex(b_re.astype(F32), b_im.astype(F32))
    eye = jnp.eye(SUBLANES, dtype=F32)
    bb = b_bar.reshape(S5_OCT, 8, S5_STATE, S5_GROUP_CH)

    def bdiag_b(t):
        return jnp.einsum('qgph,gk->qghkp', t, eye).reshape(S5_OCT, 128, 512)

    bw = jnp.concatenate([bdiag_b(bb.real), bdiag_b(bb.imag)], axis=-1).astype(BF16)
    cc_re = c_re.astype(F32).reshape(S5_OCT, 8, S5_GROUP_CH, S5_STATE)
    cc_im = c_im.astype(F32).reshape(S5_OCT, 8, S5_GROUP_CH, S5_STATE)

    def bdiag_c(t):
        return jnp.einsum('qghp,gk->qgpkh', t, eye).reshape(S5_OCT, 512, 128)

    cw = jnp.concatenate([bdiag_c(cc_re), -bdiag_c(cc_im)], axis=1).astype(BF16)
    rows = jnp.arange(SUBLANES)
    planes = []
    for d in (1, 2, 4):
        pw = jnp.exp(lam * dt * float(d)).reshape(S5_PAIRS, 1, LANES)
        m = (rows >= d).astype(F32)[None, :, None]
        planes += [pw.real * m, pw.imag * m]
    pw = jnp.exp((lam * dt).reshape(S5_PAIRS, 1, LANES) * (rows + 1).astype(F32)[None, :, None])
    planes += [pw.real, pw.imag]
    coef = jnp.stack(planes, axis=1).astype(F32)
    return bw, cw, coef


def _even_kernel(u_ref, gu_ref, gv_ref, x0_ref, bw_ref, coef_ref, cw_ref, d_ref, wglu_ref, bglu_ref,
                 ng_ref, m_ref, bias_ref, *rest, ts, chunk, per_block_init, with_vn):
    if with_vn:
        a_ref, b_ref, vn_ref, st_out_ref, st_ref, carry_ref = rest
    else:
        a_ref, b_ref, st_out_ref, st_ref, carry_ref = rest
        vn_ref = None
    n_rb = ts // SUBLANES

    if not per_block_init:
        @pl.when(pl.program_id(1) == 0)
        def _():
            carry_ref[...] = x0_ref[0]

    u = u_ref[...]
    ub = u.astype(BF16)
    for q in range(S5_OCT):
        bu = jnp.dot(ub[:, q * 128:(q + 1) * 128], bw_ref[q], preferred_element_type=F32)
        for c in range(4):
            st_ref[q * 4 + c] = bu[:, c * 128:(c + 1) * 128]
            st_ref[S5_PAIRS + q * 4 + c] = bu[:, 512 + c * 128:512 + (c + 1) * 128]

    def pair_body(j, _):
        cf = coef_ref[j]
        a1r, a1i, a2r, a2i, a4r, a4i, pr, pi = [cf[k] for k in range(8)]

        def rb_body(r, carry):
            cr, ci = carry
            row = pl.multiple_of(r * SUBLANES, SUBLANES)
            xr = st_ref[j, pl.ds(row, SUBLANES), :]
            xi = st_ref[S5_PAIRS + j, pl.ds(row, SUBLANES), :]
            for d, ar, ai in ((1, a1r, a1i), (2, a2r, a2i), (4, a4r, a4i)):
                sr = pltpu.roll(xr, d, 0)
                si = pltpu.roll(xi, d, 0)
                xr, xi = xr + ar * sr - ai * si, xi + ar * si + ai * sr
            if per_block_init:
                cr = x0_ref[j, r]
                ci = x0_ref[S5_PAIRS + j, r]
            xr, xi = xr + pr * cr - pi * ci, xi + pr * ci + pi * cr
            st_ref[j, pl.ds(row, SUBLANES), :] = xr
            st_ref[S5_PAIRS + j, pl.ds(row, SUBLANES), :] = xi
            ncr = jnp.broadcast_to(xr[SUBLANES - 1:SUBLANES, :], (SUBLANES, LANES))
            nci = jnp.broadcast_to(xi[SUBLANES - 1:SUBLANES, :], (SUBLANES, LANES))
            if per_block_init:
                st_out_ref[j, r] = ncr
                st_out_ref[S5_PAIRS + j, r] = nci
            return ncr, nci

        cr, ci = lax.fori_loop(0, n_rb, rb_body, (carry_ref[j], carry_ref[S5_PAIRS + j]),
                               unroll=min(4, n_rb))
        carry_ref[j] = cr
        carry_ref[S5_PAIRS + j] = ci
        return 0

    lax.fori_loop(0, S5_PAIRS, pair_body, 0)

    if not per_block_init:
        @pl.when(pl.program_id(1) == pl.num_programs(1) - 1)
        def _():
            st_out_ref[0] = carry_ref[...]

    ys = []
    for q in range(S5_OCT):
        xq = jnp.concatenate([st_ref[q * 4 + c] for c in range(4)]
                             + [st_ref[S5_PAIRS + q * 4 + c] for c in range(4)], axis=-1)
        ys.append(jnp.dot(xq.astype(BF16), cw_ref[q], preferred_element_type=F32))
    y = jnp.concatenate(ys, axis=-1) + d_ref[...] * u
    g = _gelu(y)
    z = jnp.dot(g.astype(BF16), wglu_ref[...], preferred_element_type=F32) + bglu_ref[...]
    a_ref[...] = g * _sigmoid(z)

    gu = _gelu(gu_ref[...])
    gv = _gelu(gv_ref[...])
    for gi in range(GM_GROUPS):
        sl = slice(gi * GM_CH, (gi + 1) * GM_CH)
        v = gv[:, sl]
        mu = jnp.mean(v, axis=-1, keepdims=True)
        vc = v - mu
        var = jnp.mean(vc * vc, axis=-1, keepdims=True)
        vn = vc * lax.rsqrt(var + LN_EPS) * ng_ref[:, sl]
        if with_vn:
            vn_ref[:, sl] = vn
        vnb = vn.astype(BF16)
        for c in range(ts // chunk):
            rs = slice(c * chunk, (c + 1) * chunk)
            s = jnp.dot(m_ref[gi], vnb[rs], preferred_element_type=F32) + bias_ref[:, sl]
            b_ref[rs, sl] = gu[rs, sl] * s


def _even_mixer(proj, x0, s5p, d_skip, w_glu, b_glu, norm_g, m_mix, bias, *, n_b, n_s, per_block_init, row0=0):
    bw, cw, coef = s5p
    if per_block_init:
        ts, grid, chunk = n_b * n_s, (1, 1), n_b * n_s
        assert n_s == SUBLANES
        n_rb = ts // SUBLANES
        x0_spec = pl.BlockSpec((2 * S5_PAIRS, n_rb, SUBLANES, LANES), lambda b, t: (0, 0, 0, 0))
        st_shape = (2 * S5_PAIRS, n_rb, SUBLANES, LANES)
        st_spec = x0_spec
    else:
        ts = min(512, n_s)
        grid, chunk = (n_b, n_s // ts), GM_CHUNK
        x0_spec = pl.BlockSpec((1, 2 * S5_PAIRS, SUBLANES, LANES), lambda b, t: (b, 0, 0, 0))
        st_shape = (n_b, 2 * S5_PAIRS, SUBLANES, LANES)
        st_spec = x0_spec
    nt = grid[1]
    with_vn = per_block_init
    blk0 = row0 // ts
    assert row0 % ts == 0

    def rows(col):
        return pl.BlockSpec((ts, MIX_HALF), lambda b, t, col=col: (blk0 + b * nt + t, col))

    def full(a):
        return pl.BlockSpec(a.shape, lambda b, t, nd=a.ndim: (0,) * nd)

    row_out = pl.BlockSpec((ts, MIX_HALF), lambda b, t: (b * nt + t, 0))
    rows_shape = jax.ShapeDtypeStruct((n_b * n_s, MIX_HALF), F32)
    n_row_outs = 3 if with_vn else 2
    out_shape = [rows_shape] * n_row_outs + [jax.ShapeDtypeStruct(st_shape, F32)]
    out_specs = [row_out] * n_row_outs + [st_spec]
    weights = (bw, coef, cw, d_skip, w_glu, b_glu, norm_g, m_mix, bias)
    outs = pl.pallas_call(
        functools.partial(_even_kernel, ts=ts, chunk=chunk, per_block_init=per_block_init, with_vn=with_vn),
        grid=grid,
        in_specs=[rows(0), rows(1), rows(2), x0_spec] + [full(w) for w in weights],
        out_specs=out_specs,
        out_shape=out_shape,
        scratch_shapes=[pltpu.VMEM((2 * S5_PAIRS, ts, LANES), F32),
                        pltpu.VMEM((2 * S5_PAIRS, SUBLANES, LANES), F32)],
        compiler_params=_cparams(("arbitrary", "arbitrary")),
    )(proj, proj, proj, x0, *weights)
    if with_vn:
        return outs[0], outs[1], outs[2], outs[3]
    return outs[0], outs[1], None, outs[2]


def _state_to_lanes(x0, n_b):
    re = x0[..., 0].astype(F32).reshape(n_b, S5_PAIRS, LANES)
    im = x0[..., 1].astype(F32).reshape(n_b, S5_PAIRS, LANES)
    return jnp.concatenate([re, im], axis=1)


def _lanes_to_state(st, n_b):
    re = st[:, :S5_PAIRS].reshape(n_b, S5_GROUPS, S5_STATE)
    im = st[:, S5_PAIRS:].reshape(n_b, S5_GROUPS, S5_STATE)
    return jnp.stack([re, im], axis=-1)


def _even_layer_mix(proj, x0, p, i, *, n_b, n_s, sample, row0=0):
    s5p = _s5_params(p['s5_lambda_re'][i], p['s5_lambda_im'][i], p['s5_b_re'][i], p['s5_b_im'][i],
                     p['s5_c_re'][i], p['s5_c_im'][i], p['s5_log_dt'][i])
    st0 = _state_to_lanes(x0, n_b)
    if sample:
        chunk = n_s
        x0k = jnp.broadcast_to(st0.transpose(1, 0, 2)[:, :, None, :], (2 * S5_PAIRS, n_b, SUBLANES, LANES))
        w = jnp.tril(p['gm_w_s'][i][:, :chunk, :chunk])
        m_mix = jnp.einsum('bc,gij->gbicj', jnp.eye(n_b, dtype=F32), w).reshape(GM_GROUPS, n_b * chunk, n_b * chunk)
        bias_rows = jnp.tile(p['gm_b_s'][i][:, :chunk].T, (n_b, 1))
    else:
        x0k = jnp.broadcast_to(st0[:, :, None, :], (n_b, 2 * S5_PAIRS, SUBLANES, LANES))
        m_mix = jnp.tril(p['gm_w_s'][i][:, :GM_CHUNK, :GM_CHUNK])
        bias_rows = p['gm_b_s'][i][:, :GM_CHUNK].T
    bias = jnp.repeat(bias_rows.astype(F32), GM_CH, axis=1)
    a, b, vn, st = _even_mixer(
        proj, x0k, s5p, p['s5_d'][i].reshape(1, MIX_HALF).astype(F32), p['s5_w_glu'][i].astype(BF16),
        p['s5_b_glu'][i].reshape(1, MIX_HALF).astype(F32), p['gm_norm_g'][i].reshape(1, MIX_HALF).astype(F32),
        m_mix.astype(BF16), bias, n_b=n_b, n_s=n_s, per_block_init=sample, row0=row0)
    if sample:
        st = st[:, :, 0, :].transpose(1, 0, 2)
    else:
        st = st[:, :, 0, :]
    return a, b, vn, _lanes_to_state(st, n_b)


POOL_HALO = 16


def _pool_kernel(c_ref, halo_ref, w_ref, scale_ref, o_ref, hist_ref, *, ts, base):
    t = pl.program_id(1)

    @pl.when(t == 0)
    def _():
        hist_ref[0:POOL_HALO, :] = halo_ref[0]

    @pl.when(t > 0)
    def _():
        hist_ref[0:POOL_HALO, :] = hist_ref[ts:ts + POOL_HALO, :]

    hist_ref[POOL_HALO:POOL_HALO + ts, :] = c_ref[...]
    pos = base + t * ts + lax.broadcasted_iota(jnp.int32, (ts, 1), 0)
    for g, win in enumerate(POOL_WINDOWS):
        sl = slice(g * POOL_CH, (g + 1) * POOL_CH)
        x = hist_ref[POOL_HALO:POOL_HALO + ts, sl]
        acc = x
        for d in range(1, win):
            acc = acc + hist_ref[POOL_HALO - d:POOL_HALO - d + ts, sl]
        cnt = jnp.minimum(pos + 1, win).astype(F32)
        pooled = acc / cnt - x
        y = jnp.dot(pooled.astype(BF16), w_ref[g], preferred_element_type=F32)
        o_ref[:, sl] = y * scale_ref[:, sl]


def _pool_mixer(proj, halo, w, scale, *, n_b, n_s, base, row0=0):
    ts = min(512, n_s)
    nt = n_s // ts
    blk0 = row0 // ts
    assert row0 % ts == 0
    return pl.pallas_call(
        functools.partial(_pool_kernel, ts=ts, base=base),
        grid=(n_b, nt),
        in_specs=[pl.BlockSpec((ts, MIX_HALF), lambda b, t: (blk0 + b * nt + t, 0)),
                  pl.BlockSpec((1, POOL_HALO, MIX_HALF), lambda b, t: (b, 0, 0)),
                  pl.BlockSpec(w.shape, lambda b, t: (0, 0, 0)),
                  pl.BlockSpec((1, MIX_HALF), lambda b, t: (0, 0))],
        out_specs=pl.BlockSpec((ts, MIX_HALF), lambda b, t: (b * nt + t, 0)),
        out_shape=jax.ShapeDtypeStruct((n_b * n_s, MIX_HALF), F32),
        scratch_shapes=[pltpu.VMEM((POOL_HALO + ts, MIX_HALF), F32)],
        compiler_params=_cparams(("arbitrary", "arbitrary")),
    )(proj, halo, w, scale)


def _rope_tables(pos):
    half = ROT_DIM // 2
    inv = ROPE_THETA ** (-jnp.arange(half, dtype=F32) * 2.0 / ROT_DIM)
    ang = pos.astype(F32)[:, None] * inv[None, :]
    cos, sin = jnp.cos(ang), jnp.sin(ang)
    n = pos.shape[0]
    one = jnp.ones((n, HEAD_DIM - ROT_DIM), F32)
    zero = jnp.zeros((n, HEAD_DIM - ROT_DIM), F32)
    z8 = jnp.zeros((n, half), F32)
    ca = jnp.concatenate([cos, cos, one], axis=1)
    sp = jnp.concatenate([z8, sin, zero], axis=1)
    sm = jnp.concatenate([-sin, z8, zero], axis=1)
    return tuple(jnp.tile(t, (1, LANES // HEAD_DIM)) for t in (ca, sp, sm))


def _rope_kernel(q_ref, k_ref, v_ref, ca_ref, sp_ref, sm_ref, qo_ref, ko_ref, qb_ref, kb_ref, vb_ref, km_ref):
    ca, sp, sm = ca_ref[...], sp_ref[...], sm_ref[...]
    half = ROT_DIM // 2
    for c in range(MIX_HALF // LANES):
        sl = slice(c * LANES, (c + 1) * LANES)
        for src, dst in ((q_ref, qo_ref), (k_ref, ko_ref)):
            x = src[:, sl]
            dst[:, sl] = x * ca + pltpu.roll(x, half, 1) * sp + pltpu.roll(x, LANES - half, 1) * sm
    q = qo_ref[...]
    k = ko_ref[...]
    qb_ref[...] = (q * (HEAD_DIM ** -0.5)).astype(BF16)
    kb_ref[...] = k.astype(BF16)
    vb_ref[...] = v_ref[...].astype(BF16)
    km_ref[0] = jnp.mean(k, axis=0, keepdims=True)


def _rope(proj, tables):
    t_rows = proj.shape[0]
    ts = MOBA_BLOCK

    def col(c):
        return pl.BlockSpec((ts, MIX_HALF), lambda i, c=c: (i, c))

    tab = pl.BlockSpec((ts, LANES), lambda i: (i, 0))
    row = pl.BlockSpec((ts, MIX_HALF), lambda i: (i, 0))
    f32o = jax.ShapeDtypeStruct((t_rows, MIX_HALF), F32)
    bfo = jax.ShapeDtypeStruct((t_rows, MIX_HALF), BF16)
    outs = pl.pallas_call(
        _rope_kernel,
        grid=(t_rows // ts,),
        in_specs=[col(1), col(2), col(3), tab, tab, tab],
        out_specs=[row, row, row, row, row, pl.BlockSpec((1, 1, MIX_HALF), lambda i: (i, 0, 0))],
        out_shape=[f32o, f32o, bfo, bfo, bfo, jax.ShapeDtypeStruct((t_rows // ts, 1, MIX_HALF), F32)],
        compiler_params=_cparams(("parallel",)),
    )(proj, proj, proj, *tables)
    return list(outs[:5]) + [outs[5].reshape(t_rows // ts, MIX_HALF)]


def _top_rows_mask(gate, n_valid_rows, k_top):
    n = gate.shape[0]
    row = lax.broadcasted_iota(jnp.int32, gate.shape, 0)
    live = row < n_valid_rows
    sel = jnp.zeros(gate.shape, jnp.bool_)
    for _ in range(k_top):
        g = jnp.where(live, gate, NEG_INF)
        mx = jnp.max(g, axis=0, keepdims=True)
        first = jnp.min(jnp.where(live & (g == mx), row, n), axis=0, keepdims=True)
        pick = row == first
        sel = sel | pick
        live = live & jnp.logical_not(pick)
    return sel


def _top_lanes_mask(gate, n_valid, k_top):
    n = gate.shape[1]
    lane = lax.broadcasted_iota(jnp.int32, gate.shape, 1)
    live = lane < n_valid
    sel = jnp.zeros(gate.shape, jnp.bool_)
    for _ in range(k_top):
        g = jnp.where(live, gate, NEG_INF)
        mx = jnp.max(g, axis=1, keepdims=True)
        first = jnp.min(jnp.where(live & (g == mx), lane, n), axis=1, keepdims=True)
        pick = lane == first
        sel = sel | pick
        live = live & jnp.logical_not(pick)
    return sel


HEAD_PAIRS = MIX_HALF // LANES


def _moba_prompt_kernel(q_ref, qb_ref, kb_ref, vb_ref, km_ref, o_ref,
                        qbd_ref, sel_ref, m_ref, l_ref, acc_ref, *, n_blk):
    qi = pl.program_id(1)
    tq = MOBA_BLOCK
    lane = lax.broadcasted_iota(jnp.int32, (tq, LANES), 1)
    krow = lax.broadcasted_iota(jnp.int32, (tq, 2 * tq), 0)
    qcol = lax.broadcasted_iota(jnp.int32, (tq, 2 * tq), 1) % tq
    causal = krow <= qcol
    nt_dims = (((1,), (1,)), ((), ()))
    tn_dims = (((0,), (0,)), ((), ()))
    row0 = pl.multiple_of(qi * tq, tq)

    def attend(pr, r0, mask_of, first):
        ps = slice(pr * LANES, (pr + 1) * LANES)
        kblk = kb_ref[pl.ds(r0, tq), ps]
        vblk = vb_ref[pl.ds(r0, tq), ps]
        for ck in range(2 * tq // LANES):
            cs = slice(ck * LANES, (ck + 1) * LANES)
            hd = ck * LANES // tq
            qs = slice(ck * LANES - hd * tq, (ck + 1) * LANES - hd * tq)
            s = lax.dot_general(kblk, qbd_ref[pr, cs, :], nt_dims, preferred_element_type=F32)
            s = jnp.where(mask_of(cs), s, NEG_INF)
            m_blk = jnp.max(s, axis=0, keepdims=True)
            m_old = m_ref[pr, :, cs]
            m_new = m_blk if first else jnp.maximum(m_old, m_blk)
            p = jnp.exp(s - m_new)
            pv = lax.dot_general(vblk, p.astype(BF16), tn_dims, preferred_element_type=F32)
            pv = pv[hd * HEAD_DIM:(hd + 1) * HEAD_DIM]
            if first:
                l_ref[pr, :, cs] = jnp.sum(p, axis=0, keepdims=True)
                acc_ref[pr, hd, :, qs] = pv
            else:
                alpha = jnp.exp(m_old - m_new)
                l_ref[pr, :, cs] = alpha * l_ref[pr, :, cs] + jnp.sum(p, axis=0, keepdims=True)
                acc_ref[pr, hd, :, qs] = alpha * acc_ref[pr, hd, :, qs] + pv
            m_ref[pr, :, cs] = m_new

    for pr in range(HEAD_PAIRS):
        ps = slice(pr * LANES, (pr + 1) * LANES)
        qb = qb_ref[:, ps]
        zero = jnp.zeros_like(qb)

        def block_diag(t):
            return jnp.concatenate([jnp.where(lane < HEAD_DIM, t, zero), jnp.where(lane >= HEAD_DIM, t, zero)], axis=0)

        qbd_ref[pr] = block_diag(qb)
        q_lo = (q_ref[:, ps] * (HEAD_DIM ** -0.5) - qb.astype(F32)).astype(BF16)
        km = km_ref[:, ps]
        km_hi = km.astype(BF16)
        km_lo = (km - km_hi.astype(F32)).astype(BF16)
        gate = (lax.dot_general(km_hi, qbd_ref[pr], nt_dims, preferred_element_type=F32)
                + lax.dot_general(km_lo, qbd_ref[pr], nt_dims, preferred_element_type=F32)
                + lax.dot_general(km_hi, block_diag(q_lo), nt_dims, preferred_element_type=F32))
        sel_ref[pr] = _top_rows_mask(gate, qi, MOBA_TOPK).astype(F32)
        attend(pr, row0, lambda cs: causal[:, cs], True)

    def blk_body(j, _):
        r0 = pl.multiple_of(j * tq, tq)
        for pr in range(HEAD_PAIRS):
            picked = sel_ref[pr, pl.ds(j, 1), :] > 0.5
            attend(pr, r0, lambda cs, picked=picked: picked[:, cs], False)
        return 0

    lax.fori_loop(0, qi, blk_body, 0)
    for pr in range(HEAD_PAIRS):
        l = l_ref[pr]
        out_t = jnp.concatenate([acc_ref[pr, 0] / l[:, :tq], acc_ref[pr, 1] / l[:, tq:]], axis=0)
        o_ref[:, pr * LANES:(pr + 1) * LANES] = out_t.T


def _moba_prompt(q_rot, qb, kb, vb, kmean, *, n_b, n_s):
    n_blk = n_s // MOBA_BLOCK
    tq = MOBA_BLOCK
    qspec = pl.BlockSpec((tq, MIX_HALF), lambda b, i: (b * n_blk + i, 0))
    kvspec = pl.BlockSpec((n_s, MIX_HALF), lambda b, i: (b, 0))
    return pl.pallas_call(
        functools.partial(_moba_prompt_kernel, n_blk=n_blk),
        grid=(n_b, n_blk),
        in_specs=[qspec, qspec, kvspec, kvspec, pl.BlockSpec((n_blk, MIX_HALF), lambda b, i: (b, 0))],
        out_specs=qspec,
        out_shape=jax.ShapeDtypeStruct((n_b * n_s, MIX_HALF), F32),
        scratch_shapes=[pltpu.VMEM((HEAD_PAIRS, 2 * tq, LANES), BF16),
                        pltpu.VMEM((HEAD_PAIRS, n_blk, 2 * tq), F32),
                        pltpu.VMEM((HEAD_PAIRS, 1, 2 * tq), F32),
                        pltpu.VMEM((HEAD_PAIRS, 1, 2 * tq), F32),
                        pltpu.VMEM((HEAD_PAIRS, 2, HEAD_DIM, tq), F32)],
        compiler_params=_cparams(("arbitrary", "arbitrary")),
    )(q_rot, qb, kb, vb, kmean)


PAGES_PER_STEP = 16
BLOCK_PAGES = MOBA_BLOCK // PAGE_SIZE


def _moba_sample_kernel(pt_ref, *refs, n_blk, n_q):
    del pt_ref
    kp = refs[:PAGES_PER_STEP]
    vp = refs[PAGES_PER_STEP:2 * PAGES_PER_STEP]
    qbt_ref, qbtf_ref, kn_ref, vn_ref, o_ref, oacc_ref, m_ref, l_ref, km_ref = refs[2 * PAGES_PER_STEP:]
    c = pl.program_id(1)
    ncol = ATT_HEADS * n_q
    blocks_per_step = PAGES_PER_STEP // BLOCK_PAGES
    nt_dims = (((1,), (1,)), ((), ()))
    qbt = qbt_ref[0]
    lane_c = lax.broadcasted_iota(jnp.int32, (ncol, LANES), 1)
    lane_k = lax.broadcasted_iota(jnp.int32, (MIX_HALF, LANES), 1)

    @pl.when(c == 0)
    def _():
        m_ref[...] = jnp.zeros(m_ref.shape, F32)
        l_ref[...] = jnp.zeros(l_ref.shape, F32)
        km_ref[...] = jnp.zeros(km_ref.shape, F32)

    kt_all = jnp.concatenate([kp[j][0, 0] for j in range(PAGES_PER_STEP)], axis=1)
    s_all = jnp.dot(qbt, kt_all.astype(BF16), preferred_element_type=F32)
    for blk in range(blocks_per_step):
        n = c * blocks_per_step + blk
        ks = slice(blk * MOBA_BLOCK, (blk + 1) * MOBA_BLOCK)
        vt = jnp.concatenate([vp[blk * BLOCK_PAGES + j][0, 0] for j in range(BLOCK_PAGES)], axis=1)
        kmean = jnp.sum(kt_all[:, ks], axis=1, keepdims=True) * (1.0 / MOBA_BLOCK)
        km_ref[...] = jnp.where(lane_k == n, kmean, km_ref[...])
        s = s_all[:, ks]
        m = jnp.max(s, axis=1, keepdims=True)
        p = jnp.exp(s - m)
        m_ref[...] = jnp.where(lane_c == n, m, m_ref[...])
        l_ref[...] = jnp.where(lane_c == n, jnp.sum(p, axis=1, keepdims=True), l_ref[...])
        oacc_ref[n] = lax.dot_general(p.astype(BF16), vt.astype(BF16), nt_dims,
                                      preferred_element_type=F32)

    @pl.when(c == pl.num_programs(1) - 1)
    def _():
        gate = jnp.dot(qbtf_ref[0], km_ref[...], precision=lax.Precision.HIGHEST,
                       preferred_element_type=F32)
        sel = _top_lanes_mask(gate, n_blk, MOBA_TOPK)
        s_own = lax.dot_general(qbt, kn_ref[...].astype(BF16), nt_dims, preferred_element_type=F32)
        qidx = lax.broadcasted_iota(jnp.int32, (ncol, n_q), 0) % n_q
        kidx = lax.broadcasted_iota(jnp.int32, (ncol, n_q), 1)
        s_own = jnp.where(kidx <= qidx, s_own, NEG_INF)
        m_all = m_ref[...]
        m_fin = jnp.maximum(jnp.max(jnp.where(sel, m_all, NEG_INF), axis=1, keepdims=True),
                            jnp.max(s_own, axis=1, keepdims=True))
        w = jnp.where(sel, jnp.exp(m_all - m_fin), 0.0)
        p_own = jnp.exp(s_own - m_fin)
        l_fin = jnp.sum(w * l_ref[...], axis=1, keepdims=True) + jnp.sum(p_own, axis=1, keepdims=True)
        w = w / l_fin
        p_own = p_own / l_fin
        acc = jnp.dot(p_own, vn_ref[...], preferred_element_type=F32)
        for n in range(n_blk):
            acc = acc + w[:, n:n + 1] * oacc_ref[n]
        head = lax.broadcasted_iota(jnp.int32, (n_q, MIX_HALF), 1) // HEAD_DIM
        out = jnp.zeros((n_q, MIX_HALF), F32)
        for h in range(ATT_HEADS):
            out = out + jnp.where(head == h, acc[h * n_q:(h + 1) * n_q], 0.0)
        o_ref[...] = out


def _moba_sample(page_table, cache_k, cache_v, layer_i, q_rot, k_rot, proj, *, n_b, n_q, row0):
    n_pages = page_table.shape[1]
    n_blk = n_pages // BLOCK_PAGES
    ncol = ATT_HEADS * n_q
    assert n_blk <= LANES and n_pages % PAGES_PER_STEP == 0 and row0 % n_q == 0
    blk0 = row0 // n_q
    q4 = (q_rot[row0:row0 + n_b * n_q] * (HEAD_DIM ** -0.5)).reshape(n_b, n_q, ATT_HEADS, HEAD_DIM)
    qbtf = jnp.einsum('bihd,hg->bhigd', q4, jnp.eye(ATT_HEADS, dtype=F32)).reshape(n_b, ncol, MIX_HALF)

    def page_spec(j):
        return pl.BlockSpec((1, 1, MIX_HALF, PAGE_SIZE),
                            lambda b, c, pt, j=j: (layer_i, pt[b, c * PAGES_PER_STEP + j], 0, 0))

    per_b3 = lambda shape: pl.BlockSpec(shape, lambda b, c, pt: (b, 0, 0))
    grid_spec = pltpu.PrefetchScalarGridSpec(
        num_scalar_prefetch=1,
        grid=(n_b, n_pages // PAGES_PER_STEP),
        in_specs=[page_spec(j) for j in range(PAGES_PER_STEP)] * 2
        + [per_b3((1, ncol, MIX_HALF)), per_b3((1, ncol, MIX_HALF)),
           pl.BlockSpec((n_q, MIX_HALF), lambda b, c, pt: (blk0 + b, 0)),
           pl.BlockSpec((n_q, MIX_HALF), lambda b, c, pt: (blk0 + b, 3))],
        out_specs=pl.BlockSpec((n_q, MIX_HALF), lambda b, c, pt: (b, 0)),
        scratch_shapes=[pltpu.VMEM((n_blk, ncol, MIX_HALF), F32),
                        pltpu.VMEM((ncol, LANES), F32), pltpu.VMEM((ncol, LANES), F32),
                        pltpu.VMEM((MIX_HALF, LANES), F32)])
    return pl.pallas_call(
        functools.partial(_moba_sample_kernel, n_blk=n_blk, n_q=n_q),
        grid_spec=grid_spec,
        out_shape=jax.ShapeDtypeStruct((n_b * n_q, MIX_HALF), F32),
        compiler_params=_cparams(("arbitrary", "arbitrary")),
    )(page_table, *([cache_k] * PAGES_PER_STEP), *([cache_v] * PAGES_PER_STEP), qbtf.astype(BF16), qbtf,
      k_rot, proj)


TOK_TILE = 256
SEG_ALIGN = SUBLANES
ASG_TILE = -(-(TOK_TILE * TOPK + N_EXPERTS * (SEG_ALIGN - 1)) // LANES) * LANES
EXP_TILE = 512


def _layer_norm_rows(h, g, b):
    mu = jnp.mean(h, axis=-1, keepdims=True)
    hc = h - mu
    var = jnp.mean(hc * hc, axis=-1, keepdims=True)
    return hc * lax.rsqrt(var + LN_EPS) * g + b


def _route_kernel(x_ref, ap_ref, as_ref, bp_ref, bs_ref, wo_ref, g_ref, bt_ref, wrh_ref, wrl_ref, br_ref,
                  x1_ref, xs_ref, dg_ref, cnt_ref, *, alpha, n_prompt_tiles):
    is_sample = pl.program_id(0) >= n_prompt_tiles
    a = jnp.where(is_sample, as_ref[...], ap_ref[...])
    b = jnp.where(is_sample, bs_ref[...], bp_ref[...])
    h = (alpha * x_ref[...]
         + jnp.dot(a.astype(BF16), wo_ref[0:MIX_HALF, :], preferred_element_type=F32)
         + jnp.dot(b.astype(BF16), wo_ref[MIX_HALF:, :], preferred_element_type=F32))
    x1 = _layer_norm_rows(h, g_ref[...], bt_ref[...])
    x1_ref[...] = x1
    x1h = x1.astype(BF16)
    x1l = (x1 - x1h.astype(F32)).astype(BF16)
    nt_dims = (((1,), (1,)), ((), ()))
    logits = (lax.dot_general(wrh_ref[...], x1h, nt_dims, preferred_element_type=F32)
              + lax.dot_general(wrh_ref[...], x1l, nt_dims, preferred_element_type=F32)
              + lax.dot_general(wrl_ref[...], x1h, nt_dims, preferred_element_type=F32)) + br_ref[...]
    row = lax.broadcasted_iota(jnp.int32, logits.shape, 0)
    g = logits
    picks, vals = [], []
    for _ in range(TOPK):
        mx = jnp.max(g, axis=0, keepdims=True)
        first = jnp.min(jnp.where(g == mx, row, N_EXPERTS), axis=0, keepdims=True)
        pick = row == first
        picks.append(pick)
        vals.append(mx)
        g = jnp.where(pick, NEG_INF, g)
    es = [jnp.exp(v - vals[0]) for v in vals]
    den = es[0] + es[1] + es[2] + es[3]
    onehot = [p.astype(F32) for p in picks]
    member = onehot[0] + onehot[1] + onehot[2] + onehot[3]
    t_r = lax.broadcasted_iota(jnp.int32, (TOK_TILE, TOK_TILE), 0)
    t_c = lax.broadcasted_iota(jnp.int32, (TOK_TILE, TOK_TILE), 1)
    before = (t_r < t_c).astype(BF16)
    rank = jnp.dot(member.astype(BF16), before, preferred_element_type=F32)
    cnt = jnp.sum(member, axis=1, keepdims=True)
    cnt = jnp.ceil(cnt * (1.0 / SEG_ALIGN)) * SEG_ALIGN
    e_r = lax.broadcasted_iota(jnp.int32, (N_EXPERTS, N_EXPERTS), 0)
    e_c = lax.broadcasted_iota(jnp.int32, (N_EXPERTS, N_EXPERTS), 1)
    lower = (e_c < e_r).astype(F32)
    off = jnp.dot(lower, jnp.broadcast_to(cnt, (N_EXPERTS, TOK_TILE)), precision=lax.Precision.HIGHEST,
                  preferred_element_type=F32)
    slot = off + rank
    dests = [jnp.sum(oh * slot, axis=0, keepdims=True) for oh in onehot]
    r_iota = lax.broadcasted_iota(jnp.int32, (ASG_TILE, TOK_TILE), 0)
    perm = jnp.zeros((ASG_TILE, TOK_TILE), F32)
    for d in dests:
        perm = jnp.where(r_iota == d.astype(jnp.int32), 1.0, perm)
    xs_ref[...] = jnp.dot(perm.astype(BF16), x1h, preferred_element_type=F32)
    dg_ref[0] = jnp.concatenate(dests + [e / den for e in es], axis=0)
    cnt_ref[0] = jnp.broadcast_to(cnt, (N_EXPERTS, LANES))


def _route(x, mix_a, mix_b, w_out_bf, ln_g, ln_b, wr_t, br, *, alpha):
    t = x.shape[0]
    nt = t // TOK_TILE
    npt = mix_a[0].shape[0] // TOK_TILE
    assert mix_a[0].shape[0] % TOK_TILE == 0 and mix_a[1].shape[0] == (nt - npt) * TOK_TILE
    full2 = lambda a: pl.BlockSpec(a.shape, lambda i: (0, 0))
    prompt_rows = pl.BlockSpec((TOK_TILE, MIX_HALF), lambda i: (jnp.minimum(i, npt - 1), 0))
    sample_rows = pl.BlockSpec((TOK_TILE, MIX_HALF), lambda i: (jnp.maximum(i - npt, 0), 0))
    wr_hi = wr_t.astype(BF16)
    wr_lo = (wr_t - wr_hi.astype(F32)).astype(BF16)
    return pl.pallas_call(
        functools.partial(_route_kernel, alpha=alpha, n_prompt_tiles=npt),
        grid=(nt,),
        in_specs=[pl.BlockSpec((TOK_TILE, D_MODEL), lambda i: (i, 0)),
                  prompt_rows, sample_rows, prompt_rows, sample_rows,
                  full2(w_out_bf), full2(ln_g), full2(ln_b), full2(wr_hi), full2(wr_lo), full2(br)],
        out_specs=[pl.BlockSpec((TOK_TILE, D_MODEL), lambda i: (i, 0)),
                   pl.BlockSpec((ASG_TILE, D_MODEL), lambda i: (i, 0)),
                   pl.BlockSpec((1, 2 * TOPK, TOK_TILE), lambda i: (i, 0, 0)),
                   pl.BlockSpec((1, N_EXPERTS, LANES), lambda i: (i, 0, 0))],
        out_shape=[jax.ShapeDtypeStruct((t, D_MODEL), F32),
                   jax.ShapeDtypeStruct((nt * ASG_TILE, D_MODEL), F32),
                   jax.ShapeDtypeStruct((nt, 2 * TOPK, TOK_TILE), F32),
                   jax.ShapeDtypeStruct((nt, N_EXPERTS, LANES), F32)],
        compiler_params=_cparams(("parallel",)),
    )(x, mix_a[0], mix_a[1], mix_b[0], mix_b[1], w_out_bf, ln_g, ln_b, wr_hi, wr_lo, br)


SEG_PIECES = tuple(1 << b for b in range(TOK_TILE.bit_length() - 1, SEG_ALIGN.bit_length() - 2, -1))
TILE_PIECES = tuple(1 << b for b in range(EXP_TILE.bit_length() - 1, SEG_ALIGN.bit_length() - 2, -1))
SEG_SMALL = 64


def _expert_kernel(te_ref, nu_ref, lo_ref, hi_ref, valid_ref, src_ref, dst_ref, len_ref, tot_ref,
                   xs_ref, wg_ref, bg_ref, wu_ref, bu_ref, wd_ref, bd_ref, ys_ref,
                   xbuf, ybuf, zbuf, wgb_ref, wub_ref, wdb_ref, gsem, ssem, zsem, *, n_tok_tiles):
    m = pl.program_id(0)
    n_used = nu_ref[0]

    def zero_tail(i, start):
        n = ASG_TILE - tot_ref[i]
        for size in SEG_PIECES:
            @pl.when((n & size) != 0)
            def _():
                row = pl.multiple_of(i * ASG_TILE + tot_ref[i] + (n & ~(2 * size - 1)), SEG_ALIGN)
                cp = pltpu.make_async_copy(zbuf.at[pl.ds(0, size)], ys_ref.at[pl.ds(row, size)], zsem)
                cp.start() if start else cp.wait()

    @pl.when(m == 0)
    def _():
        zbuf[...] = jnp.zeros(zbuf.shape, F32)

    @pl.when((m >= 1) & (m - 1 < n_tok_tiles))
    def _():
        zero_tail(m - 1, False)

    @pl.when(m < n_tok_tiles)
    def _():
        zero_tail(m, True)

    def for_pieces(t, fn):
        row0 = t * EXP_TILE

        def seg_body(s, _):
            start = jnp.maximum(dst_ref[s], row0)
            n = jnp.minimum(dst_ref[s] + len_ref[s], row0 + EXP_TILE) - start
            base_src = src_ref[s] + (start - dst_ref[s])
            base_dst = start - row0
            def pieces(sizes):
                for size in sizes:
                    @pl.when((n & size) != 0)
                    def _():
                        done = n & ~(2 * size - 1)
                        fn(pl.multiple_of(base_src + done, SEG_ALIGN), pl.multiple_of(base_dst + done, SEG_ALIGN),
                           size)

            @pl.when(n >= SEG_SMALL)
            def _():
                pieces([size for size in SEG_PIECES if size >= SEG_SMALL])

            pieces([size for size in SEG_PIECES if size < SEG_SMALL])
            return 0

        lax.fori_loop(lo_ref[t], hi_ref[t], seg_body, 0)

    def wait_rows(n, copy_of):
        for size in TILE_PIECES:
            @pl.when((n & size) != 0)
            def _():
                copy_of(size).wait()

    def gather(t, start):
        slot = t % 2
        if start:
            for_pieces(t, lambda row, r, size: pltpu.make_async_copy(
                xs_ref.at[pl.ds(row, size)], xbuf.at[slot, pl.ds(r, size)], gsem.at[slot]).start())
        else:
            wait_rows(valid_ref[t], lambda size: pltpu.make_async_copy(
                xs_ref.at[pl.ds(0, size)], xbuf.at[slot, pl.ds(0, size)], gsem.at[slot]))

    def scatter(t, start):
        slot = t % 2
        if start:
            for_pieces(t, lambda row, r, size: pltpu.make_async_copy(
                ybuf.at[slot, pl.ds(r, size)], ys_ref.at[pl.ds(row, size)], ssem.at[slot]).start())
        else:
            wait_rows(valid_ref[t], lambda size: pltpu.make_async_copy(
                ybuf.at[slot, pl.ds(0, size)], ys_ref.at[pl.ds(0, size)], ssem.at[slot]))

    @pl.when((m == 0) & (n_used > 0))
    def _():
        gather(0, True)

    @pl.when(m + 1 < n_used)
    def _():
        gather(m + 1, True)

    @pl.when((m >= 2) & (m - 2 < n_used))
    def _():
        scatter(m - 2, False)

    @pl.when(m < n_used)
    def _():
        gather(m, False)
        prev = te_ref[jnp.maximum(m - 1, 0)]

        @pl.when((m == 0) | (te_ref[m] != prev))
        def _():
            wgb_ref[...] = wg_ref[0, 0].astype(BF16)
            wub_ref[...] = wu_ref[0, 0].astype(BF16)
            wdb_ref[...] = wd_ref[0, 0].astype(BF16)

        slot = m % 2
        xb = xbuf[slot].astype(BF16)
        gt = jnp.minimum(jnp.dot(xb, wgb_ref[...], preferred_element_type=F32) + bg_ref[0, 0], SWIGLU_LIMIT)
        up = jnp.clip(jnp.dot(xb, wub_ref[...], preferred_element_type=F32) + bu_ref[0, 0],
                      -SWIGLU_LIMIT, SWIGLU_LIMIT)
        act = gt * _sigmoid(SWIGLU_ALPHA * gt) * (up + 1.0)
        ybuf[slot] = jnp.dot(act.astype(BF16), wdb_ref[...], preferred_element_type=F32) + bd_ref[0, 0]
        scatter(m, True)


def _experts(tile_expert, n_used, seg_lo, seg_hi, tile_valid, seg_src, seg_dst, seg_len, tile_total, xs, layer,
             w_gate, b_gate, w_up, b_up, w_down, b_down):
    n_tiles = tile_expert.shape[0]
    n_tok_tiles = tile_total.shape[0]
    assert n_tiles > n_tok_tiles
    wspec = pl.BlockSpec((1, 1, D_MODEL, D_MODEL), lambda m, te, *_: (layer, te[m], 0, 0))
    bspec = pl.BlockSpec((1, 1, 1, D_MODEL), lambda m, te, *_: (layer, te[m], 0, 0))
    grid_spec = pltpu.PrefetchScalarGridSpec(
        num_scalar_prefetch=9, grid=(n_tiles,),
        in_specs=[pl.BlockSpec(memory_space=pl.ANY), wspec, bspec, wspec, bspec, wspec, bspec],
        out_specs=pl.BlockSpec(memory_space=pl.ANY),
        scratch_shapes=[pltpu.VMEM((2, EXP_TILE, D_MODEL), F32), pltpu.VMEM((2, EXP_TILE, D_MODEL), F32),
                        pltpu.VMEM((TOK_TILE, D_MODEL), F32)]
        + [pltpu.VMEM((D_MODEL, D_MODEL), BF16)] * 3
        + [pltpu.SemaphoreType.DMA((2,)), pltpu.SemaphoreType.DMA((2,)), pltpu.SemaphoreType.DMA(())])
    depth = w_gate.shape[0]
    bshape = (depth, N_EXPERTS, 1, D_MODEL)
    return pl.pallas_call(
        functools.partial(_expert_kernel, n_tok_tiles=n_tok_tiles),
        grid_spec=grid_spec,
        out_shape=jax.ShapeDtypeStruct(xs.shape, F32),
        compiler_params=_cparams(("arbitrary",)),
    )(tile_expert, n_used, seg_lo, seg_hi, tile_valid, seg_src, seg_dst, seg_len, tile_total, xs,
      w_gate, b_gate.reshape(bshape), w_up, b_up.reshape(bshape), w_down, b_down.reshape(bshape))


def _combine_kernel(x1_ref, ys_ref, dg_ref, g_ref, bt_ref, o_ref, *, alpha):
    dg = dg_ref[0]
    r_iota = lax.broadcasted_iota(jnp.int32, (ASG_TILE, TOK_TILE), 0)
    comb = jnp.zeros((ASG_TILE, TOK_TILE), F32)
    for k in range(TOPK):
        comb = jnp.where(r_iota == dg[k:k + 1, :].astype(jnp.int32), dg[TOPK + k:TOPK + k + 1, :], comb)
    ffn = lax.dot_general(comb.astype(BF16), ys_ref[...].astype(BF16), (((0,), (0,)), ((), ())),
                          preferred_element_type=F32)
    o_ref[...] = _layer_norm_rows(alpha * x1_ref[...] + ffn, g_ref[...], bt_ref[...])


def _combine(x1, ys, dg, ln_g, ln_b, *, alpha):
    t = x1.shape[0]
    nt = t // TOK_TILE
    return pl.pallas_call(
        functools.partial(_combine_kernel, alpha=alpha),
        grid=(nt,),
        in_specs=[pl.BlockSpec((TOK_TILE, D_MODEL), lambda i: (i, 0)),
                  pl.BlockSpec((ASG_TILE, D_MODEL), lambda i: (i, 0)),
                  pl.BlockSpec((1, 2 * TOPK, TOK_TILE), lambda i: (i, 0, 0)),
                  pl.BlockSpec((1, D_MODEL), lambda i: (0, 0)),
                  pl.BlockSpec((1, D_MODEL), lambda i: (0, 0))],
        out_specs=pl.BlockSpec((TOK_TILE, D_MODEL), lambda i: (i, 0)),
        out_shape=jax.ShapeDtypeStruct((t, D_MODEL), F32),
        compiler_params=_cparams(("parallel",)),
    )(x1, ys, dg, ln_g, ln_b)


def _channel_mix(x, mix_a, mix_b, p, layer, alpha):
    t = x.shape[0]
    nt = t // TOK_TILE
    x1, xs, dg, cnt = _route(
        x, mix_a, mix_b, p['w_out'][layer].astype(BF16), p['ln_g'][layer, 0].reshape(1, D_MODEL),
        p['ln_b'][layer, 0].reshape(1, D_MODEL), p['router_w'][layer].T.astype(F32),
        p['router_b'][layer].reshape(N_EXPERTS, 1).astype(F32), alpha=alpha)
    cnt = cnt[:, :, 0].astype(jnp.int32)
    local_off = jnp.cumsum(cnt, axis=1) - cnt
    tile_rows = (jnp.arange(nt, dtype=jnp.int32) * ASG_TILE)[:, None] + local_off
    total = jnp.sum(cnt, axis=0)
    padded = (total + EXP_TILE - 1) // EXP_TILE * EXP_TILE
    pend = jnp.cumsum(padded)
    expert_rows = (pend - padded)[None, :] + jnp.cumsum(cnt, axis=0) - cnt
    n_tiles = -(-(nt * (TOK_TILE * TOPK + N_EXPERTS * (SEG_ALIGN - 1))) // EXP_TILE) + N_EXPERTS + 2
    tile_start = (jnp.arange(n_tiles, dtype=jnp.int32) * EXP_TILE)[:, None]
    count_below = lambda a, bound: jnp.sum((a[None, :] < bound).astype(jnp.int32), axis=1)
    tile_expert = jnp.minimum(count_below(pend, tile_start + 1), N_EXPERTS - 1)
    n_used = (pend[-1:] // EXP_TILE).astype(jnp.int32)
    seg_src = tile_rows.T.reshape(-1)
    seg_dst = expert_rows.T.reshape(-1).astype(jnp.int32)
    seg_len = cnt.T.reshape(-1)
    seg_lo = count_below(seg_dst + seg_len, tile_start + 1)
    seg_hi = count_below(seg_dst, tile_start + EXP_TILE)
    expert_end = (pend - padded + total)[tile_expert]
    tile_valid = jnp.clip(expert_end - tile_start[:, 0], 0, EXP_TILE).astype(jnp.int32)
    ys = _experts(tile_expert, n_used, seg_lo, seg_hi, tile_valid, seg_src, seg_dst, seg_len, jnp.sum(cnt, axis=1),
                  xs, layer,
                  p['moe_w_gate'], p['moe_b_gate'], p['moe_w_up'], p['moe_b_up'], p['moe_w_down'], p['moe_b_down'])
    return _combine(x1, ys, dg, p['ln_g'][layer, 1].reshape(1, D_MODEL), p['ln_b'][layer, 1].reshape(1, D_MODEL),
                    alpha=alpha)


def kernel(x_prompt, x_sample, state_s5, state_pool, cache_k, cache_v, page_table, w_in_ab, s5_lambda_re, s5_lambda_im, s5_b_re, s5_b_im, s5_c_re, s5_c_im, s5_d, s5_log_dt, s5_w_glu, s5_b_glu, gm_norm_g, gm_w_s, gm_b_s, w_in_cd, pool_w, pool_scale, w_out, ln_g, ln_b, router_w, router_b, moe_w_gate, moe_b_gate, moe_w_up, moe_b_up, moe_w_down, moe_b_down):
    p = dict(w_in_ab=w_in_ab, s5_lambda_re=s5_lambda_re, s5_lambda_im=s5_lambda_im,
             s5_b_re=s5_b_re, s5_b_im=s5_b_im, s5_c_re=s5_c_re, s5_c_im=s5_c_im, s5_d=s5_d,
             s5_log_dt=s5_log_dt, s5_w_glu=s5_w_glu, s5_b_glu=s5_b_glu, gm_norm_g=gm_norm_g,
             gm_w_s=gm_w_s, gm_b_s=gm_b_s, w_in_cd=w_in_cd, pool_w=pool_w, pool_scale=pool_scale,
             w_out=w_out, ln_g=ln_g, ln_b=ln_b, router_w=router_w, router_b=router_b,
             moe_w_gate=moe_w_gate, moe_b_gate=moe_b_gate, moe_w_up=moe_w_up, moe_b_up=moe_b_up,
             moe_w_down=moe_w_down, moe_b_down=moe_b_down)
    n_bp, n_sp, _ = x_prompt.shape
    n_bs, n_ss, _ = x_sample.shape
    t_p, t_s = n_bp * n_sp, n_bs * n_ss
    depth = w_out.shape[0]
    alpha = (2 * depth) ** 0.25
    past_len = page_table.shape[1] * PAGE_SIZE
    x = jnp.concatenate([x_prompt.reshape(t_p, D_MODEL), x_sample.reshape(t_s, D_MODEL)], axis=0)
    zero_s5 = jnp.zeros((n_bp, S5_GROUPS, S5_STATE, 2), F32)
    pos = jnp.concatenate([jnp.tile(jnp.arange(n_sp), n_bp), jnp.tile(past_len + jnp.arange(n_ss), n_bs)])
    rope_tables = _rope_tables(pos)
    cache_k2 = cache_k.transpose(0, 1, 3, 4, 2).reshape(cache_k.shape[0], cache_k.shape[1], MIX_HALF, PAGE_SIZE)
    cache_v2 = cache_v.transpose(0, 1, 3, 4, 2).reshape(cache_v.shape[0], cache_v.shape[1], MIX_HALF, PAGE_SIZE)
    s5_p, s5_s, gmv_s, pool_p, pool_s, k_p, v_p, k_s, v_s = [], [], [], [], [], [], [], [], []
    for layer in range(depth):
        i = layer // 2
        if layer % 2 == 0:
            proj = _proj(x, w_in_ab[i].astype(BF16), PROJ_TILE)
            a_p, b_p, _, st_p = _even_layer_mix(proj, zero_s5, p, i, n_b=n_bp, n_s=n_sp, sample=False)
            a_s, b_s, vn, st_s = _even_layer_mix(proj, state_s5[i], p, i, n_b=n_bs, n_s=n_ss, sample=True, row0=t_p)
            s5_p.append(st_p)
            s5_s.append(st_s)
            gmv_s.append(vn.reshape(n_bs, n_ss, MIX_HALF))
        else:
            proj = _proj(x, w_in_cd[i].astype(BF16), PROJ_TILE)
            pw = pool_w[i].astype(BF16)
            ps = pool_scale[i].reshape(1, MIX_HALF).astype(F32)
            a_p = _pool_mixer(proj, jnp.zeros((n_bp, POOL_HALO, MIX_HALF), F32), pw, ps, n_b=n_bp, n_s=n_sp, base=0)
            halo = jnp.concatenate([jnp.zeros((n_bs, POOL_HALO - POOL_BUF, MIX_HALF), F32),
                                    state_pool[i].astype(F32)], axis=1)
            a_s = _pool_mixer(proj, halo, pw, ps, n_b=n_bs, n_s=n_ss, base=POOL_BUF, row0=t_p)
            q_rot, k_rot, qb, kb, vb, kmean = _rope(proj, rope_tables)
            b_p = _moba_prompt(q_rot, qb, kb, vb, kmean, n_b=n_bp, n_s=n_sp)
            b_s = _moba_sample(page_table, cache_k2, cache_v2, i, q_rot, k_rot, proj, n_b=n_bs, n_q=n_ss, row0=t_p)
            c_p = proj[:t_p, :MIX_HALF].reshape(n_bp, n_sp, MIX_HALF)
            c_s = proj[t_p:, :MIX_HALF].reshape(n_bs, n_ss, MIX_HALF)
            pool_p.append(c_p[:, -POOL_BUF:])
            pool_s.append(jnp.concatenate([state_pool[i].astype(F32), c_s], axis=1)[:, -POOL_BUF:])
            k_p.append(k_rot[:t_p].reshape(n_bp, n_sp, ATT_HEADS, HEAD_DIM))
            k_s.append(k_rot[t_p:].reshape(n_bs, n_ss, ATT_HEADS, HEAD_DIM))
            v_p.append(proj[:t_p, 3 * MIX_HALF:].reshape(n_bp, n_sp, ATT_HEADS, HEAD_DIM))
            v_s.append(proj[t_p:, 3 * MIX_HALF:].reshape(n_bs, n_ss, ATT_HEADS, HEAD_DIM))
        x = _channel_mix(x, (a_p, a_s), (b_p, b_s), p, layer, alpha)
    return (x[:t_p].reshape(n_bp, n_sp, D_MODEL), x[t_p:].reshape(n_bs, n_ss, D_MODEL),
            jnp.stack(s5_p), jnp.stack(s5_s), jnp.stack(gmv_s), jnp.stack(pool_p), jnp.stack(pool_s),
            jnp.stack(k_p), jnp.stack(v_p), jnp.stack(k_s), jnp.stack(v_s))
```

```python
import functools
import math

import jax
import jax.numpy as jnp
from jax import lax
from jax.experimental import pallas as pl
from jax.experimental.pallas import tpu as pltpu

F32 = jnp.float32
BF16 = jnp.bfloat16

D_MODEL = 1024
MIX_HALF = D_MODEL // 2
S5_GROUP_CH = 16
S5_GROUPS = MIX_HALF // S5_GROUP_CH
S5_STATE = 64
GM_CHUNK = 128
GM_GROUPS = 4
GM_CH = MIX_HALF // GM_GROUPS
POOL_WINDOWS = (2, 4, 8, 16)
POOL_CH = MIX_HALF // len(POOL_WINDOWS)
POOL_BUF = max(POOL_WINDOWS) - 1
ATT_HEADS = 8
HEAD_DIM = MIX_HALF // ATT_HEADS
ROT_DIM = HEAD_DIM // 4
ROPE_THETA = 500000.0
MOBA_BLOCK = 256
MOBA_TOPK = 3
N_EXPERTS = 32
TOPK = 4
SWIGLU_LIMIT = 7.0
SWIGLU_ALPHA = 1.702
LN_EPS = 1e-5
PAGE_SIZE = 128

LANES = 128
SUBLANES = 8
VMEM_LIMIT = 56 * 1024 * 1024

S5_OCT = 4
S5_PAIRS = S5_GROUPS * S5_STATE // LANES
NEG_INF = float("-inf")


def _cparams(sem):
    return pltpu.CompilerParams(dimension_semantics=sem, vmem_limit_bytes=VMEM_LIMIT)


def _gelu(x):
    return 0.5 * x * (1.0 + jnp.tanh(math.sqrt(2.0 / math.pi) * (x + 0.044715 * (x * x * x))))


def _sigmoid(x):
    return 1.0 / (1.0 + jnp.exp(-x))


PROJ_TILE = 640


def _proj_kernel(x_ref, w_ref, o_ref):
    o_ref[...] = jnp.dot(x_ref[...].astype(BF16), w_ref[...], preferred_element_type=F32)


def _proj(x, w_bf16, tm):
    t, k = x.shape
    n = w_bf16.shape[1]
    return pl.pallas_call(
        _proj_kernel,
        grid=(t // tm,),
        in_specs=[pl.BlockSpec((tm, k), lambda i: (i, 0)),
                  pl.BlockSpec((k, n), lambda i: (0, 0))],
        out_specs=pl.BlockSpec((tm, n), lambda i: (i, 0)),
        out_shape=jax.ShapeDtypeStruct((t, n), F32),
        compiler_params=_cparams(("parallel",)),
    )(x, w_bf16)


def _s5_params(lam_re, lam_im, b_re, b_im, c_re, c_im, log_dt):
    dt = jnp.exp(log_dt.astype(F32))[:, None]
    lam = lax.complex(lam_re.astype(F32), lam_im.astype(F32))
    lam_bar = jnp.exp(lam * dt)
    b_bar = ((lam_bar - 1.0) / lam)[..., None] * lax.complex(b_re.astype(F32), b_im.astype(F32))
    eye = jnp.eye(SUBLANES, dtype=F32)
    bb = b_bar.reshape(S5_OCT, 8, S5_STATE, S5_GROUP_CH)

    def bdiag_b(t):
        return jnp.einsum('qgph,gk->qghkp', t, eye).reshape(S5_OCT, 128, 512)

    bw = jnp.concatenate([bdiag_b(bb.real), bdiag_b(bb.imag)], axis=-1).astype(BF16)
    cc_re = c_re.astype(F32).reshape(S5_OCT, 8, S5_GROUP_CH, S5_STATE)
    cc_im = c_im.astype(F32).reshape(S5_OCT, 8, S5_GROUP_CH, S5_STATE)

    def bdiag_c(t):
        return jnp.einsum('qghp,gk->qgpkh', t, eye).reshape(S5_OCT, 512, 128)

    cw = jnp.concatenate([bdiag_c(cc_re), -bdiag_c(cc_im)], axis=1).astype(BF16)
    rows = jnp.arange(SUBLANES)
    planes = []
    for d in (1, 2, 4):
        pw = jnp.exp(lam * dt * float(d)).reshape(S5_PAIRS, 1, LANES)
        m = (rows >= d).astype(F32)[None, :, None]
        planes += [pw.real * m, pw.imag * m]
    pw = jnp.exp((lam * dt).reshape(S5_PAIRS, 1, LANES) * (rows + 1).astype(F32)[None, :, None])
    planes += [pw.real, pw.imag]
    coef = jnp.stack(planes, axis=1).astype(F32)
    return bw, cw, coef


def _even_kernel(u_ref, gu_ref, gv_ref, x0_ref, bw_ref, coef_ref, cw_ref, d_ref, wglu_ref, bglu_ref,
                 ng_ref, m_ref, bias_ref, *rest, ts, chunk, per_block_init, with_vn):
    if with_vn:
        a_ref, b_ref, vn_ref, st_out_ref, st_ref, carry_ref = rest
    else:
        a_ref, b_ref, st_out_ref, st_ref, carry_ref = rest
        vn_ref = None
    n_rb = ts // SUBLANES

    if not per_block_init:
        @pl.when(pl.program_id(1) == 0)
        def _():
            carry_ref[...] = x0_ref[0]

    u = u_ref[...]
    ub = u.astype(BF16)
    for q in range(S5_OCT):
        bu = jnp.dot(ub[:, q * 128:(q + 1) * 128], bw_ref[q], preferred_element_type=F32)
        for c in range(4):
            st_ref[q * 4 + c] = bu[:, c * 128:(c + 1) * 128]
            st_ref[S5_PAIRS + q * 4 + c] = bu[:, 512 + c * 128:512 + (c + 1) * 128]

    def pair_body(j, _):
        cf = coef_ref[j]
        a1r, a1i, a2r, a2i, a4r, a4i, pr, pi = [cf[k] for k in range(8)]

        def rb_body(r, carry):
            cr, ci = carry
            row = pl.multiple_of(r * SUBLANES, SUBLANES)
            xr = st_ref[j, pl.ds(row, SUBLANES), :]
            xi = st_ref[S5_PAIRS + j, pl.ds(row, SUBLANES), :]
            for d, ar, ai in ((1, a1r, a1i), (2, a2r, a2i), (4, a4r, a4i)):
                sr = pltpu.roll(xr, d, 0)
                si = pltpu.roll(xi, d, 0)
                xr, xi = xr + ar * sr - ai * si, xi + ar * si + ai * sr
            if per_block_init:
                cr = x0_ref[j, r]
                ci = x0_ref[S5_PAIRS + j, r]
            xr, xi = xr + pr * cr - pi * ci, xi + pr * ci + pi * cr
            st_ref[j, pl.ds(row, SUBLANES), :] = xr
            st_ref[S5_PAIRS + j, pl.ds(row, SUBLANES), :] = xi
            ncr = jnp.broadcast_to(xr[SUBLANES - 1:SUBLANES, :], (SUBLANES, LANES))
            nci = jnp.broadcast_to(xi[SUBLANES - 1:SUBLANES, :], (SUBLANES, LANES))
            if per_block_init:
                st_out_ref[j, r] = ncr
                st_out_ref[S5_PAIRS + j, r] = nci
            return ncr, nci

        cr, ci = lax.fori_loop(0, n_rb, rb_body, (carry_ref[j], carry_ref[S5_PAIRS + j]),
                               unroll=min(8, n_rb))
        carry_ref[j] = cr
        carry_ref[S5_PAIRS + j] = ci
        return 0

    lax.fori_loop(0, S5_PAIRS, pair_body, 0)

    if not per_block_init:
        @pl.when(pl.program_id(1) == pl.num_programs(1) - 1)
        def _():
            st_out_ref[0] = carry_ref[...]

    ys = []
    for q in range(S5_OCT):
        xq = jnp.concatenate([st_ref[q * 4 + c] for c in range(4)]
                             + [st_ref[S5_PAIRS + q * 4 + c] for c in range(4)], axis=-1)
        ys.append(jnp.dot(xq.astype(BF16), cw_ref[q], preferred_element_type=F32))
    y = jnp.concatenate(ys, axis=-1) + d_ref[...] * u
    g = _gelu(y)
    z = jnp.dot(g.astype(BF16), wglu_ref[...], preferred_element_type=F32) + bglu_ref[...]
    a_ref[...] = g * _sigmoid(z)

    gu = _gelu(gu_ref[...])
    gv = _gelu(gv_ref[...])
    for gi in range(GM_GROUPS):
        sl = slice(gi * GM_CH, (gi + 1) * GM_CH)
        v = gv[:, sl]
        mu = jnp.mean(v, axis=-1, keepdims=True)
        vc = v - mu
        var = jnp.mean(vc * vc, axis=-1, keepdims=True)
        vn = vc * lax.rsqrt(var + LN_EPS) * ng_ref[:, sl]
        if with_vn:
            vn_ref[:, sl] = vn
        vnb = vn.astype(BF16)
        for c in range(ts // chunk):
            rs = slice(c * chunk, (c + 1) * chunk)
            s = jnp.dot(m_ref[gi], vnb[rs], preferred_element_type=F32) + bias_ref[:, sl]
            b_ref[rs, sl] = gu[rs, sl] * s


def _even_mixer(proj, x0, s5p, d_skip, w_glu, b_glu, norm_g, m_mix, bias, *, n_b, n_s, per_block_init, row0=0):
    bw, cw, coef = s5p
    if per_block_init:
        ts, grid, chunk = n_b * n_s, (1, 1), n_b * n_s
        assert n_s == SUBLANES
        n_rb = ts // SUBLANES
        x0_spec = pl.BlockSpec((2 * S5_PAIRS, n_rb, SUBLANES, LANES), lambda b, t: (0, 0, 0, 0))
        st_shape = (2 * S5_PAIRS, n_rb, SUBLANES, LANES)
        st_spec = x0_spec
    else:
        ts = min(512, n_s)
        grid, chunk = (n_b, n_s // ts), GM_CHUNK
        x0_spec = pl.BlockSpec((1, 2 * S5_PAIRS, SUBLANES, LANES), lambda b, t: (b, 0, 0, 0))
        st_shape = (n_b, 2 * S5_PAIRS, SUBLANES, LANES)
        st_spec = x0_spec
    nt = grid[1]
    with_vn = per_block_init
    blk0 = row0 // ts
    assert row0 % ts == 0

    def rows(col):
        return pl.BlockSpec((ts, MIX_HALF), lambda b, t, col=col: (blk0 + b * nt + t, col))

    def full(a):
        return pl.BlockSpec(a.shape, lambda b, t, nd=a.ndim: (0,) * nd)

    row_out = pl.BlockSpec((ts, MIX_HALF), lambda b, t: (b * nt + t, 0))
    rows_shape = jax.ShapeDtypeStruct((n_b * n_s, MIX_HALF), F32)
    n_row_outs = 3 if with_vn else 2
    out_shape = [rows_shape] * n_row_outs + [jax.ShapeDtypeStruct(st_shape, F32)]
    out_specs = [row_out] * n_row_outs + [st_spec]
    weights = (bw, coef, cw, d_skip, w_glu, b_glu, norm_g, m_mix, bias)
    outs = pl.pallas_call(
        functools.partial(_even_kernel, ts=ts, chunk=chunk, per_block_init=per_block_init, with_vn=with_vn),
        grid=grid,
        in_specs=[rows(0), rows(1), rows(2), x0_spec] + [full(w) for w in weights],
        out_specs=out_specs,
        out_shape=out_shape,
        scratch_shapes=[pltpu.VMEM((2 * S5_PAIRS, ts, LANES), F32),
                        pltpu.VMEM((2 * S5_PAIRS, SUBLANES, LANES), F32)],
        compiler_params=_cparams(("arbitrary", "arbitrary")),
    )(proj, proj, proj, x0, *weights)
    if with_vn:
        return outs[0], outs[1], outs[2], outs[3]
    return outs[0], outs[1], None, outs[2]


def _state_to_lanes(x0, n_b):
    re = x0[..., 0].astype(F32).reshape(n_b, S5_PAIRS, LANES)
    im = x0[..., 1].astype(F32).reshape(n_b, S5_PAIRS, LANES)
    return jnp.concatenate([re, im], axis=1)


def _lanes_to_state(st, n_b):
    re = st[:, :S5_PAIRS].reshape(n_b, S5_GROUPS, S5_STATE)
    im = st[:, S5_PAIRS:].reshape(n_b, S5_GROUPS, S5_STATE)
    return jnp.stack([re, im], axis=-1)


def _even_layer_mix(proj, x0, p, i, *, n_b, n_s, sample, row0=0):
    s5p = _s5_params(p['s5_lambda_re'][i], p['s5_lambda_im'][i], p['s5_b_re'][i], p['s5_b_im'][i],
                     p['s5_c_re'][i], p['s5_c_im'][i], p['s5_log_dt'][i])
    st0 = _state_to_lanes(x0, n_b)
    if sample:
        chunk = n_s
        x0k = jnp.broadcast_to(st0.transpose(1, 0, 2)[:, :, None, :], (2 * S5_PAIRS, n_b, SUBLANES, LANES))
        w = jnp.tril(p['gm_w_s'][i][:, :chunk, :chunk])
        m_mix = jnp.einsum('bc,gij->gbicj', jnp.eye(n_b, dtype=F32), w).reshape(GM_GROUPS, n_b * chunk, n_b * chunk)
        bias_rows = jnp.tile(p['gm_b_s'][i][:, :chunk].T, (n_b, 1))
    else:
        x0k = jnp.broadcast_to(st0[:, :, None, :], (n_b, 2 * S5_PAIRS, SUBLANES, LANES))
        m_mix = jnp.tril(p['gm_w_s'][i][:, :GM_CHUNK, :GM_CHUNK])
        bias_rows = p['gm_b_s'][i][:, :GM_CHUNK].T
    bias = jnp.repeat(bias_rows.astype(F32), GM_CH, axis=1)
    a, b, vn, st = _even_mixer(
        proj, x0k, s5p, p['s5_d'][i].reshape(1, MIX_HALF).astype(F32), p['s5_w_glu'][i].astype(BF16),
        p['s5_b_glu'][i].reshape(1, MIX_HALF).astype(F32), p['gm_norm_g'][i].reshape(1, MIX_HALF).astype(F32),
        m_mix.astype(BF16), bias, n_b=n_b, n_s=n_s, per_block_init=sample, row0=row0)
    if sample:
        st = st[:, :, 0, :].transpose(1, 0, 2)
    else:
        st = st[:, :, 0, :]
    return a, b, vn, _lanes_to_state(st, n_b)


POOL_HALO = 16


def _pool_kernel(c_ref, halo_ref, w_ref, scale_ref, o_ref, hist_ref, *, ts, base):
    t = pl.program_id(1)

    @pl.when(t == 0)
    def _():
        hist_ref[0:POOL_HALO, :] = halo_ref[0]

    @pl.when(t > 0)
    def _():
        hist_ref[0:POOL_HALO, :] = hist_ref[ts:ts + POOL_HALO, :]

    hist_ref[POOL_HALO:POOL_HALO + ts, :] = c_ref[...]
    pos = base + t * ts + lax.broadcasted_iota(jnp.int32, (ts, 1), 0)
    for g, win in enumerate(POOL_WINDOWS):
        sl = slice(g * POOL_CH, (g + 1) * POOL_CH)
        x = hist_ref[POOL_HALO:POOL_HALO + ts, sl]
        acc = x
        for d in range(1, win):
            acc = acc + hist_ref[POOL_HALO - d:POOL_HALO - d + ts, sl]
        cnt = jnp.minimum(pos + 1, win).astype(F32)
        pooled = acc / cnt - x
        y = jnp.dot(pooled.astype(BF16), w_ref[g], preferred_element_type=F32)
        o_ref[:, sl] = y * scale_ref[:, sl]


def _pool_mixer(proj, halo, w, scale, *, n_b, n_s, base, row0=0):
    ts = min(512, n_s)
    nt = n_s // ts
    blk0 = row0 // ts
    assert row0 % ts == 0
    return pl.pallas_call(
        functools.partial(_pool_kernel, ts=ts, base=base),
        grid=(n_b, nt),
        in_specs=[pl.BlockSpec((ts, MIX_HALF), lambda b, t: (blk0 + b * nt + t, 0)),
                  pl.BlockSpec((1, POOL_HALO, MIX_HALF), lambda b, t: (b, 0, 0)),
                  pl.BlockSpec(w.shape, lambda b, t: (0, 0, 0)),
                  pl.BlockSpec((1, MIX_HALF), lambda b, t: (0, 0))],
        out_specs=pl.BlockSpec((ts, MIX_HALF), lambda b, t: (b * nt + t, 0)),
        out_shape=jax.ShapeDtypeStruct((n_b * n_s, MIX_HALF), F32),
        scratch_shapes=[pltpu.VMEM((POOL_HALO + ts, MIX_HALF), F32)],
        compiler_params=_cparams(("arbitrary", "arbitrary")),
    )(proj, halo, w, scale)


def _rope_tables(pos):
    half = ROT_DIM // 2
    inv = ROPE_THETA ** (-jnp.arange(half, dtype=F32) * 2.0 / ROT_DIM)
    ang = pos.astype(F32)[:, None] * inv[None, :]
    cos, sin = jnp.cos(ang), jnp.sin(ang)
    n = pos.shape[0]
    one = jnp.ones((n, HEAD_DIM - ROT_DIM), F32)
    zero = jnp.zeros((n, HEAD_DIM - ROT_DIM), F32)
    z8 = jnp.zeros((n, half), F32)
    ca = jnp.concatenate([cos, cos, one], axis=1)
    sp = jnp.concatenate([z8, sin, zero], axis=1)
    sm = jnp.concatenate([-sin, z8, zero], axis=1)
    return tuple(jnp.tile(t, (1, LANES // HEAD_DIM)) for t in (ca, sp, sm))


def _rope_kernel(q_ref, k_ref, v_ref, ca_ref, sp_ref, sm_ref, qo_ref, ko_ref, qb_ref, kb_ref, vb_ref, km_ref):
    ca, sp, sm = ca_ref[...], sp_ref[...], sm_ref[...]
    half = ROT_DIM // 2
    for c in range(MIX_HALF // LANES):
        sl = slice(c * LANES, (c + 1) * LANES)
        for src, dst in ((q_ref, qo_ref), (k_ref, ko_ref)):
            x = src[:, sl]
            dst[:, sl] = x * ca + pltpu.roll(x, half, 1) * sp + pltpu.roll(x, LANES - half, 1) * sm
    q = qo_ref[...]
    k = ko_ref[...]
    qb_ref[...] = (q * (HEAD_DIM ** -0.5)).astype(BF16)
    kb_ref[...] = k.astype(BF16)
    vb_ref[...] = v_ref[...].astype(BF16)
    km_ref[0] = jnp.mean(k, axis=0, keepdims=True)


def _rope(proj, tables):
    t_rows = proj.shape[0]
    ts = MOBA_BLOCK

    def col(c):
        return pl.BlockSpec((ts, MIX_HALF), lambda i, c=c: (i, c))

    tab = pl.BlockSpec((ts, LANES), lambda i: (i, 0))
    row = pl.BlockSpec((ts, MIX_HALF), lambda i: (i, 0))
    f32o = jax.ShapeDtypeStruct((t_rows, MIX_HALF), F32)
    bfo = jax.ShapeDtypeStruct((t_rows, MIX_HALF), BF16)
    outs = pl.pallas_call(
        _rope_kernel,
        grid=(t_rows // ts,),
        in_specs=[col(1), col(2), col(3), tab, tab, tab],
        out_specs=[row, row, row, row, row, pl.BlockSpec((1, 1, MIX_HALF), lambda i: (i, 0, 0))],
        out_shape=[f32o, f32o, bfo, bfo, bfo, jax.ShapeDtypeStruct((t_rows // ts, 1, MIX_HALF), F32)],
        compiler_params=_cparams(("parallel",)),
    )(proj, proj, proj, *tables)
    return list(outs[:5]) + [outs[5].reshape(t_rows // ts, MIX_HALF)]


def _top_rows_mask(gate, n_valid_rows, k_top):
    n = gate.shape[0]
    row = lax.broadcasted_iota(jnp.int32, gate.shape, 0)
    live = row < n_valid_rows
    sel = jnp.zeros(gate.shape, jnp.bool_)
    for _ in range(k_top):
        g = jnp.where(live, gate, NEG_INF)
        mx = jnp.max(g, axis=0, keepdims=True)
        first = jnp.min(jnp.where(live & (g == mx), row, n), axis=0, keepdims=True)
        pick = row == first
        sel = sel | pick
        live = live & jnp.logical_not(pick)
    return sel


def _top_lanes_mask(gate, n_valid, k_top):
    n = gate.shape[1]
    lane = lax.broadcasted_iota(jnp.int32, gate.shape, 1)
    live = lane < n_valid
    sel = jnp.zeros(gate.shape, jnp.bool_)
    for _ in range(k_top):
        g = jnp.where(live, gate, NEG_INF)
        mx = jnp.max(g, axis=1, keepdims=True)
        first = jnp.min(jnp.where(live & (g == mx), lane, n), axis=1, keepdims=True)
        pick = lane == first
        sel = sel | pick
        live = live & jnp.logical_not(pick)
    return sel


HEAD_PAIRS = MIX_HALF // LANES


def _moba_prompt_kernel(q_ref, qb_ref, kb_ref, vb_ref, km_ref, o_ref,
                        qbd_ref, sel_ref, m_ref, l_ref, acc_ref, *, n_blk):
    qi = pl.program_id(1)
    tq = MOBA_BLOCK
    lane = lax.broadcasted_iota(jnp.int32, (tq, LANES), 1)
    krow = lax.broadcasted_iota(jnp.int32, (tq, 2 * tq), 0)
    qcol = lax.broadcasted_iota(jnp.int32, (tq, 2 * tq), 1) % tq
    causal = krow <= qcol
    nt_dims = (((1,), (1,)), ((), ()))
    tn_dims = (((0,), (0,)), ((), ()))
    row0 = pl.multiple_of(qi * tq, tq)

    def attend(pr, r0, mask_of, first):
        ps = slice(pr * LANES, (pr + 1) * LANES)
        kblk = kb_ref[pl.ds(r0, tq), ps]
        vblk = vb_ref[pl.ds(r0, tq), ps]
        for ck in range(2 * tq // LANES):
            cs = slice(ck * LANES, (ck + 1) * LANES)
            hd = ck * LANES // tq
            qs = slice(ck * LANES - hd * tq, (ck + 1) * LANES - hd * tq)
            s = lax.dot_general(kblk, qbd_ref[pr, cs, :], nt_dims, preferred_element_type=F32)
            s = jnp.where(mask_of(cs), s, NEG_INF)
            m_blk = jnp.max(s, axis=0, keepdims=True)
            m_old = m_ref[pr, :, cs]
            m_new = m_blk if first else jnp.maximum(m_old, m_blk)
            p = jnp.exp(s - m_new)
            pv = lax.dot_general(vblk, p.astype(BF16), tn_dims, preferred_element_type=F32)
            pv = pv[hd * HEAD_DIM:(hd + 1) * HEAD_DIM]
            if first:
                l_ref[pr, :, cs] = jnp.sum(p, axis=0, keepdims=True)
                acc_ref[pr, hd, :, qs] = pv
            else:
                alpha = jnp.exp(m_old - m_new)
                l_ref[pr, :, cs] = alpha * l_ref[pr, :, cs] + jnp.sum(p, axis=0, keepdims=True)
                acc_ref[pr, hd, :, qs] = alpha * acc_ref[pr, hd, :, qs] + pv
            m_ref[pr, :, cs] = m_new

    for pr in range(HEAD_PAIRS):
        ps = slice(pr * LANES, (pr + 1) * LANES)
        qb = qb_ref[:, ps]
        zero = jnp.zeros_like(qb)

        def block_diag(t):
            return jnp.concatenate([jnp.where(lane < HEAD_DIM, t, zero), jnp.where(lane >= HEAD_DIM, t, zero)], axis=0)

        qbd_ref[pr] = block_diag(qb)
        q_lo = (q_ref[:, ps] * (HEAD_DIM ** -0.5) - qb.astype(F32)).astype(BF16)
        km = km_ref[:, ps]
        km_hi = km.astype(BF16)
        km_lo = (km - km_hi.astype(F32)).astype(BF16)
        gate = (lax.dot_general(km_hi, qbd_ref[pr], nt_dims, preferred_element_type=F32)
                + lax.dot_general(km_lo, qbd_ref[pr], nt_dims, preferred_element_type=F32)
                + lax.dot_general(km_hi, block_diag(q_lo), nt_dims, preferred_element_type=F32))
        sel_ref[pr] = _top_rows_mask(gate, qi, MOBA_TOPK).astype(F32)
        attend(pr, row0, lambda cs: causal[:, cs], True)

    def blk_body(j, _):
        r0 = pl.multiple_of(j * tq, tq)
        for pr in range(HEAD_PAIRS):
            picked = sel_ref[pr, pl.ds(j, 1), :] > 0.5
            attend(pr, r0, lambda cs, picked=picked: picked[:, cs], False)
        return 0

    lax.fori_loop(0, qi, blk_body, 0)
    for pr in range(HEAD_PAIRS):
        l = l_ref[pr]
        out_t = jnp.concatenate([acc_ref[pr, 0] / l[:, :tq], acc_ref[pr, 1] / l[:, tq:]], axis=0)
        o_ref[:, pr * LANES:(pr + 1) * LANES] = out_t.T


def _moba_prompt(q_rot, qb, kb, vb, kmean, *, n_b, n_s):
    n_blk = n_s // MOBA_BLOCK
    tq = MOBA_BLOCK
    qspec = pl.BlockSpec((tq, MIX_HALF), lambda b, i: (b * n_blk + i, 0))
    kvspec = pl.BlockSpec((n_s, MIX_HALF), lambda b, i: (b, 0))
    return pl.pallas_call(
        functools.partial(_moba_prompt_kernel, n_blk=n_blk),
        grid=(n_b, n_blk),
        in_specs=[qspec, qspec, kvspec, kvspec, pl.BlockSpec((n_blk, MIX_HALF), lambda b, i: (b, 0))],
        out_specs=qspec,
        out_shape=jax.ShapeDtypeStruct((n_b * n_s, MIX_HALF), F32),
        scratch_shapes=[pltpu.VMEM((HEAD_PAIRS, 2 * tq, LANES), BF16),
                        pltpu.VMEM((HEAD_PAIRS, n_blk, 2 * tq), F32),
                        pltpu.VMEM((HEAD_PAIRS, 1, 2 * tq), F32),
                        pltpu.VMEM((HEAD_PAIRS, 1, 2 * tq), F32),
                        pltpu.VMEM((HEAD_PAIRS, 2, HEAD_DIM, tq), F32)],
        compiler_params=_cparams(("arbitrary", "arbitrary")),
    )(q_rot, qb, kb, vb, kmean)


PAGES_PER_STEP = 16
BLOCK_PAGES = MOBA_BLOCK // PAGE_SIZE


def _moba_sample_kernel(pt_ref, *refs, n_blk, n_q):
    del pt_ref
    kp = refs[:PAGES_PER_STEP]
    vp = refs[PAGES_PER_STEP:2 * PAGES_PER_STEP]
    qbt_ref, qbtf_ref, kn_ref, vn_ref, o_ref, oacc_ref, m_ref, l_ref, km_ref = refs[2 * PAGES_PER_STEP:]
    c = pl.program_id(1)
    ncol = ATT_HEADS * n_q
    blocks_per_step = PAGES_PER_STEP // BLOCK_PAGES
    nt_dims = (((1,), (1,)), ((), ()))
    qbt = qbt_ref[0]
    lane_c = lax.broadcasted_iota(jnp.int32, (ncol, LANES), 1)
    lane_k = lax.broadcasted_iota(jnp.int32, (MIX_HALF, LANES), 1)

    @pl.when(c == 0)
    def _():
        m_ref[...] = jnp.zeros(m_ref.shape, F32)
        l_ref[...] = jnp.zeros(l_ref.shape, F32)
        km_ref[...] = jnp.zeros(km_ref.shape, F32)

    kt_all = jnp.concatenate([kp[j][0, 0] for j in range(PAGES_PER_STEP)], axis=1)
    s_all = jnp.dot(qbt, kt_all.astype(BF16), preferred_element_type=F32)
    for blk in range(blocks_per_step):
        n = c * blocks_per_step + blk
        ks = slice(blk * MOBA_BLOCK, (blk + 1) * MOBA_BLOCK)
        vt = jnp.concatenate([vp[blk * BLOCK_PAGES + j][0, 0] for j in range(BLOCK_PAGES)], axis=1)
        kmean = jnp.sum(kt_all[:, ks], axis=1, keepdims=True) * (1.0 / MOBA_BLOCK)
        km_ref[...] = jnp.where(lane_k == n, kmean, km_ref[...])
        s = s_all[:, ks]
        m = jnp.max(s, axis=1, keepdims=True)
        p = jnp.exp(s - m)
        m_ref[...] = jnp.where(lane_c == n, m, m_ref[...])
        l_ref[...] = jnp.where(lane_c == n, jnp.sum(p, axis=1, keepdims=True), l_ref[...])
        oacc_ref[n] = lax.dot_general(p.astype(BF16), vt.astype(BF16), nt_dims,
                                      preferred_element_type=F32)

    @pl.when(c == pl.num_programs(1) - 1)
    def _():
        gate = jnp.dot(qbtf_ref[0], km_ref[...], precision=lax.Precision.HIGHEST,
                       preferred_element_type=F32)
        sel = _top_lanes_mask(gate, n_blk, MOBA_TOPK)
        s_own = lax.dot_general(qbt, kn_ref[...].astype(BF16), nt_dims, preferred_element_type=F32)
        qidx = lax.broadcasted_iota(jnp.int32, (ncol, n_q), 0) % n_q
        kidx = lax.broadcasted_iota(jnp.int32, (ncol, n_q), 1)
        s_own = jnp.where(kidx <= qidx, s_own, NEG_INF)
        m_all = m_ref[...]
        m_fin = jnp.maximum(jnp.max(jnp.where(sel, m_all, NEG_INF), axis=1, keepdims=True),
                            jnp.max(s_own, axis=1, keepdims=True))
        w = jnp.where(sel, jnp.exp(m_all - m_fin), 0.0)
        p_own = jnp.exp(s_own - m_fin)
        l_fin = jnp.sum(w * l_ref[...], axis=1, keepdims=True) + jnp.sum(p_own, axis=1, keepdims=True)
        w = w / l_fin
        p_own = p_own / l_fin
        acc = jnp.dot(p_own, vn_ref[...], preferred_element_type=F32)
        for n in range(n_blk):
            acc = acc + w[:, n:n + 1] * oacc_ref[n]
        head = lax.broadcasted_iota(jnp.int32, (n_q, MIX_HALF), 1) // HEAD_DIM
        out = jnp.zeros((n_q, MIX_HALF), F32)
        for h in range(ATT_HEADS):
            out = out + jnp.where(head == h, acc[h * n_q:(h + 1) * n_q], 0.0)
        o_ref[...] = out


def _moba_sample(page_table, cache_k, cache_v, layer_i, q_rot, k_rot, proj, *, n_b, n_q, row0):
    n_pages = page_table.shape[1]
    n_blk = n_pages // BLOCK_PAGES
    ncol = ATT_HEADS * n_q
    assert n_blk <= LANES and n_pages % PAGES_PER_STEP == 0 and row0 % n_q == 0
    blk0 = row0 // n_q
    q4 = (q_rot[row0:row0 + n_b * n_q] * (HEAD_DIM ** -0.5)).reshape(n_b, n_q, ATT_HEADS, HEAD_DIM)
    qbtf = jnp.einsum('bihd,hg->bhigd', q4, jnp.eye(ATT_HEADS, dtype=F32)).reshape(n_b, ncol, MIX_HALF)

    def page_spec(j):
        return pl.BlockSpec((1, 1, MIX_HALF, PAGE_SIZE),
                            lambda b, c, pt, j=j: (layer_i, pt[b, c * PAGES_PER_STEP + j], 0, 0))

    per_b3 = lambda shape: pl.BlockSpec(shape, lambda b, c, pt: (b, 0, 0))
    grid_spec = pltpu.PrefetchScalarGridSpec(
        num_scalar_prefetch=1,
        grid=(n_b, n_pages // PAGES_PER_STEP),
        in_specs=[page_spec(j) for j in range(PAGES_PER_STEP)] * 2
        + [per_b3((1, ncol, MIX_HALF)), per_b3((1, ncol, MIX_HALF)),
           pl.BlockSpec((n_q, MIX_HALF), lambda b, c, pt: (blk0 + b, 0)),
           pl.BlockSpec((n_q, MIX_HALF), lambda b, c, pt: (blk0 + b, 3))],
        out_specs=pl.BlockSpec((n_q, MIX_HALF), lambda b, c, pt: (b, 0)),
        scratch_shapes=[pltpu.VMEM((n_blk, ncol, MIX_HALF), F32),
                        pltpu.VMEM((ncol, LANES), F32), pltpu.VMEM((ncol, LANES), F32),
                        pltpu.VMEM((MIX_HALF, LANES), F32)])
    return pl.pallas_call(
        functools.partial(_moba_sample_kernel, n_blk=n_blk, n_q=n_q),
        grid_spec=grid_spec,
        out_shape=jax.ShapeDtypeStruct((n_b * n_q, MIX_HALF), F32),
        compiler_params=_cparams(("arbitrary", "arbitrary")),
    )(page_table, *([cache_k] * PAGES_PER_STEP), *([cache_v] * PAGES_PER_STEP), qbtf.astype(BF16), qbtf,
      k_rot, proj)


TOK_TILE = 256
SEG_ALIGN = SUBLANES
ASG_TILE = -(-(TOK_TILE * TOPK + N_EXPERTS * (SEG_ALIGN - 1)) // LANES) * LANES
EXP_TILE = 512


def _layer_norm_rows(h, g, b):
    mu = jnp.mean(h, axis=-1, keepdims=True)
    hc = h - mu
    var = jnp.mean(hc * hc, axis=-1, keepdims=True)
    return hc * lax.rsqrt(var + LN_EPS) * g + b


def _route_kernel(x_ref, ap_ref, as_ref, bp_ref, bs_ref, wo_ref, g_ref, bt_ref, wrh_ref, wrl_ref, br_ref,
                  x1_ref, xs_ref, dg_ref, cnt_ref, *, alpha, n_prompt_tiles):
    is_sample = pl.program_id(0) >= n_prompt_tiles
    a = jnp.where(is_sample, as_ref[...], ap_ref[...])
    b = jnp.where(is_sample, bs_ref[...], bp_ref[...])
    h = (alpha * x_ref[...]
         + jnp.dot(a.astype(BF16), wo_ref[0:MIX_HALF, :], preferred_element_type=F32)
         + jnp.dot(b.astype(BF16), wo_ref[MIX_HALF:, :], preferred_element_type=F32))
    x1 = _layer_norm_rows(h, g_ref[...], bt_ref[...])
    x1_ref[...] = x1
    x1h = x1.astype(BF16)
    x1l = (x1 - x1h.astype(F32)).astype(BF16)
    nt_dims = (((1,), (1,)), ((), ()))
    logits = (lax.dot_general(wrh_ref[...], x1h, nt_dims, preferred_element_type=F32)
              + lax.dot_general(wrh_ref[...], x1l, nt_dims, preferred_element_type=F32)
              + lax.dot_general(wrl_ref[...], x1h, nt_dims, preferred_element_type=F32)) + br_ref[...]
    row = lax.broadcasted_iota(jnp.int32, logits.shape, 0)
    g = logits
    picks, vals = [], []
    for _ in range(TOPK):
        mx = jnp.max(g, axis=0, keepdims=True)
        first = jnp.min(jnp.where(g == mx, row, N_EXPERTS), axis=0, keepdims=True)
        pick = row == first
        picks.append(pick)
        vals.append(mx)
        g = jnp.where(pick, NEG_INF, g)
    es = [jnp.exp(v - vals[0]) for v in vals]
    den = es[0] + es[1] + es[2] + es[3]
    onehot = [p.astype(F32) for p in picks]
    member = onehot[0] + onehot[1] + onehot[2] + onehot[3]
    t_r = lax.broadcasted_iota(jnp.int32, (TOK_TILE, TOK_TILE), 0)
    t_c = lax.broadcasted_iota(jnp.int32, (TOK_TILE, TOK_TILE), 1)
    before = (t_r < t_c).astype(BF16)
    rank = jnp.dot(member.astype(BF16), before, preferred_element_type=F32)
    cnt = jnp.sum(member, axis=1, keepdims=True)
    cnt = jnp.ceil(cnt * (1.0 / SEG_ALIGN)) * SEG_ALIGN
    e_r = lax.broadcasted_iota(jnp.int32, (N_EXPERTS, N_EXPERTS), 0)
    e_c = lax.broadcasted_iota(jnp.int32, (N_EXPERTS, N_EXPERTS), 1)
    lower = (e_c < e_r).astype(F32)
    off = jnp.dot(lower, jnp.broadcast_to(cnt, (N_EXPERTS, TOK_TILE)), precision=lax.Precision.HIGHEST,
                  preferred_element_type=F32)
    slot = off + rank
    dests = [jnp.sum(oh * slot, axis=0, keepdims=True) for oh in onehot]
    r_iota = lax.broadcasted_iota(jnp.int32, (ASG_TILE, TOK_TILE), 0)
    perm = jnp.zeros((ASG_TILE, TOK_TILE), F32)
    for d in dests:
        perm = jnp.where(r_iota == d.astype(jnp.int32), 1.0, perm)
    xs_ref[...] = jnp.dot(perm.astype(BF16), x1h, preferred_element_type=F32)
    dg_ref[0] = jnp.concatenate(dests + [e / den for e in es], axis=0)
    cnt_ref[0] = jnp.broadcast_to(cnt, (N_EXPERTS, LANES))


def _route(x, mix_a, mix_b, w_out_bf, ln_g, ln_b, wr_t, br, *, alpha):
    t = x.shape[0]
    nt = t // TOK_TILE
    npt = mix_a[0].shape[0] // TOK_TILE
    assert mix_a[0].shape[0] % TOK_TILE == 0 and mix_a[1].shape[0] == (nt - npt) * TOK_TILE
    full2 = lambda a: pl.BlockSpec(a.shape, lambda i: (0, 0))
    prompt_rows = pl.BlockSpec((TOK_TILE, MIX_HALF), lambda i: (jnp.minimum(i, npt - 1), 0))
    sample_rows = pl.BlockSpec((TOK_TILE, MIX_HALF), lambda i: (jnp.maximum(i - npt, 0), 0))
    wr_hi = wr_t.astype(BF16)
    wr_lo = (wr_t - wr_hi.astype(F32)).astype(BF16)
    return pl.pallas_call(
        functools.partial(_route_kernel, alpha=alpha, n_prompt_tiles=npt),
        grid=(nt,),
        in_specs=[pl.BlockSpec((TOK_TILE, D_MODEL), lambda i: (i, 0)),
                  prompt_rows, sample_rows, prompt_rows, sample_rows,
                  full2(w_out_bf), full2(ln_g), full2(ln_b), full2(wr_hi), full2(wr_lo), full2(br)],
        out_specs=[pl.BlockSpec((TOK_TILE, D_MODEL), lambda i: (i, 0)),
                   pl.BlockSpec((ASG_TILE, D_MODEL), lambda i: (i, 0)),
                   pl.BlockSpec((1, 2 * TOPK, TOK_TILE), lambda i: (i, 0, 0)),
                   pl.BlockSpec((1, N_EXPERTS, LANES), lambda i: (i, 0, 0))],
        out_shape=[jax.ShapeDtypeStruct((t, D_MODEL), F32),
                   jax.ShapeDtypeStruct((nt * ASG_TILE, D_MODEL), F32),
                   jax.ShapeDtypeStruct((nt, 2 * TOPK, TOK_TILE), F32),
                   jax.ShapeDtypeStruct((nt, N_EXPERTS, LANES), F32)],
        compiler_params=_cparams(("parallel",)),
    )(x, mix_a[0], mix_a[1], mix_b[0], mix_b[1], w_out_bf, ln_g, ln_b, wr_hi, wr_lo, br)


SEG_PIECES = tuple(1 << b for b in range(TOK_TILE.bit_length() - 1, SEG_ALIGN.bit_length() - 2, -1))
TILE_PIECES = tuple(1 << b for b in range(EXP_TILE.bit_length() - 1, SEG_ALIGN.bit_length() - 2, -1))


def _expert_kernel(te_ref, nu_ref, lo_ref, hi_ref, valid_ref, src_ref, dst_ref, len_ref, tot_ref,
                   xs_ref, wg_ref, bg_ref, wu_ref, bu_ref, wd_ref, bd_ref, ys_ref,
                   xbuf, ybuf, zbuf, wgb_ref, wub_ref, wdb_ref, gsem, ssem, zsem, *, n_tok_tiles):
    m = pl.program_id(0)
    n_used = nu_ref[0]

    def zero_tail(i, start):
        n = ASG_TILE - tot_ref[i]
        for size in SEG_PIECES:
            @pl.when((n & size) != 0)
            def _():
                row = pl.multiple_of(i * ASG_TILE + tot_ref[i] + (n & ~(2 * size - 1)), SEG_ALIGN)
                cp = pltpu.make_async_copy(zbuf.at[pl.ds(0, size)], ys_ref.at[pl.ds(row, size)], zsem)
                cp.start() if start else cp.wait()

    @pl.when(m == 0)
    def _():
        zbuf[...] = jnp.zeros(zbuf.shape, F32)

    @pl.when((m >= 1) & (m - 1 < n_tok_tiles))
    def _():
        zero_tail(m - 1, False)

    @pl.when(m < n_tok_tiles)
    def _():
        zero_tail(m, True)

    def for_pieces(t, fn):
        row0 = t * EXP_TILE

        def seg_body(s, _):
            start = jnp.maximum(dst_ref[s], row0)
            n = jnp.minimum(dst_ref[s] + len_ref[s], row0 + EXP_TILE) - start
            base_src = src_ref[s] + (start - dst_ref[s])
            base_dst = start - row0
            for size in SEG_PIECES:
                @pl.when((n & size) != 0)
                def _():
                    done = n & ~(2 * size - 1)
                    fn(pl.multiple_of(base_src + done, SEG_ALIGN), pl.multiple_of(base_dst + done, SEG_ALIGN), size)
            return 0

        lax.fori_loop(lo_ref[t], hi_ref[t], seg_body, 0)

    def wait_rows(n, copy_of):
        for size in TILE_PIECES:
            @pl.when((n & size) != 0)
            def _():
                copy_of(size).wait()

    def gather(t, start):
        slot = t % 2
        if start:
            for_pieces(t, lambda row, r, size: pltpu.make_async_copy(
                xs_ref.at[pl.ds(row, size)], xbuf.at[slot, pl.ds(r, size)], gsem.at[slot]).start())
        else:
            wait_rows(valid_ref[t], lambda size: pltpu.make_async_copy(
                xs_ref.at[pl.ds(0, size)], xbuf.at[slot, pl.ds(0, size)], gsem.at[slot]))

    def scatter(t, start):
        slot = t % 2
        if start:
            for_pieces(t, lambda row, r, size: pltpu.make_async_copy(
                ybuf.at[slot, pl.ds(r, size)], ys_ref.at[pl.ds(row, size)], ssem.at[slot]).start())
        else:
            wait_rows(valid_ref[t], lambda size: pltpu.make_async_copy(
                ybuf.at[slot, pl.ds(0, size)], ys_ref.at[pl.ds(0, size)], ssem.at[slot]))

    @pl.when((m == 0) & (n_used > 0))
    def _():
        gather(0, True)

    @pl.when(m + 1 < n_used)
    def _():
        gather(m + 1, True)

    @pl.when((m >= 2) & (m - 2 < n_used))
    def _():
        scatter(m - 2, False)

    @pl.when(m < n_used)
    def _():
        gather(m, False)
        prev = te_ref[jnp.maximum(m - 1, 0)]

        @pl.when((m == 0) | (te_ref[m] != prev))
        def _():
            wgb_ref[...] = wg_ref[0, 0].astype(BF16)
            wub_ref[...] = wu_ref[0, 0].astype(BF16)
            wdb_ref[...] = wd_ref[0, 0].astype(BF16)

        slot = m % 2
        xb = xbuf[slot].astype(BF16)
        gt = jnp.minimum(jnp.dot(xb, wgb_ref[...], preferred_element_type=F32) + bg_ref[0, 0], SWIGLU_LIMIT)
        up = jnp.clip(jnp.dot(xb, wub_ref[...], preferred_element_type=F32) + bu_ref[0, 0],
                      -SWIGLU_LIMIT, SWIGLU_LIMIT)
        act = gt * _sigmoid(SWIGLU_ALPHA * gt) * (up + 1.0)
        ybuf[slot] = jnp.dot(act.astype(BF16), wdb_ref[...], preferred_element_type=F32) + bd_ref[0, 0]
        scatter(m, True)


def _experts(tile_expert, n_used, seg_lo, seg_hi, tile_valid, seg_src, seg_dst, seg_len, tile_total, xs, layer,
             w_gate, b_gate, w_up, b_up, w_down, b_down):
    n_tiles = tile_expert.shape[0]
    n_tok_tiles = tile_total.shape[0]
    assert n_tiles > n_tok_tiles
    wspec = pl.BlockSpec((1, 1, D_MODEL, D_MODEL), lambda m, te, *_: (layer, te[m], 0, 0))
    bspec = pl.BlockSpec((1, 1, 1, D_MODEL), lambda m, te, *_: (layer, te[m], 0, 0))
    grid_spec = pltpu.PrefetchScalarGridSpec(
        num_scalar_prefetch=9, grid=(n_tiles,),
        in_specs=[pl.BlockSpec(memory_space=pl.ANY), wspec, bspec, wspec, bspec, wspec, bspec],
        out_specs=pl.BlockSpec(memory_space=pl.ANY),
        scratch_shapes=[pltpu.VMEM((2, EXP_TILE, D_MODEL), F32), pltpu.VMEM((2, EXP_TILE, D_MODEL), F32),
                        pltpu.VMEM((TOK_TILE, D_MODEL), F32)]
        + [pltpu.VMEM((D_MODEL, D_MODEL), BF16)] * 3
        + [pltpu.SemaphoreType.DMA((2,)), pltpu.SemaphoreType.DMA((2,)), pltpu.SemaphoreType.DMA(())])
    depth = w_gate.shape[0]
    bshape = (depth, N_EXPERTS, 1, D_MODEL)
    return pl.pallas_call(
        functools.partial(_expert_kernel, n_tok_tiles=n_tok_tiles),
        grid_spec=grid_spec,
        out_shape=jax.ShapeDtypeStruct(xs.shape, F32),
        compiler_params=_cparams(("arbitrary",)),
    )(tile_expert, n_used, seg_lo, seg_hi, tile_valid, seg_src, seg_dst, seg_len, tile_total, xs,
      w_gate, b_gate.reshape(bshape), w_up, b_up.reshape(bshape), w_down, b_down.reshape(bshape))


def _combine_kernel(x1_ref, ys_ref, dg_ref, g_ref, bt_ref, o_ref, *, alpha):
    dg = dg_ref[0]
    r_iota = lax.broadcasted_iota(jnp.int32, (ASG_TILE, TOK_TILE), 0)
    comb = jnp.zeros((ASG_TILE, TOK_TILE), F32)
    for k in range(TOPK):
        comb = jnp.where(r_iota == dg[k:k + 1, :].astype(jnp.int32), dg[TOPK + k:TOPK + k + 1, :], comb)
    ffn = lax.dot_general(comb.astype(BF16), ys_ref[...].astype(BF16), (((0,), (0,)), ((), ())),
                          preferred_element_type=F32)
    o_ref[...] = _layer_norm_rows(alpha * x1_ref[...] + ffn, g_ref[...], bt_ref[...])


def _combine(x1, ys, dg, ln_g, ln_b, *, alpha):
    t = x1.shape[0]
    nt = t // TOK_TILE
    return pl.pallas_call(
        functools.partial(_combine_kernel, alpha=alpha),
        grid=(nt,),
        in_specs=[pl.BlockSpec((TOK_TILE, D_MODEL), lambda i: (i, 0)),
                  pl.BlockSpec((ASG_TILE, D_MODEL), lambda i: (i, 0)),
                  pl.BlockSpec((1, 2 * TOPK, TOK_TILE), lambda i: (i, 0, 0)),
                  pl.BlockSpec((1, D_MODEL), lambda i: (0, 0)),
                  pl.BlockSpec((1, D_MODEL), lambda i: (0, 0))],
        out_specs=pl.BlockSpec((TOK_TILE, D_MODEL), lambda i: (i, 0)),
        out_shape=jax.ShapeDtypeStruct((t, D_MODEL), F32),
        compiler_params=_cparams(("parallel",)),
    )(x1, ys, dg, ln_g, ln_b)


def _channel_mix(x, mix_a, mix_b, p, layer, alpha):
    t = x.shape[0]
    nt = t // TOK_TILE
    x1, xs, dg, cnt = _route(
        x, mix_a, mix_b, p['w_out'][layer].astype(BF16), p['ln_g'][layer, 0].reshape(1, D_MODEL),
        p['ln_b'][layer, 0].reshape(1, D_MODEL), p['router_w'][layer].T.astype(F32),
        p['router_b'][layer].reshape(N_EXPERTS, 1).astype(F32), alpha=alpha)
    cnt = cnt[:, :, 0].astype(jnp.int32)
    local_off = jnp.cumsum(cnt, axis=1) - cnt
    tile_rows = (jnp.arange(nt, dtype=jnp.int32) * ASG_TILE)[:, None] + local_off
    total = jnp.sum(cnt, axis=0)
    padded = (total + EXP_TILE - 1) // EXP_TILE * EXP_TILE
    pend = jnp.cumsum(padded)
    expert_rows = (pend - padded)[None, :] + jnp.cumsum(cnt, axis=0) - cnt
    n_tiles = -(-(nt * (TOK_TILE * TOPK + N_EXPERTS * (SEG_ALIGN - 1))) // EXP_TILE) + N_EXPERTS + 2
    tile_start = (jnp.arange(n_tiles, dtype=jnp.int32) * EXP_TILE)[:, None]
    count_below = lambda a, bound: jnp.sum((a[None, :] < bound).astype(jnp.int32), axis=1)
    tile_expert = jnp.minimum(count_below(pend, tile_start + 1), N_EXPERTS - 1)
    n_used = (pend[-1:] // EXP_TILE).astype(jnp.int32)
    seg_src = tile_rows.T.reshape(-1)
    seg_dst = expert_rows.T.reshape(-1).astype(jnp.int32)
    seg_len = cnt.T.reshape(-1)
    seg_lo = count_below(seg_dst + seg_len, tile_start + 1)
    seg_hi = count_below(seg_dst, tile_start + EXP_TILE)
    expert_end = (pend - padded + total)[tile_expert]
    tile_valid = jnp.clip(expert_end - tile_start[:, 0], 0, EXP_TILE).astype(jnp.int32)
    ys = _experts(tile_expert, n_used, seg_lo, seg_hi, tile_valid, seg_src, seg_dst, seg_len, jnp.sum(cnt, axis=1),
                  xs, layer,
                  p['moe_w_gate'], p['moe_b_gate'], p['moe_w_up'], p['moe_b_up'], p['moe_w_down'], p['moe_b_down'])
    return _combine(x1, ys, dg, p['ln_g'][layer, 1].reshape(1, D_MODEL), p['ln_b'][layer, 1].reshape(1, D_MODEL),
                    alpha=alpha)


def kernel(x_prompt, x_sample, state_s5, state_pool, cache_k, cache_v, page_table, w_in_ab, s5_lambda_re, s5_lambda_im, s5_b_re, s5_b_im, s5_c_re, s5_c_im, s5_d, s5_log_dt, s5_w_glu, s5_b_glu, gm_norm_g, gm_w_s, gm_b_s, w_in_cd, pool_w, pool_scale, w_out, ln_g, ln_b, router_w, router_b, moe_w_gate, moe_b_gate, moe_w_up, moe_b_up, moe_w_down, moe_b_down):
    p = dict(w_in_ab=w_in_ab, s5_lambda_re=s5_lambda_re, s5_lambda_im=s5_lambda_im,
             s5_b_re=s5_b_re, s5_b_im=s5_b_im, s5_c_re=s5_c_re, s5_c_im=s5_c_im, s5_d=s5_d,
             s5_log_dt=s5_log_dt, s5_w_glu=s5_w_glu, s5_b_glu=s5_b_glu, gm_norm_g=gm_norm_g,
             gm_w_s=gm_w_s, gm_b_s=gm_b_s, w_in_cd=w_in_cd, pool_w=pool_w, pool_scale=pool_scale,
             w_out=w_out, ln_g=ln_g, ln_b=ln_b, router_w=router_w, router_b=router_b,
             moe_w_gate=moe_w_gate, moe_b_gate=moe_b_gate, moe_w_up=moe_w_up, moe_b_up=moe_b_up,
             moe_w_down=moe_w_down, moe_b_down=moe_b_down)
    n_bp, n_sp, _ = x_prompt.shape
    n_bs, n_ss, _ = x_sample.shape
    t_p, t_s = n_bp * n_sp, n_bs * n_ss
    depth = w_out.shape[0]
    alpha = (2 * depth) ** 0.25
    past_len = page_table.shape[1] * PAGE_SIZE
    x = jnp.concatenate([x_prompt.reshape(t_p, D_MODEL), x_sample.reshape(t_s, D_MODEL)], axis=0)
    zero_s5 = jnp.zeros((n_bp, S5_GROUPS, S5_STATE, 2), F32)
    pos = jnp.concatenate([jnp.tile(jnp.arange(n_sp), n_bp), jnp.tile(past_len + jnp.arange(n_ss), n_bs)])
    rope_tables = _rope_tables(pos)
    cache_k2 = cache_k.transpose(0, 1, 3, 4, 2).reshape(cache_k.shape[0], cache_k.shape[1], MIX_HALF, PAGE_SIZE)
    cache_v2 = cache_v.transpose(0, 1, 3, 4, 2).reshape(cache_v.shape[0], cache_v.shape[1], MIX_HALF, PAGE_SIZE)
    s5_p, s5_s, gmv_s, pool_p, pool_s, k_p, v_p, k_s, v_s = [], [], [], [], [], [], [], [], []
    for layer in range(depth):
        i = layer // 2
        if layer % 2 == 0:
            proj = _proj(x, w_in_ab[i].astype(BF16), PROJ_TILE)
            a_p, b_p, _, st_p = _even_layer_mix(proj, zero_s5, p, i, n_b=n_bp, n_s=n_sp, sample=False)
            a_s, b_s, vn, st_s = _even_layer_mix(proj, state_s5[i], p, i, n_b=n_bs, n_s=n_ss, sample=True, row0=t_p)
            s5_p.append(st_p)
            s5_s.append(st_s)
            gmv_s.append(vn.reshape(n_bs, n_ss, MIX_HALF))
        else:
            proj = _proj(x, w_in_cd[i].astype(BF16), PROJ_TILE)
            pw = pool_w[i].astype(BF16)
            ps = pool_scale[i].reshape(1, MIX_HALF).astype(F32)
            a_p = _pool_mixer(proj, jnp.zeros((n_bp, POOL_HALO, MIX_HALF), F32), pw, ps, n_b=n_bp, n_s=n_sp, base=0)
            halo = jnp.concatenate([jnp.zeros((n_bs, POOL_HALO - POOL_BUF, MIX_HALF), F32),
                                    state_pool[i].astype(F32)], axis=1)
            a_s = _pool_mixer(proj, halo, pw, ps, n_b=n_bs, n_s=n_ss, base=POOL_BUF, row0=t_p)
            q_rot, k_rot, qb, kb, vb, kmean = _rope(proj, rope_tables)
            b_p = _moba_prompt(q_rot, qb, kb, vb, kmean, n_b=n_bp, n_s=n_sp)
            b_s = _moba_sample(page_table, cache_k2, cache_v2, i, q_rot, k_rot, proj, n_b=n_bs, n_q=n_ss, row0=t_p)
            c_p = proj[:t_p, :MIX_HALF].reshape(n_bp, n_sp, MIX_HALF)
            c_s = proj[t_p:, :MIX_HALF].reshape(n_bs, n_ss, MIX_HALF)
            pool_p.append(c_p[:, -POOL_BUF:])
            pool_s.append(jnp.concatenate([state_pool[i].astype(F32), c_s], axis=1)[:, -POOL_BUF:])
            k_p.append(k_rot[:t_p].reshape(n_bp, n_sp, ATT_HEADS, HEAD_DIM))
            k_s.append(k_rot[t_p:].reshape(n_bs, n_ss, ATT_HEADS, HEAD_DIM))
            v_p.append(proj[:t_p, 3 * MIX_HALF:].reshape(n_bp, n_sp, ATT_HEADS, HEAD_DIM))
            v_s.append(proj[t_p:, 3 * MIX_HALF:].reshape(n_bs, n_ss, ATT_HEADS, HEAD_DIM))
        x = _channel_mix(x, (a_p, a_s), (b_p, b_s), p, layer, alpha)
    return (x[:t_p].reshape(n_bp, n_sp, D_MODEL), x[t_p:].reshape(n_bs, n_ss, D_MODEL),
            jnp.stack(s5_p), jnp.stack(s5_s), jnp.stack(gmv_s), jnp.stack(pool_p), jnp.stack(pool_s),
            jnp.stack(k_p), jnp.stack(v_p), jnp.stack(k_s), jnp.stack(v_s))
```

```python
import functools
import math

import jax
import jax.numpy as jnp
from jax import lax
from jax.experimental import pallas as pl
from jax.experimental.pallas import tpu as pltpu

F32 = jnp.float32
BF16 = jnp.bfloat16

D_MODEL = 1024
MIX_HALF = D_MODEL // 2
S5_GROUP_CH = 16
S5_GROUPS = MIX_HALF // S5_GROUP_CH
S5_STATE = 64
GM_CHUNK = 128
GM_GROUPS = 4
GM_CH = MIX_HALF // GM_GROUPS
POOL_WINDOWS = (2, 4, 8, 16)
POOL_CH = MIX_HALF // len(POOL_WINDOWS)
POOL_BUF = max(POOL_WINDOWS) - 1
ATT_HEADS = 8
HEAD_DIM = MIX_HALF // ATT_HEADS
ROT_DIM = HEAD_DIM // 4
ROPE_THETA = 500000.0
MOBA_BLOCK = 256
MOBA_TOPK = 3
N_EXPERTS = 32
TOPK = 4
SWIGLU_LIMIT = 7.0
SWIGLU_ALPHA = 1.702
LN_EPS = 1e-5
PAGE_SIZE = 128

LANES = 128
SUBLANES = 8
VMEM_LIMIT = 56 * 1024 * 1024

S5_OCT = 4
S5_PAIRS = S5_GROUPS * S5_STATE // LANES
NEG_INF = float("-inf")


def _cparams(sem):
    return pltpu.CompilerParams(dimension_semantics=sem, vmem_limit_bytes=VMEM_LIMIT)


def _gelu(x):
    return 0.5 * x * (1.0 + jnp.tanh(math.sqrt(2.0 / math.pi) * (x + 0.044715 * (x * x * x))))


def _sigmoid(x):
    return 1.0 / (1.0 + jnp.exp(-x))


PROJ_TILE = 640


def _proj_kernel(x_ref, w_ref, o_ref):
    o_ref[...] = jnp.dot(x_ref[...].astype(BF16), w_ref[...], preferred_element_type=F32)


def _proj(x, w_bf16, tm):
    t, k = x.shape
    n = w_bf16.shape[1]
    return pl.pallas_call(
        _proj_kernel,
        grid=(t // tm,),
        in_specs=[pl.BlockSpec((tm, k), lambda i: (i, 0)),
                  pl.BlockSpec((k, n), lambda i: (0, 0))],
        out_specs=pl.BlockSpec((tm, n), lambda i: (i, 0)),
        out_shape=jax.ShapeDtypeStruct((t, n), F32),
        compiler_params=_cparams(("parallel",)),
    )(x, w_bf16)


def _s5_params(lam_re, lam_im, b_re, b_im, c_re, c_im, log_dt):
    dt = jnp.exp(log_dt.astype(F32))[:, None]
    lam = lax.complex(lam_re.astype(F32), lam_im.astype(F32))
    lam_bar = jnp.exp(lam * dt)
    b_bar = ((lam_bar - 1.0) / lam)[..., None] * lax.complex(b_re.astype(F32), b_im.astype(F32))
    eye = jnp.eye(SUBLANES, dtype=F32)
    bb = b_bar.reshape(S5_OCT, 8, S5_STATE, S5_GROUP_CH)

    def bdiag_b(t):
        return jnp.einsum('qgph,gk->qghkp', t, eye).reshape(S5_OCT, 128, 512)

    bw = jnp.concatenate([bdiag_b(bb.real), bdiag_b(bb.imag)], axis=-1).astype(BF16)
    cc_re = c_re.astype(F32).reshape(S5_OCT, 8, S5_GROUP_CH, S5_STATE)
    cc_im = c_im.astype(F32).reshape(S5_OCT, 8, S5_GROUP_CH, S5_STATE)

    def bdiag_c(t):
        return jnp.einsum('qghp,gk->qgpkh', t, eye).reshape(S5_OCT, 512, 128)

    cw = jnp.concatenate([bdiag_c(cc_re), -bdiag_c(cc_im)], axis=1).astype(BF16)
    rows = jnp.arange(SUBLANES)
    planes = []
    for d in (1, 2, 4):
        pw = jnp.exp(lam * dt * float(d)).reshape(S5_PAIRS, 1, LANES)
        m = (rows >= d).astype(F32)[None, :, None]
        planes += [pw.real * m, pw.imag * m]
    pw = jnp.exp((lam * dt).reshape(S5_PAIRS, 1, LANES) * (rows + 1).astype(F32)[None, :, None])
    planes += [pw.real, pw.imag]
    coef = jnp.stack(planes, axis=1).astype(F32)
    return bw, cw, coef


def _even_kernel(u_ref, gu_ref, gv_ref, x0_ref, bw_ref, coef_ref, cw_ref, d_ref, wglu_ref, bglu_ref,
                 ng_ref, m_ref, bias_ref, *rest, ts, chunk, per_block_init, with_vn):
    if with_vn:
        a_ref, b_ref, vn_ref, st_out_ref, st_ref, carry_ref = rest
    else:
        a_ref, b_ref, st_out_ref, st_ref, carry_ref = rest
        vn_ref = None
    n_rb = ts // SUBLANES

    if not per_block_init:
        @pl.when(pl.program_id(1) == 0)
        def _():
            carry_ref[...] = x0_ref[0]

    u = u_ref[...]
    ub = u.astype(BF16)
    for q in range(S5_OCT):
        bu = jnp.dot(ub[:, q * 128:(q + 1) * 128], bw_ref[q], preferred_element_type=F32)
        for c in range(4):
            st_ref[q * 4 + c] = bu[:, c * 128:(c + 1) * 128]
            st_ref[S5_PAIRS + q * 4 + c] = bu[:, 512 + c * 128:512 + (c + 1) * 128]

    def pair_body(j, _):
        cf = coef_ref[j]
        a1r, a1i, a2r, a2i, a4r, a4i, pr, pi = [cf[k] for k in range(8)]

        def rb_body(r, carry):
            cr, ci = carry
            row = pl.multiple_of(r * SUBLANES, SUBLANES)
            xr = st_ref[j, pl.ds(row, SUBLANES), :]
            xi = st_ref[S5_PAIRS + j, pl.ds(row, SUBLANES), :]
            for d, ar, ai in ((1, a1r, a1i), (2, a2r, a2i), (4, a4r, a4i)):
                sr = pltpu.roll(xr, d, 0)
                si = pltpu.roll(xi, d, 0)
                xr, xi = xr + ar * sr - ai * si, xi + ar * si + ai * sr
            if per_block_init:
                cr = x0_ref[j, r]
                ci = x0_ref[S5_PAIRS + j, r]
            xr, xi = xr + pr * cr - pi * ci, xi + pr * ci + pi * cr
            st_ref[j, pl.ds(row, SUBLANES), :] = xr
            st_ref[S5_PAIRS + j, pl.ds(row, SUBLANES), :] = xi
            ncr = jnp.broadcast_to(xr[SUBLANES - 1:SUBLANES, :], (SUBLANES, LANES))
            nci = jnp.broadcast_to(xi[SUBLANES - 1:SUBLANES, :], (SUBLANES, LANES))
            if per_block_init:
                st_out_ref[j, r] = ncr
                st_out_ref[S5_PAIRS + j, r] = nci
            return ncr, nci

        cr, ci = lax.fori_loop(0, n_rb, rb_body, (carry_ref[j], carry_ref[S5_PAIRS + j]),
                               unroll=min(16, n_rb))
        carry_ref[j] = cr
        carry_ref[S5_PAIRS + j] = ci
        return 0

    lax.fori_loop(0, S5_PAIRS, pair_body, 0)

    if not per_block_init:
        @pl.when(pl.program_id(1) == pl.num_programs(1) - 1)
        def _():
            st_out_ref[0] = carry_ref[...]

    ys = []
    for q in range(S5_OCT):
        xq = jnp.concatenate([st_ref[q * 4 + c] for c in range(4)]
                             + [st_ref[S5_PAIRS + q * 4 + c] for c in range(4)], axis=-1)
        ys.append(jnp.dot(xq.astype(BF16), cw_ref[q], preferred_element_type=F32))
    y = jnp.concatenate(ys, axis=-1) + d_ref[...] * u
    g = _gelu(y)
    z = jnp.dot(g.astype(BF16), wglu_ref[...], preferred_element_type=F32) + bglu_ref[...]
    a_ref[...] = g * _sigmoid(z)

    gu = _gelu(gu_ref[...])
    gv = _gelu(gv_ref[...])
    for gi in range(GM_GROUPS):
        sl = slice(gi * GM_CH, (gi + 1) * GM_CH)
        v = gv[:, sl]
        mu = jnp.mean(v, axis=-1, keepdims=True)
        vc = v - mu
        var = jnp.mean(vc * vc, axis=-1, keepdims=True)
        vn = vc * lax.rsqrt(var + LN_EPS) * ng_ref[:, sl]
        if with_vn:
            vn_ref[:, sl] = vn
        vnb = vn.astype(BF16)
        for c in range(ts // chunk):
            rs = slice(c * chunk, (c + 1) * chunk)
            s = jnp.dot(m_ref[gi], vnb[rs], preferred_element_type=F32) + bias_ref[:, sl]
            b_ref[rs, sl] = gu[rs, sl] * s


def _even_mixer(proj, x0, s5p, d_skip, w_glu, b_glu, norm_g, m_mix, bias, *, n_b, n_s, per_block_init, row0=0):
    bw, cw, coef = s5p
    if per_block_init:
        ts, grid, chunk = n_b * n_s, (1, 1), n_b * n_s
        assert n_s == SUBLANES
        n_rb = ts // SUBLANES
        x0_spec = pl.BlockSpec((2 * S5_PAIRS, n_rb, SUBLANES, LANES), lambda b, t: (0, 0, 0, 0))
        st_shape = (2 * S5_PAIRS, n_rb, SUBLANES, LANES)
        st_spec = x0_spec
    else:
        ts = min(512, n_s)
        grid, chunk = (n_b, n_s // ts), GM_CHUNK
        x0_spec = pl.BlockSpec((1, 2 * S5_PAIRS, SUBLANES, LANES), lambda b, t: (b, 0, 0, 0))
        st_shape = (n_b, 2 * S5_PAIRS, SUBLANES, LANES)
        st_spec = x0_spec
    nt = grid[1]
    with_vn = per_block_init
    blk0 = row0 // ts
    assert row0 % ts == 0

    def rows(col):
        return pl.BlockSpec((ts, MIX_HALF), lambda b, t, col=col: (blk0 + b * nt + t, col))

    def full(a):
        return pl.BlockSpec(a.shape, lambda b, t, nd=a.ndim: (0,) * nd)

    row_out = pl.BlockSpec((ts, MIX_HALF), lambda b, t: (b * nt + t, 0))
    rows_shape = jax.ShapeDtypeStruct((n_b * n_s, MIX_HALF), F32)
    n_row_outs = 3 if with_vn else 2
    out_shape = [rows_shape] * n_row_outs + [jax.ShapeDtypeStruct(st_shape, F32)]
    out_specs = [row_out] * n_row_outs + [st_spec]
    weights = (bw, coef, cw, d_skip, w_glu, b_glu, norm_g, m_mix, bias)
    outs = pl.pallas_call(
        functools.partial(_even_kernel, ts=ts, chunk=chunk, per_block_init=per_block_init, with_vn=with_vn),
        grid=grid,
        in_specs=[rows(0), rows(1), rows(2), x0_spec] + [full(w) for w in weights],
        out_specs=out_specs,
        out_shape=out_shape,
        scratch_shapes=[pltpu.VMEM((2 * S5_PAIRS, ts, LANES), F32),
                        pltpu.VMEM((2 * S5_PAIRS, SUBLANES, LANES), F32)],
        compiler_params=_cparams(("arbitrary", "arbitrary")),
    )(proj, proj, proj, x0, *weights)
    if with_vn:
        return outs[0], outs[1], outs[2], outs[3]
    return outs[0], outs[1], None, outs[2]


def _state_to_lanes(x0, n_b):
    re = x0[..., 0].astype(F32).reshape(n_b, S5_PAIRS, LANES)
    im = x0[..., 1].astype(F32).reshape(n_b, S5_PAIRS, LANES)
    return jnp.concatenate([re, im], axis=1)


def _lanes_to_state(st, n_b):
    re = st[:, :S5_PAIRS].reshape(n_b, S5_GROUPS, S5_STATE)
    im = st[:, S5_PAIRS:].reshape(n_b, S5_GROUPS, S5_STATE)
    return jnp.stack([re, im], axis=-1)


def _even_layer_mix(proj, x0, p, i, *, n_b, n_s, sample, row0=0):
    s5p = _s5_params(p['s5_lambda_re'][i], p['s5_lambda_im'][i], p['s5_b_re'][i], p['s5_b_im'][i],
                     p['s5_c_re'][i], p['s5_c_im'][i], p['s5_log_dt'][i])
    st0 = _state_to_lanes(x0, n_b)
    if sample:
        chunk = n_s
        x0k = jnp.broadcast_to(st0.transpose(1, 0, 2)[:, :, None, :], (2 * S5_PAIRS, n_b, SUBLANES, LANES))
        w = jnp.tril(p['gm_w_s'][i][:, :chunk, :chunk])
        m_mix = jnp.einsum('bc,gij->gbicj', jnp.eye(n_b, dtype=F32), w).reshape(GM_GROUPS, n_b * chunk, n_b * chunk)
        bias_rows = jnp.tile(p['gm_b_s'][i][:, :chunk].T, (n_b, 1))
    else:
        x0k = jnp.broadcast_to(st0[:, :, None, :], (n_b, 2 * S5_PAIRS, SUBLANES, LANES))
        m_mix = jnp.tril(p['gm_w_s'][i][:, :GM_CHUNK, :GM_CHUNK])
        bias_rows = p['gm_b_s'][i][:, :GM_CHUNK].T
    bias = jnp.repeat(bias_rows.astype(F32), GM_CH, axis=1)
    a, b, vn, st = _even_mixer(
        proj, x0k, s5p, p['s5_d'][i].reshape(1, MIX_HALF).astype(F32), p['s5_w_glu'][i].astype(BF16),
        p['s5_b_glu'][i].reshape(1, MIX_HALF).astype(F32), p['gm_norm_g'][i].reshape(1, MIX_HALF).astype(F32),
        m_mix.astype(BF16), bias, n_b=n_b, n_s=n_s, per_block_init=sample, row0=row0)
    if sample:
        st = st[:, :, 0, :].transpose(1, 0, 2)
    else:
        st = st[:, :, 0, :]
    return a, b, vn, _lanes_to_state(st, n_b)


POOL_HALO = 16


def _pool_kernel(c_ref, halo_ref, w_ref, scale_ref, o_ref, hist_ref, *, ts, base):
    t = pl.program_id(1)

    @pl.when(t == 0)
    def _():
        hist_ref[0:POOL_HALO, :] = halo_ref[0]

    @pl.when(t > 0)
    def _():
        hist_ref[0:POOL_HALO, :] = hist_ref[ts:ts + POOL_HALO, :]

    hist_ref[POOL_HALO:POOL_HALO + ts, :] = c_ref[...]
    pos = base + t * ts + lax.broadcasted_iota(jnp.int32, (ts, 1), 0)
    for g, win in enumerate(POOL_WINDOWS):
        sl = slice(g * POOL_CH, (g + 1) * POOL_CH)
        x = hist_ref[POOL_HALO:POOL_HALO + ts, sl]
        acc = x
        for d in range(1, win):
            acc = acc + hist_ref[POOL_HALO - d:POOL_HALO - d + ts, sl]
        cnt = jnp.minimum(pos + 1, win).astype(F32)
        pooled = acc / cnt - x
        y = jnp.dot(pooled.astype(BF16), w_ref[g], preferred_element_type=F32)
        o_ref[:, sl] = y * scale_ref[:, sl]


def _pool_mixer(proj, halo, w, scale, *, n_b, n_s, base, row0=0):
    ts = min(512, n_s)
    nt = n_s // ts
    blk0 = row0 // ts
    assert row0 % ts == 0
    return pl.pallas_call(
        functools.partial(_pool_kernel, ts=ts, base=base),
        grid=(n_b, nt),
        in_specs=[pl.BlockSpec((ts, MIX_HALF), lambda b, t: (blk0 + b * nt + t, 0)),
                  pl.BlockSpec((1, POOL_HALO, MIX_HALF), lambda b, t: (b, 0, 0)),
                  pl.BlockSpec(w.shape, lambda b, t: (0, 0, 0)),
                  pl.BlockSpec((1, MIX_HALF), lambda b, t: (0, 0))],
        out_specs=pl.BlockSpec((ts, MIX_HALF), lambda b, t: (b * nt + t, 0)),
        out_shape=jax.ShapeDtypeStruct((n_b * n_s, MIX_HALF), F32),
        scratch_shapes=[pltpu.VMEM((POOL_HALO + ts, MIX_HALF), F32)],
        compiler_params=_cparams(("arbitrary", "arbitrary")),
    )(proj, halo, w, scale)


def _rope_tables(pos):
    half = ROT_DIM // 2
    inv = ROPE_THETA ** (-jnp.arange(half, dtype=F32) * 2.0 / ROT_DIM)
    ang = pos.astype(F32)[:, None] * inv[None, :]
    cos, sin = jnp.cos(ang), jnp.sin(ang)
    n = pos.shape[0]
    one = jnp.ones((n, HEAD_DIM - ROT_DIM), F32)
    zero = jnp.zeros((n, HEAD_DIM - ROT_DIM), F32)
    z8 = jnp.zeros((n, half), F32)
    ca = jnp.concatenate([cos, cos, one], axis=1)
    sp = jnp.concatenate([z8, sin, zero], axis=1)
    sm = jnp.concatenate([-sin, z8, zero], axis=1)
    return tuple(jnp.tile(t, (1, LANES // HEAD_DIM)) for t in (ca, sp, sm))


def _rope_kernel(q_ref, k_ref, v_ref, ca_ref, sp_ref, sm_ref, qo_ref, ko_ref, qb_ref, kb_ref, vb_ref, km_ref):
    ca, sp, sm = ca_ref[...], sp_ref[...], sm_ref[...]
    half = ROT_DIM // 2
    for c in range(MIX_HALF // LANES):
        sl = slice(c * LANES, (c + 1) * LANES)
        for src, dst in ((q_ref, qo_ref), (k_ref, ko_ref)):
            x = src[:, sl]
            dst[:, sl] = x * ca + pltpu.roll(x, half, 1) * sp + pltpu.roll(x, LANES - half, 1) * sm
    q = qo_ref[...]
    k = ko_ref[...]
    qb_ref[...] = (q * (HEAD_DIM ** -0.5)).astype(BF16)
    kb_ref[...] = k.astype(BF16)
    vb_ref[...] = v_ref[...].astype(BF16)
    km_ref[0] = jnp.mean(k, axis=0, keepdims=True)


def _rope(proj, tables):
    t_rows = proj.shape[0]
    ts = MOBA_BLOCK

    def col(c):
        return pl.BlockSpec((ts, MIX_HALF), lambda i, c=c: (i, c))

    tab = pl.BlockSpec((ts, LANES), lambda i: (i, 0))
    row = pl.BlockSpec((ts, MIX_HALF), lambda i: (i, 0))
    f32o = jax.ShapeDtypeStruct((t_rows, MIX_HALF), F32)
    bfo = jax.ShapeDtypeStruct((t_rows, MIX_HALF), BF16)
    outs = pl.pallas_call(
        _rope_kernel,
        grid=(t_rows // ts,),
        in_specs=[col(1), col(2), col(3), tab, tab, tab],
        out_specs=[row, row, row, row, row, pl.BlockSpec((1, 1, MIX_HALF), lambda i: (i, 0, 0))],
        out_shape=[f32o, f32o, bfo, bfo, bfo, jax.ShapeDtypeStruct((t_rows // ts, 1, MIX_HALF), F32)],
        compiler_params=_cparams(("parallel",)),
    )(proj, proj, proj, *tables)
    return list(outs[:5]) + [outs[5].reshape(t_rows // ts, MIX_HALF)]


def _top_rows_mask(gate, n_valid_rows, k_top):
    n = gate.shape[0]
    row = lax.broadcasted_iota(jnp.int32, gate.shape, 0)
    live = row < n_valid_rows
    sel = jnp.zeros(gate.shape, jnp.bool_)
    for _ in range(k_top):
        g = jnp.where(live, gate, NEG_INF)
        mx = jnp.max(g, axis=0, keepdims=True)
        first = jnp.min(jnp.where(live & (g == mx), row, n), axis=0, keepdims=True)
        pick = row == first
        sel = sel | pick
        live = live & jnp.logical_not(pick)
    return sel


def _top_lanes_mask(gate, n_valid, k_top):
    n = gate.shape[1]
    lane = lax.broadcasted_iota(jnp.int32, gate.shape, 1)
    live = lane < n_valid
    sel = jnp.zeros(gate.shape, jnp.bool_)
    for _ in range(k_top):
        g = jnp.where(live, gate, NEG_INF)
        mx = jnp.max(g, axis=1, keepdims=True)
        first = jnp.min(jnp.where(live & (g == mx), lane, n), axis=1, keepdims=True)
        pick = lane == first
        sel = sel | pick
        live = live & jnp.logical_not(pick)
    return sel


HEAD_PAIRS = MIX_HALF // LANES


def _moba_prompt_kernel(q_ref, qb_ref, kb_ref, vb_ref, km_ref, o_ref,
                        qbd_ref, sel_ref, m_ref, l_ref, acc_ref, *, n_blk):
    qi = pl.program_id(1)
    tq = MOBA_BLOCK
    lane = lax.broadcasted_iota(jnp.int32, (tq, LANES), 1)
    krow = lax.broadcasted_iota(jnp.int32, (tq, 2 * tq), 0)
    qcol = lax.broadcasted_iota(jnp.int32, (tq, 2 * tq), 1) % tq
    causal = krow <= qcol
    nt_dims = (((1,), (1,)), ((), ()))
    tn_dims = (((0,), (0,)), ((), ()))
    row0 = pl.multiple_of(qi * tq, tq)

    def attend(pr, r0, mask_of, first):
        ps = slice(pr * LANES, (pr + 1) * LANES)
        kblk = kb_ref[pl.ds(r0, tq), ps]
        vblk = vb_ref[pl.ds(r0, tq), ps]
        for ck in range(2 * tq // LANES):
            cs = slice(ck * LANES, (ck + 1) * LANES)
            hd = ck * LANES // tq
            qs = slice(ck * LANES - hd * tq, (ck + 1) * LANES - hd * tq)
            s = lax.dot_general(kblk, qbd_ref[pr, cs, :], nt_dims, preferred_element_type=F32)
            s = jnp.where(mask_of(cs), s, NEG_INF)
            m_blk = jnp.max(s, axis=0, keepdims=True)
            m_old = m_ref[pr, :, cs]
            m_new = m_blk if first else jnp.maximum(m_old, m_blk)
            p = jnp.exp(s - m_new)
            pv = lax.dot_general(vblk, p.astype(BF16), tn_dims, preferred_element_type=F32)
            pv = pv[hd * HEAD_DIM:(hd + 1) * HEAD_DIM]
            if first:
                l_ref[pr, :, cs] = jnp.sum(p, axis=0, keepdims=True)
                acc_ref[pr, hd, :, qs] = pv
            else:
                alpha = jnp.exp(m_old - m_new)
                l_ref[pr, :, cs] = alpha * l_ref[pr, :, cs] + jnp.sum(p, axis=0, keepdims=True)
                acc_ref[pr, hd, :, qs] = alpha * acc_ref[pr, hd, :, qs] + pv
            m_ref[pr, :, cs] = m_new

    for pr in range(HEAD_PAIRS):
        ps = slice(pr * LANES, (pr + 1) * LANES)
        qb = qb_ref[:, ps]
        zero = jnp.zeros_like(qb)

        def block_diag(t):
            return jnp.concatenate([jnp.where(lane < HEAD_DIM, t, zero), jnp.where(lane >= HEAD_DIM, t, zero)], axis=0)

        qbd_ref[pr] = block_diag(qb)
        q_lo = (q_ref[:, ps] * (HEAD_DIM ** -0.5) - qb.astype(F32)).astype(BF16)
        km = km_ref[:, ps]
        km_hi = km.astype(BF16)
        km_lo = (km - km_hi.astype(F32)).astype(BF16)
        gate = (lax.dot_general(km_hi, qbd_ref[pr], nt_dims, preferred_element_type=F32)
                + lax.dot_general(km_lo, qbd_ref[pr], nt_dims, preferred_element_type=F32)
                + lax.dot_general(km_hi, block_diag(q_lo), nt_dims, preferred_element_type=F32))
        sel_ref[pr] = _top_rows_mask(gate, qi, MOBA_TOPK).astype(F32)
        attend(pr, row0, lambda cs: causal[:, cs], True)

    def blk_body(j, _):
        r0 = pl.multiple_of(j * tq, tq)
        for pr in range(HEAD_PAIRS):
            picked = sel_ref[pr, pl.ds(j, 1), :] > 0.5
            attend(pr, r0, lambda cs, picked=picked: picked[:, cs], False)
        return 0

    lax.fori_loop(0, qi, blk_body, 0)
    for pr in range(HEAD_PAIRS):
        l = l_ref[pr]
        out_t = jnp.concatenate([acc_ref[pr, 0] / l[:, :tq], acc_ref[pr, 1] / l[:, tq:]], axis=0)
        o_ref[:, pr * LANES:(pr + 1) * LANES] = out_t.T


def _moba_prompt(q_rot, qb, kb, vb, kmean, *, n_b, n_s):
    n_blk = n_s // MOBA_BLOCK
    tq = MOBA_BLOCK
    qspec = pl.BlockSpec((tq, MIX_HALF), lambda b, i: (b * n_blk + i, 0))
    kvspec = pl.BlockSpec((n_s, MIX_HALF), lambda b, i: (b, 0))
    return pl.pallas_call(
        functools.partial(_moba_prompt_kernel, n_blk=n_blk),
        grid=(n_b, n_blk),
        in_specs=[qspec, qspec, kvspec, kvspec, pl.BlockSpec((n_blk, MIX_HALF), lambda b, i: (b, 0))],
        out_specs=qspec,
        out_shape=jax.ShapeDtypeStruct((n_b * n_s, MIX_HALF), F32),
        scratch_shapes=[pltpu.VMEM((HEAD_PAIRS, 2 * tq, LANES), BF16),
                        pltpu.VMEM((HEAD_PAIRS, n_blk, 2 * tq), F32),
                        pltpu.VMEM((HEAD_PAIRS, 1, 2 * tq), F32),
                        pltpu.VMEM((HEAD_PAIRS, 1, 2 * tq), F32),
                        pltpu.VMEM((HEAD_PAIRS, 2, HEAD_DIM, tq), F32)],
        compiler_params=_cparams(("arbitrary", "arbitrary")),
    )(q_rot, qb, kb, vb, kmean)


PAGES_PER_STEP = 16
BLOCK_PAGES = MOBA_BLOCK // PAGE_SIZE


def _moba_sample_kernel(pt_ref, *refs, n_blk, n_q):
    del pt_ref
    kp = refs[:PAGES_PER_STEP]
    vp = refs[PAGES_PER_STEP:2 * PAGES_PER_STEP]
    qbt_ref, qbtf_ref, kn_ref, vn_ref, o_ref, oacc_ref, m_ref, l_ref, km_ref = refs[2 * PAGES_PER_STEP:]
    c = pl.program_id(1)
    ncol = ATT_HEADS * n_q
    blocks_per_step = PAGES_PER_STEP // BLOCK_PAGES
    nt_dims = (((1,), (1,)), ((), ()))
    qbt = qbt_ref[0]
    lane_c = lax.broadcasted_iota(jnp.int32, (ncol, LANES), 1)
    lane_k = lax.broadcasted_iota(jnp.int32, (MIX_HALF, LANES), 1)

    @pl.when(c == 0)
    def _():
        m_ref[...] = jnp.zeros(m_ref.shape, F32)
        l_ref[...] = jnp.zeros(l_ref.shape, F32)
        km_ref[...] = jnp.zeros(km_ref.shape, F32)

    kt_all = jnp.concatenate([kp[j][0, 0] for j in range(PAGES_PER_STEP)], axis=1)
    s_all = jnp.dot(qbt, kt_all.astype(BF16), preferred_element_type=F32)
    for blk in range(blocks_per_step):
        n = c * blocks_per_step + blk
        ks = slice(blk * MOBA_BLOCK, (blk + 1) * MOBA_BLOCK)
        vt = jnp.concatenate([vp[blk * BLOCK_PAGES + j][0, 0] for j in range(BLOCK_PAGES)], axis=1)
        kmean = jnp.sum(kt_all[:, ks], axis=1, keepdims=True) * (1.0 / MOBA_BLOCK)
        km_ref[...] = jnp.where(lane_k == n, kmean, km_ref[...])
        s = s_all[:, ks]
        m = jnp.max(s, axis=1, keepdims=True)
        p = jnp.exp(s - m)
        m_ref[...] = jnp.where(lane_c == n, m, m_ref[...])
        l_ref[...] = jnp.where(lane_c == n, jnp.sum(p, axis=1, keepdims=True), l_ref[...])
        oacc_ref[n] = lax.dot_general(p.astype(BF16), vt.astype(BF16), nt_dims,
                                      preferred_element_type=F32)

    @pl.when(c == pl.num_programs(1) - 1)
    def _():
        q_lo = (qbtf_ref[0] - qbt.astype(F32)).astype(BF16)
        km = km_ref[...]
        km_hi = km.astype(BF16)
        km_lo = (km - km_hi.astype(F32)).astype(BF16)
        gate = (jnp.dot(qbt, km_hi, preferred_element_type=F32) + jnp.dot(q_lo, km_hi, preferred_element_type=F32)
                + jnp.dot(qbt, km_lo, preferred_element_type=F32))
        sel = _top_lanes_mask(gate, n_blk, MOBA_TOPK)
        s_own = lax.dot_general(qbt, kn_ref[...].astype(BF16), nt_dims, preferred_element_type=F32)
        qidx = lax.broadcasted_iota(jnp.int32, (ncol, n_q), 0) % n_q
        kidx = lax.broadcasted_iota(jnp.int32, (ncol, n_q), 1)
        s_own = jnp.where(kidx <= qidx, s_own, NEG_INF)
        m_all = m_ref[...]
        m_fin = jnp.maximum(jnp.max(jnp.where(sel, m_all, NEG_INF), axis=1, keepdims=True),
                            jnp.max(s_own, axis=1, keepdims=True))
        w = jnp.where(sel, jnp.exp(m_all - m_fin), 0.0)
        p_own = jnp.exp(s_own - m_fin)
        l_fin = jnp.sum(w * l_ref[...], axis=1, keepdims=True) + jnp.sum(p_own, axis=1, keepdims=True)
        w = w / l_fin
        p_own = p_own / l_fin
        acc = jnp.dot(p_own, vn_ref[...], preferred_element_type=F32)
        for n in range(n_blk):
            acc = acc + w[:, n:n + 1] * oacc_ref[n]
        head = lax.broadcasted_iota(jnp.int32, (n_q, MIX_HALF), 1) // HEAD_DIM
        out = jnp.zeros((n_q, MIX_HALF), F32)
        for h in range(ATT_HEADS):
            out = out + jnp.where(head == h, acc[h * n_q:(h + 1) * n_q], 0.0)
        o_ref[...] = out


def _moba_sample(page_table, cache_k, cache_v, layer_i, q_rot, k_rot, proj, *, n_b, n_q, row0):
    n_pages = page_table.shape[1]
    n_blk = n_pages // BLOCK_PAGES
    ncol = ATT_HEADS * n_q
    assert n_blk <= LANES and n_pages % PAGES_PER_STEP == 0 and row0 % n_q == 0
    blk0 = row0 // n_q
    q4 = (q_rot[row0:row0 + n_b * n_q] * (HEAD_DIM ** -0.5)).reshape(n_b, n_q, ATT_HEADS, HEAD_DIM)
    qbtf = jnp.einsum('bihd,hg->bhigd', q4, jnp.eye(ATT_HEADS, dtype=F32)).reshape(n_b, ncol, MIX_HALF)

    def page_spec(j):
        return pl.BlockSpec((1, 1, MIX_HALF, PAGE_SIZE),
                            lambda b, c, pt, j=j: (layer_i, pt[b, c * PAGES_PER_STEP + j], 0, 0))

    per_b3 = lambda shape: pl.BlockSpec(shape, lambda b, c, pt: (b, 0, 0))
    grid_spec = pltpu.PrefetchScalarGridSpec(
        num_scalar_prefetch=1,
        grid=(n_b, n_pages // PAGES_PER_STEP),
        in_specs=[page_spec(j) for j in range(PAGES_PER_STEP)] * 2
        + [per_b3((1, ncol, MIX_HALF)), per_b3((1, ncol, MIX_HALF)),
           pl.BlockSpec((n_q, MIX_HALF), lambda b, c, pt: (blk0 + b, 0)),
           pl.BlockSpec((n_q, MIX_HALF), lambda b, c, pt: (blk0 + b, 3))],
        out_specs=pl.BlockSpec((n_q, MIX_HALF), lambda b, c, pt: (b, 0)),
        scratch_shapes=[pltpu.VMEM((n_blk, ncol, MIX_HALF), F32),
                        pltpu.VMEM((ncol, LANES), F32), pltpu.VMEM((ncol, LANES), F32),
                        pltpu.VMEM((MIX_HALF, LANES), F32)])
    return pl.pallas_call(
        functools.partial(_moba_sample_kernel, n_blk=n_blk, n_q=n_q),
        grid_spec=grid_spec,
        out_shape=jax.ShapeDtypeStruct((n_b * n_q, MIX_HALF), F32),
        compiler_params=_cparams(("arbitrary", "arbitrary")),
    )(page_table, *([cache_k] * PAGES_PER_STEP), *([cache_v] * PAGES_PER_STEP), qbtf.astype(BF16), qbtf,
      k_rot, proj)


TOK_TILE = 256
SEG_ALIGN = SUBLANES
ASG_TILE = -(-(TOK_TILE * TOPK + N_EXPERTS * (SEG_ALIGN - 1)) // LANES) * LANES
EXP_TILE = 512


def _layer_norm_rows(h, g, b):
    mu = jnp.mean(h, axis=-1, keepdims=True)
    hc = h - mu
    var = jnp.mean(hc * hc, axis=-1, keepdims=True)
    return hc * lax.rsqrt(var + LN_EPS) * g + b


def _route_kernel(x_ref, ap_ref, as_ref, bp_ref, bs_ref, wo_ref, g_ref, bt_ref, wrh_ref, wrl_ref, br_ref,
                  x1_ref, xs_ref, dg_ref, cnt_ref, *, alpha, n_prompt_tiles):
    is_sample = pl.program_id(0) >= n_prompt_tiles
    a = jnp.where(is_sample, as_ref[...], ap_ref[...])
    b = jnp.where(is_sample, bs_ref[...], bp_ref[...])
    h = (alpha * x_ref[...]
         + jnp.dot(a.astype(BF16), wo_ref[0:MIX_HALF, :], preferred_element_type=F32)
         + jnp.dot(b.astype(BF16), wo_ref[MIX_HALF:, :], preferred_element_type=F32))
    x1 = _layer_norm_rows(h, g_ref[...], bt_ref[...])
    x1_ref[...] = x1
    x1h = x1.astype(BF16)
    x1l = (x1 - x1h.astype(F32)).astype(BF16)
    nt_dims = (((1,), (1,)), ((), ()))
    logits = (lax.dot_general(wrh_ref[...], x1h, nt_dims, preferred_element_type=F32)
              + lax.dot_general(wrh_ref[...], x1l, nt_dims, preferred_element_type=F32)
              + lax.dot_general(wrl_ref[...], x1h, nt_dims, preferred_element_type=F32)) + br_ref[...]
    row = lax.broadcasted_iota(jnp.int32, logits.shape, 0)
    g = logits
    picks, vals = [], []
    for _ in range(TOPK):
        mx = jnp.max(g, axis=0, keepdims=True)
        first = jnp.min(jnp.where(g == mx, row, N_EXPERTS), axis=0, keepdims=True)
        pick = row == first
        picks.append(pick)
        vals.append(mx)
        g = jnp.where(pick, NEG_INF, g)
    es = [jnp.exp(v - vals[0]) for v in vals]
    den = es[0] + es[1] + es[2] + es[3]
    onehot = [p.astype(F32) for p in picks]
    member = onehot[0] + onehot[1] + onehot[2] + onehot[3]
    t_r = lax.broadcasted_iota(jnp.int32, (TOK_TILE, TOK_TILE), 0)
    t_c = lax.broadcasted_iota(jnp.int32, (TOK_TILE, TOK_TILE), 1)
    before = (t_r < t_c).astype(BF16)
    rank = jnp.dot(member.astype(BF16), before, preferred_element_type=F32)
    cnt = jnp.sum(member, axis=1, keepdims=True)
    cnt = jnp.ceil(cnt * (1.0 / SEG_ALIGN)) * SEG_ALIGN
    e_r = lax.broadcasted_iota(jnp.int32, (N_EXPERTS, N_EXPERTS), 0)
    e_c = lax.broadcasted_iota(jnp.int32, (N_EXPERTS, N_EXPERTS), 1)
    lower = (e_c < e_r).astype(BF16)
    off = jnp.dot(lower, jnp.broadcast_to(cnt, (N_EXPERTS, TOK_TILE)).astype(BF16), preferred_element_type=F32)
    slot = off + rank
    dests = [jnp.sum(oh * slot, axis=0, keepdims=True) for oh in onehot]
    r_iota = lax.broadcasted_iota(jnp.int32, (ASG_TILE, TOK_TILE), 0)
    perm = jnp.zeros((ASG_TILE, TOK_TILE), F32)
    for d in dests:
        perm = jnp.where(r_iota == d.astype(jnp.int32), 1.0, perm)
    xs_ref[...] = jnp.dot(perm.astype(BF16), x1h, preferred_element_type=F32)
    dg_ref[0] = jnp.concatenate(dests + [e / den for e in es], axis=0)
    cnt_ref[0] = jnp.broadcast_to(cnt, (N_EXPERTS, LANES))


def _route(x, mix_a, mix_b, w_out_bf, ln_g, ln_b, wr_t, br, *, alpha):
    t = x.shape[0]
    nt = t // TOK_TILE
    npt = mix_a[0].shape[0] // TOK_TILE
    assert mix_a[0].shape[0] % TOK_TILE == 0 and mix_a[1].shape[0] == (nt - npt) * TOK_TILE
    full2 = lambda a: pl.BlockSpec(a.shape, lambda i: (0, 0))
    prompt_rows = pl.BlockSpec((TOK_TILE, MIX_HALF), lambda i: (jnp.minimum(i, npt - 1), 0))
    sample_rows = pl.BlockSpec((TOK_TILE, MIX_HALF), lambda i: (jnp.maximum(i - npt, 0), 0))
    wr_hi = wr_t.astype(BF16)
    wr_lo = (wr_t - wr_hi.astype(F32)).astype(BF16)
    return pl.pallas_call(
        functools.partial(_route_kernel, alpha=alpha, n_prompt_tiles=npt),
        grid=(nt,),
        in_specs=[pl.BlockSpec((TOK_TILE, D_MODEL), lambda i: (i, 0)),
                  prompt_rows, sample_rows, prompt_rows, sample_rows,
                  full2(w_out_bf), full2(ln_g), full2(ln_b), full2(wr_hi), full2(wr_lo), full2(br)],
        out_specs=[pl.BlockSpec((TOK_TILE, D_MODEL), lambda i: (i, 0)),
                   pl.BlockSpec((ASG_TILE, D_MODEL), lambda i: (i, 0)),
                   pl.BlockSpec((1, 2 * TOPK, TOK_TILE), lambda i: (i, 0, 0)),
                   pl.BlockSpec((1, N_EXPERTS, LANES), lambda i: (i, 0, 0))],
        out_shape=[jax.ShapeDtypeStruct((t, D_MODEL), F32),
                   jax.ShapeDtypeStruct((nt * ASG_TILE, D_MODEL), F32),
                   jax.ShapeDtypeStruct((nt, 2 * TOPK, TOK_TILE), F32),
                   jax.ShapeDtypeStruct((nt, N_EXPERTS, LANES), F32)],
        compiler_params=_cparams(("parallel",)),
    )(x, mix_a[0], mix_a[1], mix_b[0], mix_b[1], w_out_bf, ln_g, ln_b, wr_hi, wr_lo, br)


SEG_PIECES = tuple(1 << b for b in range(TOK_TILE.bit_length() - 1, SEG_ALIGN.bit_length() - 2, -1))
TILE_PIECES = tuple(1 << b for b in range(EXP_TILE.bit_length() - 1, SEG_ALIGN.bit_length() - 2, -1))


def _expert_kernel(te_ref, nu_ref, lo_ref, hi_ref, valid_ref, src_ref, dst_ref, len_ref, tot_ref,
                   xs_ref, wg_ref, bg_ref, wu_ref, bu_ref, wd_ref, bd_ref, ys_ref,
                   xbuf, ybuf, zbuf, wgb_ref, wub_ref, wdb_ref, gsem, ssem, zsem, *, n_tok_tiles):
    m = pl.program_id(0)
    n_used = nu_ref[0]

    def zero_tail(i, start):
        n = ASG_TILE - tot_ref[i]
        for size in SEG_PIECES:
            @pl.when((n & size) != 0)
            def _():
                row = pl.multiple_of(i * ASG_TILE + tot_ref[i] + (n & ~(2 * size - 1)), SEG_ALIGN)
                cp = pltpu.make_async_copy(zbuf.at[pl.ds(0, size)], ys_ref.at[pl.ds(row, size)], zsem)
                cp.start() if start else cp.wait()

    @pl.when(m == 0)
    def _():
        zbuf[...] = jnp.zeros(zbuf.shape, F32)

    @pl.when((m >= 1) & (m - 1 < n_tok_tiles))
    def _():
        zero_tail(m - 1, False)

    @pl.when(m < n_tok_tiles)
    def _():
        zero_tail(m, True)

    def for_pieces(t, fn):
        row0 = t * EXP_TILE

        def seg_body(s, _):
            start = jnp.maximum(dst_ref[s], row0)
            n = jnp.minimum(dst_ref[s] + len_ref[s], row0 + EXP_TILE) - start
            base_src = src_ref[s] + (start - dst_ref[s])
            base_dst = start - row0
            for size in SEG_PIECES:
                @pl.when((n & size) != 0)
                def _():
                    done = n & ~(2 * size - 1)
                    fn(pl.multiple_of(base_src + done, SEG_ALIGN), pl.multiple_of(base_dst + done, SEG_ALIGN), size)
            return 0

        lax.fori_loop(lo_ref[t], hi_ref[t], seg_body, 0)

    def wait_rows(n, copy_of):
        for size in TILE_PIECES:
            @pl.when((n & size) != 0)
            def _():
                copy_of(size).wait()

    def gather(t, start):
        slot = t % 2
        if start:
            for_pieces(t, lambda row, r, size: pltpu.make_async_copy(
                xs_ref.at[pl.ds(row, size)], xbuf.at[slot, pl.ds(r, size)], gsem.at[slot]).start())
        else:
            wait_rows(valid_ref[t], lambda size: pltpu.make_async_copy(
                xs_ref.at[pl.ds(0, size)], xbuf.at[slot, pl.ds(0, size)], gsem.at[slot]))

    def scatter(t, start):
        slot = t % 2
        if start:
            for_pieces(t, lambda row, r, size: pltpu.make_async_copy(
                ybuf.at[slot, pl.ds(r, size)], ys_ref.at[pl.ds(row, size)], ssem.at[slot]).start())
        else:
            wait_rows(valid_ref[t], lambda size: pltpu.make_async_copy(
                ybuf.at[slot, pl.ds(0, size)], ys_ref.at[pl.ds(0, size)], ssem.at[slot]))

    @pl.when((m == 0) & (n_used > 0))
    def _():
        gather(0, True)

    @pl.when(m + 1 < n_used)
    def _():
        gather(m + 1, True)

    @pl.when((m >= 2) & (m - 2 < n_used))
    def _():
        scatter(m - 2, False)

    @pl.when(m < n_used)
    def _():
        gather(m, False)
        prev = te_ref[jnp.maximum(m - 1, 0)]

        @pl.when((m == 0) | (te_ref[m] != prev))
        def _():
            wgb_ref[...] = wg_ref[0, 0].astype(BF16)
            wub_ref[...] = wu_ref[0, 0].astype(BF16)
            wdb_ref[...] = wd_ref[0, 0].astype(BF16)

        slot = m % 2
        xb = xbuf[slot].astype(BF16)
        gt = jnp.minimum(jnp.dot(xb, wgb_ref[...], preferred_element_type=F32) + bg_ref[0, 0], SWIGLU_LIMIT)
        up = jnp.clip(jnp.dot(xb, wub_ref[...], preferred_element_type=F32) + bu_ref[0, 0],
                      -SWIGLU_LIMIT, SWIGLU_LIMIT)
        act = gt * _sigmoid(SWIGLU_ALPHA * gt) * (up + 1.0)
        ybuf[slot] = jnp.dot(act.astype(BF16), wdb_ref[...], preferred_element_type=F32) + bd_ref[0, 0]
        scatter(m, True)


def _experts(tile_expert, n_used, seg_lo, seg_hi, tile_valid, seg_src, seg_dst, seg_len, tile_total, xs, layer,
             w_gate, b_gate, w_up, b_up, w_down, b_down):
    n_tiles = tile_expert.shape[0]
    n_tok_tiles = tile_total.shape[0]
    assert n_tiles > n_tok_tiles
    wspec = pl.BlockSpec((1, 1, D_MODEL, D_MODEL), lambda m, te, *_: (layer, te[m], 0, 0))
    bspec = pl.BlockSpec((1, 1, 1, D_MODEL), lambda m, te, *_: (layer, te[m], 0, 0))
    grid_spec = pltpu.PrefetchScalarGridSpec(
        num_scalar_prefetch=9, grid=(n_tiles,),
        in_specs=[pl.BlockSpec(memory_space=pl.ANY), wspec, bspec, wspec, bspec, wspec, bspec],
        out_specs=pl.BlockSpec(memory_space=pl.ANY),
        scratch_shapes=[pltpu.VMEM((2, EXP_TILE, D_MODEL), F32), pltpu.VMEM((2, EXP_TILE, D_MODEL), F32),
                        pltpu.VMEM((TOK_TILE, D_MODEL), F32)]
        + [pltpu.VMEM((D_MODEL, D_MODEL), BF16)] * 3
        + [pltpu.SemaphoreType.DMA((2,)), pltpu.SemaphoreType.DMA((2,)), pltpu.SemaphoreType.DMA(())])
    depth = w_gate.shape[0]
    bshape = (depth, N_EXPERTS, 1, D_MODEL)
    return pl.pallas_call(
        functools.partial(_expert_kernel, n_tok_tiles=n_tok_tiles),
        grid_spec=grid_spec,
        out_shape=jax.ShapeDtypeStruct(xs.shape, F32),
        compiler_params=_cparams(("arbitrary",)),
    )(tile_expert, n_used, seg_lo, seg_hi, tile_valid, seg_src, seg_dst, seg_len, tile_total, xs,
      w_gate, b_gate.reshape(bshape), w_up, b_up.reshape(bshape), w_down, b_down.reshape(bshape))


def _combine_kernel(x1_ref, ys_ref, dg_ref, g_ref, bt_ref, o_ref, *, alpha):
    dg = dg_ref[0]
    r_iota = lax.broadcasted_iota(jnp.int32, (ASG_TILE, TOK_TILE), 0)
    comb = jnp.zeros((ASG_TILE, TOK_TILE), F32)
    for k in range(TOPK):
        comb = jnp.where(r_iota == dg[k:k + 1, :].astype(jnp.int32), dg[TOPK + k:TOPK + k + 1, :], comb)
    ffn = lax.dot_general(comb.astype(BF16), ys_ref[...].astype(BF16), (((0,), (0,)), ((), ())),
                          preferred_element_type=F32)
    o_ref[...] = _layer_norm_rows(alpha * x1_ref[...] + ffn, g_ref[...], bt_ref[...])


def _combine(x1, ys, dg, ln_g, ln_b, *, alpha):
    t = x1.shape[0]
    nt = t // TOK_TILE
    return pl.pallas_call(
        functools.partial(_combine_kernel, alpha=alpha),
        grid=(nt,),
        in_specs=[pl.BlockSpec((TOK_TILE, D_MODEL), lambda i: (i, 0)),
                  pl.BlockSpec((ASG_TILE, D_MODEL), lambda i: (i, 0)),
                  pl.BlockSpec((1, 2 * TOPK, TOK_TILE), lambda i: (i, 0, 0)),
                  pl.BlockSpec((1, D_MODEL), lambda i: (0, 0)),
                  pl.BlockSpec((1, D_MODEL), lambda i: (0, 0))],
        out_specs=pl.BlockSpec((TOK_TILE, D_MODEL), lambda i: (i, 0)),
        out_shape=jax.ShapeDtypeStruct((t, D_MODEL), F32),
        compiler_params=_cparams(("parallel",)),
    )(x1, ys, dg, ln_g, ln_b)


def _channel_mix(x, mix_a, mix_b, p, layer, alpha):
    t = x.shape[0]
    nt = t // TOK_TILE
    x1, xs, dg, cnt = _route(
        x, mix_a, mix_b, p['w_out'][layer].astype(BF16), p['ln_g'][layer, 0].reshape(1, D_MODEL),
        p['ln_b'][layer, 0].reshape(1, D_MODEL), p['router_w'][layer].T.astype(F32),
        p['router_b'][layer].reshape(N_EXPERTS, 1).astype(F32), alpha=alpha)
    cnt = cnt[:, :, 0].astype(jnp.int32)
    local_off = jnp.cumsum(cnt, axis=1) - cnt
    tile_rows = (jnp.arange(nt, dtype=jnp.int32) * ASG_TILE)[:, None] + local_off
    total = jnp.sum(cnt, axis=0)
    padded = (total + EXP_TILE - 1) // EXP_TILE * EXP_TILE
    pend = jnp.cumsum(padded)
    expert_rows = (pend - padded)[None, :] + jnp.cumsum(cnt, axis=0) - cnt
    n_tiles = -(-(nt * (TOK_TILE * TOPK + N_EXPERTS * (SEG_ALIGN - 1))) // EXP_TILE) + N_EXPERTS + 2
    tile_start = (jnp.arange(n_tiles, dtype=jnp.int32) * EXP_TILE)[:, None]
    count_below = lambda a, bound: jnp.sum((a[None, :] < bound).astype(jnp.int32), axis=1)
    tile_expert = jnp.minimum(count_below(pend, tile_start + 1), N_EXPERTS - 1)
    n_used = (pend[-1:] // EXP_TILE).astype(jnp.int32)
    seg_src = tile_rows.T.reshape(-1)
    seg_dst = expert_rows.T.reshape(-1).astype(jnp.int32)
    seg_len = cnt.T.reshape(-1)
    seg_lo = count_below(seg_dst + seg_len, tile_start + 1)
    seg_hi = count_below(seg_dst, tile_start + EXP_TILE)
    expert_end = (pend - padded + total)[tile_expert]
    tile_valid = jnp.clip(expert_end - tile_start[:, 0], 0, EXP_TILE).astype(jnp.int32)
    ys = _experts(tile_expert, n_used, seg_lo, seg_hi, tile_valid, seg_src, seg_dst, seg_len, jnp.sum(cnt, axis=1),
                  xs, layer,
                  p['moe_w_gate'], p['moe_b_gate'], p['moe_w_up'], p['moe_b_up'], p['moe_w_down'], p['moe_b_down'])
    return _combine(x1, ys, dg, p['ln_g'][layer, 1].reshape(1, D_MODEL), p['ln_b'][layer, 1].reshape(1, D_MODEL),
                    alpha=alpha)


def kernel(x_prompt, x_sample, state_s5, state_pool, cache_k, cache_v, page_table, w_in_ab, s5_lambda_re, s5_lambda_im, s5_b_re, s5_b_im, s5_c_re, s5_c_im, s5_d, s5_log_dt, s5_w_glu, s5_b_glu, gm_norm_g, gm_w_s, gm_b_s, w_in_cd, pool_w, pool_scale, w_out, ln_g, ln_b, router_w, router_b, moe_w_gate, moe_b_gate, moe_w_up, moe_b_up, moe_w_down, moe_b_down):
    p = dict(w_in_ab=w_in_ab, s5_lambda_re=s5_lambda_re, s5_lambda_im=s5_lambda_im,
             s5_b_re=s5_b_re, s5_b_im=s5_b_im, s5_c_re=s5_c_re, s5_c_im=s5_c_im, s5_d=s5_d,
             s5_log_dt=s5_log_dt, s5_w_glu=s5_w_glu, s5_b_glu=s5_b_glu, gm_norm_g=gm_norm_g,
             gm_w_s=gm_w_s, gm_b_s=gm_b_s, w_in_cd=w_in_cd, pool_w=pool_w, pool_scale=pool_scale,
             w_out=w_out, ln_g=ln_g, ln_b=ln_b, router_w=router_w, router_b=router_b,
             moe_w_gate=moe_w_gate, moe_b_gate=moe_b_gate, moe_w_up=moe_w_up, moe_b_up=moe_b_up,
             moe_w_down=moe_w_down, moe_b_down=moe_b_down)
    n_bp, n_sp, _ = x_prompt.shape
    n_bs, n_ss, _ = x_sample.shape
    t_p, t_s = n_bp * n_sp, n_bs * n_ss
    depth = w_out.shape[0]
    alpha = (2 * depth) ** 0.25
    past_len = page_table.shape[1] * PAGE_SIZE
    x = jnp.concatenate([x_prompt.reshape(t_p, D_MODEL), x_sample.reshape(t_s, D_MODEL)], axis=0)
    zero_s5 = jnp.zeros((n_bp, S5_GROUPS, S5_STATE, 2), F32)
    pos = jnp.concatenate([jnp.tile(jnp.arange(n_sp), n_bp), jnp.tile(past_len + jnp.arange(n_ss), n_bs)])
    rope_tables = _rope_tables(pos)
    cache_k2 = cache_k.transpose(0, 1, 3, 4, 2).reshape(cache_k.shape[0], cache_k.shape[1], MIX_HALF, PAGE_SIZE)
    cache_v2 = cache_v.transpose(0, 1, 3, 4, 2).reshape(cache_v.shape[0], cache_v.shape[1], MIX_HALF, PAGE_SIZE)
    s5_p, s5_s, gmv_s, pool_p, pool_s, k_p, v_p, k_s, v_s = [], [], [], [], [], [], [], [], []
    for layer in range(depth):
        i = layer // 2
        if layer % 2 == 0:
            proj = _proj(x, w_in_ab[i].astype(BF16), PROJ_TILE)
            a_p, b_p, _, st_p = _even_layer_mix(proj, zero_s5, p, i, n_b=n_bp, n_s=n_sp, sample=False)
            a_s, b_s, vn, st_s = _even_layer_mix(proj, state_s5[i], p, i, n_b=n_bs, n_s=n_ss, sample=True, row0=t_p)
            s5_p.append(st_p)
            s5_s.append(st_s)
            gmv_s.append(vn.reshape(n_bs, n_ss, MIX_HALF))
        else:
            proj = _proj(x, w_in_cd[i].astype(BF16), PROJ_TILE)
            pw = pool_w[i].astype(BF16)
            ps = pool_scale[i].reshape(1, MIX_HALF).astype(F32)
            a_p = _pool_mixer(proj, jnp.zeros((n_bp, POOL_HALO, MIX_HALF), F32), pw, ps, n_b=n_bp, n_s=n_sp, base=0)
            halo = jnp.concatenate([jnp.zeros((n_bs, POOL_HALO - POOL_BUF, MIX_HALF), F32),
                                    state_pool[i].astype(F32)], axis=1)
            a_s = _pool_mixer(proj, halo, pw, ps, n_b=n_bs, n_s=n_ss, base=POOL_BUF, row0=t_p)
            q_rot, k_rot, qb, kb, vb, kmean = _rope(proj, rope_tables)
            b_p = _moba_prompt(q_rot, qb, kb, vb, kmean, n_b=n_bp, n_s=n_sp)
            b_s = _moba_sample(page_table, cache_k2, cache_v2, i, q_rot, k_rot, proj, n_b=n_bs, n_q=n_ss, row0=t_p)
            c_p = proj[:t_p, :MIX_HALF].reshape(n_bp, n_sp, MIX_HALF)
            c_s = proj[t_p:, :MIX_HALF].reshape(n_bs, n_ss, MIX_HALF)
            pool_p.append(c_p[:, -POOL_BUF:])
            pool_s.append(jnp.concatenate([state_pool[i].astype(F32), c_s], axis=1)[:, -POOL_BUF:])
            k_p.append(k_rot[:t_p].reshape(n_bp, n_sp, ATT_HEADS, HEAD_DIM))
            k_s.append(k_rot[t_p:].reshape(n_bs, n_ss, ATT_HEADS, HEAD_DIM))
            v_p.append(proj[:t_p, 3 * MIX_HALF:].reshape(n_bp, n_sp, ATT_HEADS, HEAD_DIM))
            v_s.append(proj[t_p:, 3 * MIX_HALF:].reshape(n_bs, n_ss, ATT_HEADS, HEAD_DIM))
        x = _channel_mix(x, (a_p, a_s), (b_p, b_s), p, layer, alpha)
    return (x[:t_p].reshape(n_bp, n_sp, D_MODEL), x[t_p:].reshape(n_bs, n_ss, D_MODEL),
            jnp.stack(s5_p), jnp.stack(s5_s), jnp.stack(gmv_s), jnp.stack(pool_p), jnp.stack(pool_s),
            jnp.stack(k_p), jnp.stack(v_p), jnp.stack(k_s), jnp.stack(v_s))
```
